```python
import jax, jax.numpy as jnp
from jax import lax
import numpy as np

D_MODEL = 1024
BATCH = 4
SEQ = 4096
DEPTH = 2

N_MIXERS = 2
N_MLSTM_LAYERS = (DEPTH + 1) // 2
N_RWKV_LAYERS = DEPTH // 2
DN_ALPHA = (2.0 * DEPTH) ** 0.25
DN_BETA = (8.0 * DEPTH) ** -0.25
LN_EPS = 1e-5
D_FF = 4 * D_MODEL

M_HEADS = 4
M_DV = D_MODEL // M_HEADS
M_DK = M_DV // 2
M_CHUNK = 64
M_CONV = 4
M_GATE_CAP = 15.0
M_QK = 2 * M_HEADS * M_DK
M_PROJ = M_QK + 2 * M_HEADS * M_DV + 2 * M_HEADS

R_N = 64
R_HEADS = D_MODEL // R_N
R_LW = D_MODEL // 16
R_LA = D_MODEL // 16
R_LG = D_MODEL // 8
R_PROJ = 3 * D_MODEL + R_LW + R_LA + R_LG
R_GN_EPS = 64e-5

kernel_name = 'hybrid_mlstm_rwkv7_deepnorm'


def layer_norm(x, g, b):
    xf = x.astype(jnp.float32)
    mu = jnp.mean(xf, -1, keepdims=True)
    var = jnp.mean(jnp.square(xf - mu), -1, keepdims=True)
    return ((xf - mu) * lax.rsqrt(var + LN_EPS) * g + b).astype(x.dtype)


def causal_depthwise_conv(x, w, b):
    k = w.shape[0]
    y = lax.conv_general_dilated(x, w[:, None, :].astype(x.dtype), window_strides=(1,),
                                 padding=[(k - 1, 0)], dimension_numbers=('NWC', 'WIO', 'NWC'),
                                 feature_group_count=x.shape[-1])
    return y + b


def token_shift(x):
    return jnp.pad(x, ((0, 0), (1, 0), (0, 0)))[:, :-1]


def soft_cap(x, cap):
    return cap * jnp.tanh(x / cap)


def mlstm_chunkwise(q, k, v, log_i, log_f):
    bsz, seq = q.shape[0], q.shape[1]
    nc = seq // M_CHUNK

    def chunks(t):
        return t.reshape(bsz, nc, M_CHUNK, M_HEADS, -1).transpose(1, 0, 3, 2, 4)

    def gchunks(t):
        return t.reshape(bsz, nc, M_CHUNK, M_HEADS).transpose(1, 0, 3, 2)

    causal = jnp.tril(jnp.ones((M_CHUNK, M_CHUNK), dtype=bool))

    def step(carry, inp):
        c_prev, n_prev, m_prev = carry
        qc, kc, vc, ic, fc = inp
        bcum = jnp.cumsum(fc, axis=-1)
        log_d = bcum[..., :, None] - bcum[..., None, :] + ic[..., None, :]
        log_d = jnp.where(causal, log_d, -jnp.inf)
        log_inter = bcum + m_prev[..., None]
        m_row = jnp.maximum(jnp.max(log_d, -1), log_inter)
        d_mat = jnp.exp(log_d - m_row[..., None])
        inter = jnp.exp(log_inter - m_row)
        s = jnp.einsum('bhld,bhsd->bhls', qc, kc) * d_mat
        num = jnp.einsum('bhls,bhse->bhle', s, vc) + inter[..., None] * jnp.einsum('bhld,bhde->bhle', qc, c_prev)
        den = jnp.sum(s, -1) + inter * jnp.einsum('bhld,bhd->bhl', qc, n_prev)
        hc = num / jnp.maximum(jnp.abs(den), jnp.exp(-m_row))[..., None]
        b_last = bcum[..., -1]
        log_w = b_last[..., None] - bcum + ic
        m_new = jnp.maximum(b_last + m_prev, jnp.max(log_w, -1))
        decay = jnp.exp(b_last + m_prev - m_new)
        wk = jnp.exp(log_w - m_new[..., None])
        c_new = decay[..., None, None] * c_prev + jnp.einsum('bhl,bhld,bhle->bhde', wk, kc, vc)
        n_new = decay[..., None] * n_prev + jnp.einsum('bhl,bhld->bhd', wk, kc)
        return (c_new, n_new, m_new), hc

    init = (jnp.zeros((bsz, M_HEADS, M_DK, M_DV), jnp.float32),
            jnp.zeros((bsz, M_HEADS, M_DK), jnp.float32),
            jnp.zeros((bsz, M_HEADS), jnp.float32))
    _, h = lax.scan(step, init, (chunks(q), chunks(k), chunks(v), gchunks(log_i), gchunks(log_f)))
    return h.transpose(1, 0, 3, 2, 4).reshape(bsz, seq, M_HEADS, M_DV)


def mlstm_mixer(x, w_in, b_i, b_f, conv_w, conv_b, norm_g, w_out):
    bsz, seq, _ = x.shape
    f32 = jnp.float32
    proj = x @ w_in
    qk = jax.nn.silu(causal_depthwise_conv(proj[..., :M_QK], conv_w, conv_b)).astype(f32)
    q = qk[..., :M_QK // 2].reshape(bsz, seq, M_HEADS, M_DK) * (M_DK ** -0.5)
    k = qk[..., M_QK // 2:].reshape(bsz, seq, M_HEADS, M_DK)
    o0 = M_QK
    o1 = o0 + M_HEADS * M_DV
    o2 = o1 + M_HEADS * M_DV
    v = proj[..., o0:o1].astype(f32).reshape(bsz, seq, M_HEADS, M_DV)
    o_gate = jax.nn.sigmoid(proj[..., o1:o2].astype(f32))
    gates = proj[..., o2:].astype(f32)
    log_i = soft_cap(gates[..., :M_HEADS] + b_i, M_GATE_CAP)
    log_f = jax.nn.log_sigmoid(soft_cap(gates[..., M_HEADS:] + b_f, M_GATE_CAP))
    h = mlstm_chunkwise(q, k, v, log_i, log_f)
    h = h * lax.rsqrt(jnp.mean(jnp.square(h), -1, keepdims=True) + 1e-6)
    h = h.reshape(bsz, seq, M_HEADS * M_DV) * norm_g * o_gate
    return h.astype(x.dtype) @ w_out


def wkv7_scan(r, w, k, v, a, b):
    bsz = r.shape[0]

    def step(state, inp):
        rt, wt, kt, vt, at, bt = inp
        sa = jnp.einsum('bhvk,bhk->bhv', state, at)
        state = state * wt[:, :, None, :] + sa[..., None] * bt[:, :, None, :] + vt[..., None] * kt[:, :, None, :]
        return state, jnp.einsum('bhvk,bhk->bhv', state, rt)

    init = jnp.zeros((bsz, R_HEADS, R_N, R_N), jnp.float32)
    xs = (jnp.swapaxes(r, 0, 1), jnp.swapaxes(w, 0, 1), jnp.swapaxes(k, 0, 1),
          jnp.swapaxes(v, 0, 1), jnp.swapaxes(a, 0, 1), jnp.swapaxes(b, 0, 1))
    _, y = lax.scan(step, init, xs)
    return jnp.swapaxes(y, 0, 1)


def rwkv7_mixer(x, w_in, mu, w0, w2, a0, a2, g2, k_k, k_a, r_k, gn_g, gn_b, w_out):
    bsz, seq, d = x.shape
    proj = x @ w_in
    proj = (proj + mu * (token_shift(proj) - proj)).astype(jnp.float32)
    r = proj[..., :d]
    k = proj[..., d:2 * d]
    v = proj[..., 2 * d:3 * d]
    xw = proj[..., 3 * d:3 * d + R_LW]
    xa = proj[..., 3 * d + R_LW:3 * d + R_LW + R_LA]
    xg = proj[..., 3 * d + R_LW + R_LA:]
    log_w = -jax.nn.softplus(-(w0 + jnp.tanh(xw) @ w2)) - 0.5
    decay = jnp.exp(-jnp.exp(log_w))
    a = jax.nn.sigmoid(a0 + xa @ a2)
    g = jax.nn.sigmoid(xg) @ g2
    kk = (k * k_k).reshape(bsz, seq, R_HEADS, R_N)
    kk = kk / jnp.maximum(jnp.sqrt(jnp.sum(jnp.square(kk), -1, keepdims=True)), 1e-12)
    k = k * (1.0 + (a - 1.0) * k_a)
    r_h = r.reshape(bsz, seq, R_HEADS, R_N)
    k_h = k.reshape(bsz, seq, R_HEADS, R_N)
    v_h = v.reshape(bsz, seq, R_HEADS, R_N)
    a_h = a.reshape(bsz, seq, R_HEADS, R_N)
    w_h = decay.reshape(bsz, seq, R_HEADS, R_N)
    y = wkv7_scan(r_h, w_h, k_h, v_h, -kk, kk * a_h)
    y_mu = jnp.mean(y, -1, keepdims=True)
    y_var = jnp.mean(jnp.square(y - y_mu), -1, keepdims=True)
    y = ((y - y_mu) * lax.rsqrt(y_var + R_GN_EPS)).reshape(bsz, seq, d) * gn_g + gn_b
    bonus = jnp.sum(r_h * k_h * r_k, -1, keepdims=True) * v_h
    y = (y + bonus.reshape(bsz, seq, d)) * g
    return y.astype(x.dtype) @ w_out


def squared_relu_mlp(x, w1, w2):
    return jnp.square(jax.nn.relu(x @ w1)) @ w2


def setup_inputs(seed: int = 0) -> dict:
    key = jax.random.key(seed)
    ks = jax.random.split(key, 28)
    f32 = jnp.float32
    nm, nr, d = N_MLSTM_LAYERS, N_RWKV_LAYERS, D_MODEL

    def normal(k, shape, s):
        return s * jax.random.normal(k, shape, f32)

    return {
        'x': normal(ks[0], (BATCH, SEQ, d), 1.0),
        'mlstm_w_in': normal(ks[1], (nm, d, M_PROJ), d ** -0.5),
        'mlstm_b_i': normal(ks[2], (nm, M_HEADS), 0.1),
        'mlstm_b_f': jnp.linspace(3.0, 6.0, M_HEADS, dtype=f32) + normal(ks[3], (nm, M_HEADS), 0.1),
        'mlstm_conv_w': normal(ks[4], (nm, M_CONV, M_QK), M_CONV ** -0.5),
        'mlstm_conv_b': normal(ks[5], (nm, M_QK), 0.01),
        'mlstm_norm_g': 1.0 + normal(ks[6], (nm, M_HEADS * M_DV), 0.01),
        'mlstm_w_out': normal(ks[7], (nm, M_HEADS * M_DV, d), DN_BETA * (M_HEADS * M_DV) ** -0.5),
        'rwkv_w_in': normal(ks[8], (nr, d, R_PROJ), d ** -0.5),
        'rwkv_mu': jax.random.uniform(ks[9], (nr, R_PROJ), f32),
        'rwkv_w0': jnp.linspace(-6.5, -1.5, d, dtype=f32) + normal(ks[10], (nr, d), 0.1),
        'rwkv_w2': normal(ks[11], (nr, R_LW, d), 0.5 * R_LW ** -0.5),
        'rwkv_a0': normal(ks[12], (nr, d), 0.1),
        'rwkv_a2': normal(ks[13], (nr, R_LA, d), 0.5 * R_LA ** -0.5),
        'rwkv_g2': normal(ks[14], (nr, R_LG, d), R_LG ** -0.5),
        'rwkv_k_k': 0.85 + normal(ks[15], (nr, d), 0.02),
        'rwkv_k_a': 1.0 + normal(ks[16], (nr, d), 0.02),
        'rwkv_r_k': normal(ks[17], (nr, R_HEADS, R_N), 0.1),
        'rwkv_gn_g': 1.0 + normal(ks[18], (nr, d), 0.01),
        'rwkv_gn_b': normal(ks[19], (nr, d), 0.01),
        'rwkv_w_out': normal(ks[20], (nr, d, d), DN_BETA * d ** -0.5),
        'ln_mix_g': 1.0 + normal(ks[21], (DEPTH, d), 0.01),
        'ln_mix_b': normal(ks[22], (DEPTH, d), 0.01),
        'mlp_w1': normal(ks[23], (DEPTH, d, D_FF), d ** -0.5),
        'mlp_w2': normal(ks[24], (DEPTH, D_FF, d), DN_BETA * D_FF ** -0.5),
        'ln_ffn_g': 1.0 + normal(ks[25], (DEPTH, d), 0.01),
        'ln_ffn_b': normal(ks[26], (DEPTH, d), 0.01),
    }


def reference(x, mlstm_w_in, mlstm_b_i, mlstm_b_f, mlstm_conv_w, mlstm_conv_b, mlstm_norm_g, mlstm_w_out,
              rwkv_w_in, rwkv_mu, rwkv_w0, rwkv_w2, rwkv_a0, rwkv_a2, rwkv_g2, rwkv_k_k, rwkv_k_a, rwkv_r_k,
              rwkv_gn_g, rwkv_gn_b, rwkv_w_out, ln_mix_g, ln_mix_b, mlp_w1, mlp_w2, ln_ffn_g, ln_ffn_b):
    for layer in range(DEPTH):
        j = layer // N_MIXERS
        if layer % N_MIXERS == 0:
            mix = mlstm_mixer(x, mlstm_w_in[j], mlstm_b_i[j], mlstm_b_f[j], mlstm_conv_w[j], mlstm_conv_b[j],
                              mlstm_norm_g[j], mlstm_w_out[j])
        else:
            mix = rwkv7_mixer(x, rwkv_w_in[j], rwkv_mu[j], rwkv_w0[j], rwkv_w2[j], rwkv_a0[j], rwkv_a2[j],
                              rwkv_g2[j], rwkv_k_k[j], rwkv_k_a[j], rwkv_r_k[j], rwkv_gn_g[j], rwkv_gn_b[j],
                              rwkv_w_out[j])
        x = layer_norm(DN_ALPHA * x + mix, ln_mix_g[layer], ln_mix_b[layer])
        x = layer_norm(DN_ALPHA * x + squared_relu_mlp(x, mlp_w1[layer], mlp_w2[layer]),
                       ln_ffn_g[layer], ln_ffn_b[layer])
    return x
```

```python
import functools

import jax
import jax.numpy as jnp
from jax import lax
from jax.experimental import pallas as pl
from jax.experimental.pallas import tpu as pltpu

F32 = jnp.float32
BF16 = jnp.bfloat16
HIGHEST = lax.Precision.HIGHEST

DEPTH = 2
DN_ALPHA = (2.0 * DEPTH) ** 0.25
LN_EPS = 1e-5

M_HEADS = 4
M_DK = 128
M_DV = 256
M_GATE_CAP = 15.0
M_CHUNK = 256

R_N = 64
R_GN_EPS = 64e-5
R_CHUNK = 64
R_GROUP = 2

VMEM_LIMIT = 48 * 1024 * 1024


def _cparams(sem):
    return pltpu.CompilerParams(dimension_semantics=sem, vmem_limit_bytes=VMEM_LIMIT)


def _dot(a, b, precision=None):
    return jnp.dot(a, b, preferred_element_type=F32, precision=precision)


def _dot_nt(a, b, precision=None):
    return lax.dot_general(a, b, (((1,), (1,)), ((), ())), preferred_element_type=F32, precision=precision)


def _dot_tn(a, b, precision=None):
    return lax.dot_general(a, b, (((0,), (0,)), ((), ())), preferred_element_type=F32, precision=precision)


def _mm_body(x_ref, w_ref, o_ref, *, act):
    acc = _dot(x_ref[...].astype(BF16), w_ref[...])
    if act == "relu2":
        acc = jnp.square(jnp.maximum(acc, 0.0))
    o_ref[...] = acc.astype(o_ref.dtype)


def _matmul(x, w, *, tm, tn, out_dtype, act=None):
    m, k = x.shape
    n = w.shape[1]
    tm = min(tm, m)
    assert m % tm == 0 and n % tn == 0
    return pl.pallas_call(
        functools.partial(_mm_body, act=act),
        grid=(m // tm, n // tn),
        in_specs=[pl.BlockSpec((tm, k), lambda i, j: (i, 0)),
                  pl.BlockSpec((k, tn), lambda i, j: (0, j))],
        out_specs=pl.BlockSpec((tm, tn), lambda i, j: (i, j)),
        out_shape=jax.ShapeDtypeStruct((m, n), out_dtype),
        compiler_params=_cparams(("parallel", "parallel")),
    )(x, w)


def _mm_ln_body(x_ref, w_ref, r_ref, g_ref, b_ref, o_ref):
    acc = _dot(x_ref[...].astype(BF16), w_ref[...])
    y = DN_ALPHA * r_ref[...] + acc
    mu = jnp.mean(y, axis=-1, keepdims=True)
    yc = y - mu
    var = jnp.mean(jnp.square(yc), axis=-1, keepdims=True)
    o_ref[...] = yc * lax.rsqrt(var + LN_EPS) * g_ref[...] + b_ref[...]


def _matmul_res_ln(x, w, res, g, b, *, tm):
    m, k = x.shape
    n = w.shape[1]
    tm = min(tm, m)
    assert m % tm == 0
    return pl.pallas_call(
        _mm_ln_body,
        grid=(m // tm,),
        in_specs=[pl.BlockSpec((tm, k), lambda i: (i, 0)),
                  pl.BlockSpec((k, n), lambda i: (0, 0)),
                  pl.BlockSpec((tm, n), lambda i: (i, 0)),
                  pl.BlockSpec((1, n), lambda i: (0, 0)),
                  pl.BlockSpec((1, n), lambda i: (0, 0))],
        out_specs=pl.BlockSpec((tm, n), lambda i: (i, 0)),
        out_shape=jax.ShapeDtypeStruct((m, n), F32),
        compiler_params=_cparams(("parallel",)),
    )(x, w, res, g.reshape(1, n), b.reshape(1, n))


def _mlstm_body(q_ref, k_ref, v_ref, o_ref, gate_ref, ng_ref, h_ref, c_ref, n_ref, m_ref):
    lc = q_ref.shape[1]

    @pl.when(pl.program_id(2) == 0)
    def _():
        c_ref[...] = jnp.zeros_like(c_ref)
        n_ref[...] = jnp.zeros_like(n_ref)
        m_ref[...] = jnp.zeros_like(m_ref)

    q = q_ref[0] * (M_DK ** -0.5)
    k = k_ref[0]
    v = v_ref[0]
    gates = gate_ref[0, 0]
    f_row = gates[0:1, :]
    i_row = gates[1:2, :]
    ri = lax.broadcasted_iota(jnp.int32, (lc, lc), 0)
    ci = lax.broadcasted_iota(jnp.int32, (lc, lc), 1)
    causal = ci <= ri
    tri_u = jnp.where(ri <= ci, 1.0, 0.0).astype(F32)
    bcum_row = _dot(jnp.broadcast_to(f_row, (8, lc)), tri_u, HIGHEST)[0:1, :]
    bcum_col = jnp.sum(jnp.where(causal, f_row, 0.0), axis=-1, keepdims=True)
    i_col = jnp.sum(jnp.where(ci == ri, i_row, 0.0), axis=-1, keepdims=True)
    m_prev = m_ref[0:1, 0:1]
    c_prev = c_ref[...]
    n_prev = n_ref[0:1, :]

    log_d = jnp.where(causal, bcum_col - bcum_row + i_row, -jnp.inf)
    log_inter = bcum_col + m_prev
    m_row = jnp.maximum(jnp.max(log_d, axis=-1, keepdims=True), log_inter)
    d_mat = jnp.exp(log_d - m_row)
    inter = jnp.exp(log_inter - m_row)
    qb = q.astype(BF16)
    kb = k.astype(BF16)
    s = _dot_nt(qb, kb) * d_mat
    num = _dot(s.astype(BF16), v.astype(BF16)) + inter * _dot(qb, c_prev.astype(BF16))
    den = jnp.sum(s, axis=-1, keepdims=True) + inter * jnp.sum(q * n_prev, axis=-1, keepdims=True)
    hc = num / jnp.maximum(jnp.abs(den), jnp.exp(-m_row))

    b_last = bcum_col[lc - 1:lc, :]
    log_w = b_last - bcum_col + i_col
    m_new = jnp.maximum(b_last + m_prev, jnp.max(log_w, axis=0, keepdims=True))
    decay = jnp.exp(b_last + m_prev - m_new)
    kw = k * jnp.exp(log_w - m_new)
    c_ref[...] = decay * c_prev + _dot_tn(kw.astype(BF16), v.astype(BF16))
    n_ref[...] = jnp.broadcast_to(decay * n_prev + jnp.sum(kw, axis=0, keepdims=True), n_ref.shape)
    m_ref[...] = jnp.broadcast_to(m_new, m_ref.shape)

    hn = hc * lax.rsqrt(jnp.mean(jnp.square(hc), axis=-1, keepdims=True) + 1e-6)
    h_ref[0] = (hn * ng_ref[...] * jax.nn.sigmoid(o_ref[0])).astype(h_ref.dtype)


def _mlstm_core(qk, proj, gates, norm_g):
    bsz, seq, _ = qk.shape
    lc = gates.shape[-1]
    nc = seq // lc
    hdv = M_HEADS * M_DV
    return pl.pallas_call(
        _mlstm_body,
        grid=(bsz, M_HEADS, nc),
        in_specs=[pl.BlockSpec((1, lc, M_DK), lambda b, h, c: (b, c, h)),
                  pl.BlockSpec((1, lc, M_DK), lambda b, h, c: (b, c, M_HEADS + h)),
                  pl.BlockSpec((1, lc, M_DV), lambda b, h, c: (b, c, M_HEADS + h)),
                  pl.BlockSpec((1, lc, M_DV), lambda b, h, c: (b, c, 2 * M_HEADS + h)),
                  pl.BlockSpec((1, 1, 2, lc), lambda b, h, c: (b * M_HEADS + h, c, 0, 0)),
                  pl.BlockSpec((1, M_DV), lambda b, h, c: (0, h))],
        out_specs=pl.BlockSpec((1, lc, M_DV), lambda b, h, c: (b, c, h)),
        out_shape=jax.ShapeDtypeStruct((bsz, seq, hdv), BF16),
        scratch_shapes=[pltpu.VMEM((M_DK, M_DV), F32),
                        pltpu.VMEM((8, M_DK), F32),
                        pltpu.VMEM((8, 128), F32)],
        compiler_params=_cparams(("parallel", "parallel", "arbitrary")),
    )(qk, qk, proj, proj, gates, norm_g.reshape(1, hdv))


def _softplus(z):
    return jnp.maximum(z, 0.0) + jnp.log1p(jnp.exp(-jnp.abs(z)))


def _mlstm_mixer(x, w_in, b_i, b_f, conv_w, conv_b, norm_g):
    bsz, seq, d = x.shape
    m = bsz * seq
    hdv = M_HEADS * M_DV
    n_main = 2 * M_HEADS * M_DK + 2 * hdv
    xf = x.reshape(m, d)
    proj = _matmul(xf, w_in[:, :n_main].astype(BF16), tm=512, tn=1024, out_dtype=F32)
    w_gate = jnp.pad(w_in[:, n_main:], ((0, 0), (0, 128 - 2 * M_HEADS))).astype(BF16)
    gate_pre = _matmul(xf, w_gate, tm=512, tn=128, out_dtype=F32)[:, :2 * M_HEADS]
    proj = proj.reshape(bsz, seq, n_main)
    qk_pre = proj[..., :2 * M_HEADS * M_DK]
    kc = conv_w.shape[0]
    padded = jnp.pad(qk_pre, ((0, 0), (kc - 1, 0), (0, 0)))
    conv = conv_b
    for j in range(kc):
        conv = conv + conv_w[j] * padded[:, j:j + seq, :]
    qk = jax.nn.silu(conv)
    gate_pre = gate_pre.reshape(bsz, seq, 2 * M_HEADS)
    log_i = M_GATE_CAP * jnp.tanh((gate_pre[..., :M_HEADS] + b_i) / M_GATE_CAP)
    log_f = -_softplus(-(M_GATE_CAP * jnp.tanh((gate_pre[..., M_HEADS:] + b_f) / M_GATE_CAP)))
    lc = min(M_CHUNK, seq)
    gates = jnp.stack([log_f, log_i], axis=0)
    gates = gates.reshape(2, bsz, seq // lc, lc, M_HEADS).transpose(1, 4, 2, 0, 3)
    gates = gates.reshape(bsz * M_HEADS, seq // lc, 2, lc)
    return _mlstm_core(qk, proj, gates, norm_g)


def _wkv_body(r_ref, k_ref, v_ref, dw_ref, ap_ref, g_ref, w0_ref, a0_ref, kk_ref, ka_ref, rk_ref,
              gng_ref, gnb_ref, o_ref, s_ref):
    lc = r_ref.shape[1]
    gw = r_ref.shape[2]
    nh = gw // R_N
    rows = nh * lc

    @pl.when(pl.program_id(2) == 0)
    def _():
        s_ref[...] = jnp.zeros_like(s_ref)

    lane = lax.broadcasted_iota(jnp.int32, (1, gw), 1)
    head_masks = [jnp.logical_and(lane >= R_N * h, lane < R_N * (h + 1)) for h in range(nh)]

    def seg_sum(x):
        out = jnp.zeros_like(x)
        for msk in head_masks:
            out = jnp.where(msk, jnp.sum(jnp.where(msk, x, 0.0), axis=-1, keepdims=True), out)
        return out

    def stack(x):
        return jnp.concatenate([jnp.where(msk, x, 0.0) for msk in head_masks], axis=0)

    r = r_ref[0]
    k = k_ref[0]
    v = v_ref[0]
    log_w = -_softplus(-(w0_ref[...] + dw_ref[0])) - 0.5
    ld = -jnp.exp(log_w)
    a_lr = jax.nn.sigmoid(a0_ref[...] + ap_ref[0])
    kk = k * kk_ref[...]
    kk = kk / jnp.maximum(jnp.sqrt(seg_sum(jnp.square(kk))), 1e-12)
    k = k * (1.0 + (a_lr - 1.0) * ka_ref[...])
    a_ = -kk
    b_ = kk * a_lr

    ti = lax.broadcasted_iota(jnp.int32, (lc, lc), 0)
    tj = lax.broadcasted_iota(jnp.int32, (lc, lc), 1)
    cum = _dot(jnp.where(tj <= ti, 1.0, 0.0).astype(F32), ld, HIGHEST)
    c_last = cum[lc - 1:lc, :]
    e_pos = jnp.exp(cum)
    e_neg = jnp.exp(-cum)
    e_end = jnp.exp(c_last - cum)
    a_t = stack(a_ * jnp.exp(cum - ld))
    r_t = stack(r * e_pos)
    b_t = stack(b_ * e_neg)
    k_t = stack(k * e_neg)
    b_h = stack(b_ * e_end)
    k_h = stack(k * e_end)
    v_s = stack(v)

    p = _dot_nt(jnp.concatenate([a_t, r_t], axis=0).astype(BF16),
                jnp.concatenate([b_t, k_t], axis=0).astype(BF16))
    ri = lax.broadcasted_iota(jnp.int32, (rows, rows), 0)
    ci = lax.broadcasted_iota(jnp.int32, (rows, rows), 1)
    n_ab = jnp.where(ci < ri, p[:rows, :rows], 0.0)
    m_ak = jnp.where(ci < ri, p[:rows, rows:], 0.0)
    m_rb = jnp.where(ci <= ri, p[rows:, :rows], 0.0)
    m_rk = jnp.where(ci <= ri, p[rows:, rows:], 0.0)

    t_inv = jnp.where(ci == ri, 1.0, 0.0).astype(F32) + n_ab
    pw = n_ab
    steps = 1
    while steps * 2 < lc:
        pw = _dot(pw, pw, HIGHEST)
        t_inv = t_inv + _dot(t_inv, pw, HIGHEST)
        steps *= 2

    s0 = s_ref[...]
    s0b = s0.astype(BF16)
    v_sb = v_s.astype(BF16)
    z = _dot_nt(a_t.astype(BF16), s0b) + _dot(m_ak.astype(BF16), v_sb)
    u = _dot(t_inv, z, HIGHEST)
    ub = u.astype(BF16)
    y_bd = _dot_nt(r_t.astype(BF16), s0b) + _dot(m_rb.astype(BF16), ub) + _dot(m_rk.astype(BF16), v_sb)
    s_ref[...] = s0 * jnp.exp(c_last) + _dot_tn(ub, b_h.astype(BF16)) + _dot_tn(v_sb, k_h.astype(BF16))

    y = y_bd[0:lc]
    for h in range(1, nh):
        y = y + y_bd[h * lc:(h + 1) * lc]

    mu = seg_sum(y) * (1.0 / R_N)
    yc = y - mu
    var = seg_sum(jnp.square(yc)) * (1.0 / R_N)
    yn = yc * lax.rsqrt(var + R_GN_EPS) * gng_ref[...] + gnb_ref[...]
    bonus = seg_sum(r * k * rk_ref[...]) * v
    o_ref[0] = ((yn + bonus) * g_ref[0]).astype(o_ref.dtype)


def _wkv_core(proj, lowrank, w0, a0, k_k, k_a, r_k, gn_g, gn_b):
    bsz, seq, _ = proj.shape
    d = w0.shape[-1]
    gw = R_GROUP * R_N
    ng = d // gw
    lc = min(R_CHUNK, seq)
    nc = seq // lc
    tile = lambda off: pl.BlockSpec((1, lc, gw), lambda b, g, c: (b, c, off * ng + g))
    par = pl.BlockSpec((1, gw), lambda b, g, c: (0, g))
    row = lambda p: p.reshape(1, d)
    return pl.pallas_call(
        _wkv_body,
        grid=(bsz, ng, nc),
        in_specs=[tile(0), tile(1), tile(2), tile(0), tile(1), tile(2)] + [par] * 7,
        out_specs=pl.BlockSpec((1, lc, gw), lambda b, g, c: (b, c, g)),
        out_shape=jax.ShapeDtypeStruct((bsz, seq, d), BF16),
        scratch_shapes=[pltpu.VMEM((gw, gw), F32)],
        compiler_params=_cparams(("parallel", "parallel", "arbitrary")),
    )(proj, proj, proj, lowrank, lowrank, lowrank,
      row(w0), row(a0), row(k_k), row(k_a), row(r_k), row(gn_g), row(gn_b))


def _rwkv_mixer(x, w_in, mu, w0, w2, a0, a2, g2, k_k, k_a, r_k, gn_g, gn_b):
    bsz, seq, d = x.shape
    m = bsz * seq
    n_proj = w_in.shape[1]
    proj = _matmul(x.reshape(m, d), w_in.astype(BF16), tm=512, tn=n_proj // 13, out_dtype=F32)
    proj = proj.reshape(bsz, seq, n_proj)
    shifted = jnp.pad(proj, ((0, 0), (1, 0), (0, 0)))[:, :-1]
    proj = proj + mu * (shifted - proj)
    lw, la, lg = w2.shape[0], a2.shape[0], g2.shape[0]
    xw = proj[..., 3 * d:3 * d + lw]
    xa = proj[..., 3 * d + lw:3 * d + lw + la]
    xg = proj[..., 3 * d + lw + la:]
    feats = jnp.concatenate([jnp.tanh(xw), xa, jax.nn.sigmoid(xg)], axis=-1).reshape(m, lw + la + lg)
    w_lr = jnp.zeros((lw + la + lg, 3 * d), F32)
    w_lr = w_lr.at[:lw, :d].set(w2).at[lw:lw + la, d:2 * d].set(a2).at[lw + la:, 2 * d:].set(g2)
    lowrank = _matmul(feats, w_lr.astype(BF16), tm=512, tn=1024, out_dtype=F32).reshape(bsz, seq, 3 * d)
    return _wkv_core(proj, lowrank, w0, a0, k_k, k_a, r_k, gn_g, gn_b)


def _ffn(xf, w1, w2, g, b):
    hidden = _matmul(xf, w1.astype(BF16), tm=512, tn=1024, out_dtype=BF16, act="relu2")
    return _matmul_res_ln(hidden, w2.astype(BF16), xf, g, b, tm=256)


def kernel(x, mlstm_w_in, mlstm_b_i, mlstm_b_f, mlstm_conv_w, mlstm_conv_b, mlstm_norm_g, mlstm_w_out, rwkv_w_in, rwkv_mu, rwkv_w0, rwkv_w2, rwkv_a0, rwkv_a2, rwkv_g2, rwkv_k_k, rwkv_k_a, rwkv_r_k, rwkv_gn_g, rwkv_gn_b, rwkv_w_out, ln_mix_g, ln_mix_b, mlp_w1, mlp_w2, ln_ffn_g, ln_ffn_b):
    bsz, seq, d = x.shape
    m = bsz * seq
    xf = x.reshape(m, d)
    for layer in range(DEPTH):
        j = layer // 2
        if layer % 2 == 0:
            mix = _mlstm_mixer(xf.reshape(bsz, seq, d), mlstm_w_in[j], mlstm_b_i[j], mlstm_b_f[j],
                               mlstm_conv_w[j], mlstm_conv_b[j], mlstm_norm_g[j])
            w_out = mlstm_w_out[j]
        else:
            mix = _rwkv_mixer(xf.reshape(bsz, seq, d), rwkv_w_in[j], rwkv_mu[j], rwkv_w0[j], rwkv_w2[j],
                              rwkv_a0[j], rwkv_a2[j], rwkv_g2[j], rwkv_k_k[j], rwkv_k_a[j],
                              rwkv_r_k[j].reshape(-1), rwkv_gn_g[j], rwkv_gn_b[j])
            w_out = rwkv_w_out[j]
        xf = _matmul_res_ln(mix.reshape(m, -1), w_out.astype(BF16), xf, ln_mix_g[layer], ln_mix_b[layer], tm=512)
        xf = _ffn(xf, mlp_w1[layer], mlp_w2[layer], ln_ffn_g[layer], ln_ffn_b[layer])
    return xf.reshape(bsz, seq, d)
```

```python
import functools

import jax
import jax.numpy as jnp
from jax import lax
from jax.experimental import pallas as pl
from jax.experimental.pallas import tpu as pltpu

F32 = jnp.float32
BF16 = jnp.bfloat16
HIGHEST = lax.Precision.HIGHEST

DEPTH = 2
DN_ALPHA = (2.0 * DEPTH) ** 0.25
LN_EPS = 1e-5

M_HEADS = 4
M_DK = 128
M_DV = 256
M_GATE_CAP = 15.0
M_CHUNK = 256

R_N = 64
R_GN_EPS = 64e-5
R_CHUNK = 64
R_GROUP = 2
R_PAIRS = 4

VMEM_LIMIT = 48 * 1024 * 1024


def _cparams(sem):
    return pltpu.CompilerParams(dimension_semantics=sem, vmem_limit_bytes=VMEM_LIMIT)


def _dot(a, b, precision=None):
    return jnp.dot(a, b, preferred_element_type=F32, precision=precision)


def _dot_nt(a, b, precision=None):
    return lax.dot_general(a, b, (((1,), (1,)), ((), ())), preferred_element_type=F32, precision=precision)


def _dot_tn(a, b, precision=None):
    return lax.dot_general(a, b, (((0,), (0,)), ((), ())), preferred_element_type=F32, precision=precision)


def _mm_body(x_ref, w_ref, o_ref, *, act):
    acc = _dot(x_ref[...].astype(BF16), w_ref[...])
    if act == "relu2":
        acc = jnp.square(jnp.maximum(acc, 0.0))
    o_ref[...] = acc.astype(o_ref.dtype)


def _matmul(x, w, *, tm, tn, out_dtype, act=None):
    m, k = x.shape
    n = w.shape[1]
    tm = min(tm, m)
    assert m % tm == 0 and n % tn == 0
    return pl.pallas_call(
        functools.partial(_mm_body, act=act),
        grid=(m // tm, n // tn),
        in_specs=[pl.BlockSpec((tm, k), lambda i, j: (i, 0)),
                  pl.BlockSpec((k, tn), lambda i, j: (0, j))],
        out_specs=pl.BlockSpec((tm, tn), lambda i, j: (i, j)),
        out_shape=jax.ShapeDtypeStruct((m, n), out_dtype),
        compiler_params=_cparams(("parallel", "parallel")),
    )(x, w)


def _mm_ln_body(x_ref, w_ref, r_ref, g_ref, b_ref, o_ref):
    acc = _dot(x_ref[...].astype(BF16), w_ref[...])
    y = DN_ALPHA * r_ref[...] + acc
    mu = jnp.mean(y, axis=-1, keepdims=True)
    yc = y - mu
    var = jnp.mean(jnp.square(yc), axis=-1, keepdims=True)
    o_ref[...] = yc * lax.rsqrt(var + LN_EPS) * g_ref[...] + b_ref[...]


def _matmul_res_ln(x, w, res, g, b, *, tm):
    m, k = x.shape
    n = w.shape[1]
    tm = min(tm, m)
    assert m % tm == 0
    return pl.pallas_call(
        _mm_ln_body,
        grid=(m // tm,),
        in_specs=[pl.BlockSpec((tm, k), lambda i: (i, 0)),
                  pl.BlockSpec((k, n), lambda i: (0, 0)),
                  pl.BlockSpec((tm, n), lambda i: (i, 0)),
                  pl.BlockSpec((1, n), lambda i: (0, 0)),
                  pl.BlockSpec((1, n), lambda i: (0, 0))],
        out_specs=pl.BlockSpec((tm, n), lambda i: (i, 0)),
        out_shape=jax.ShapeDtypeStruct((m, n), F32),
        compiler_params=_cparams(("parallel",)),
    )(x, w, res, g.reshape(1, n), b.reshape(1, n))


def _mlstm_body(q_ref, k_ref, v_ref, o_ref, gate_ref, ng_ref, h_ref, c_ref, n_ref, m_ref):
    lc = q_ref.shape[1]

    @pl.when(pl.program_id(2) == 0)
    def _():
        c_ref[...] = jnp.zeros_like(c_ref)
        n_ref[...] = jnp.zeros_like(n_ref)
        m_ref[...] = jnp.zeros_like(m_ref)

    q = q_ref[0] * (M_DK ** -0.5)
    k = k_ref[0]
    v = v_ref[0]
    gates = gate_ref[0, 0]
    f_row = gates[0:1, :]
    i_row = gates[1:2, :]
    ri = lax.broadcasted_iota(jnp.int32, (lc, lc), 0)
    ci = lax.broadcasted_iota(jnp.int32, (lc, lc), 1)
    causal = ci <= ri
    tri_u = jnp.where(ri <= ci, 1.0, 0.0).astype(F32)
    bcum_row = _dot(jnp.broadcast_to(f_row, (8, lc)), tri_u, HIGHEST)[0:1, :]
    bcum_col = jnp.sum(jnp.where(causal, f_row, 0.0), axis=-1, keepdims=True)
    i_col = jnp.sum(jnp.where(ci == ri, i_row, 0.0), axis=-1, keepdims=True)
    m_prev = m_ref[0:1, 0:1]
    c_prev = c_ref[...]
    n_prev = n_ref[0:1, :]

    log_d = jnp.where(causal, bcum_col - bcum_row + i_row, -jnp.inf)
    log_inter = bcum_col + m_prev
    m_row = jnp.maximum(jnp.max(log_d, axis=-1, keepdims=True), log_inter)
    d_mat = jnp.exp(log_d - m_row)
    inter = jnp.exp(log_inter - m_row)
    qb = q.astype(BF16)
    kb = k.astype(BF16)
    s = _dot_nt(qb, kb) * d_mat
    num = _dot(s.astype(BF16), v.astype(BF16)) + inter * _dot(qb, c_prev.astype(BF16))
    den = jnp.sum(s, axis=-1, keepdims=True) + inter * jnp.sum(q * n_prev, axis=-1, keepdims=True)
    hc = num / jnp.maximum(jnp.abs(den), jnp.exp(-m_row))

    b_last = bcum_col[lc - 1:lc, :]
    log_w = b_last - bcum_col + i_col
    m_new = jnp.maximum(b_last + m_prev, jnp.max(log_w, axis=0, keepdims=True))
    decay = jnp.exp(b_last + m_prev - m_new)
    kw = k * jnp.exp(log_w - m_new)
    c_ref[...] = decay * c_prev + _dot_tn(kw.astype(BF16), v.astype(BF16))
    n_ref[...] = jnp.broadcast_to(decay * n_prev + jnp.sum(kw, axis=0, keepdims=True), n_ref.shape)
    m_ref[...] = jnp.broadcast_to(m_new, m_ref.shape)

    hn = hc * lax.rsqrt(jnp.mean(jnp.square(hc), axis=-1, keepdims=True) + 1e-6)
    h_ref[0] = (hn * ng_ref[...] * jax.nn.sigmoid(o_ref[0])).astype(h_ref.dtype)


def _mlstm_core(qk, proj, gates, norm_g):
    bsz, seq, _ = qk.shape
    lc = gates.shape[-1]
    nc = seq // lc
    hdv = M_HEADS * M_DV
    return pl.pallas_call(
        _mlstm_body,
        grid=(bsz, M_HEADS, nc),
        in_specs=[pl.BlockSpec((1, lc, M_DK), lambda b, h, c: (b, c, h)),
                  pl.BlockSpec((1, lc, M_DK), lambda b, h, c: (b, c, M_HEADS + h)),
                  pl.BlockSpec((1, lc, M_DV), lambda b, h, c: (b, c, M_HEADS + h)),
                  pl.BlockSpec((1, lc, M_DV), lambda b, h, c: (b, c, 2 * M_HEADS + h)),
                  pl.BlockSpec((1, 1, 2, lc), lambda b, h, c: (b * M_HEADS + h, c, 0, 0)),
                  pl.BlockSpec((1, M_DV), lambda b, h, c: (0, h))],
        out_specs=pl.BlockSpec((1, lc, M_DV), lambda b, h, c: (b, c, h)),
        out_shape=jax.ShapeDtypeStruct((bsz, seq, hdv), BF16),
        scratch_shapes=[pltpu.VMEM((M_DK, M_DV), F32),
                        pltpu.VMEM((8, M_DK), F32),
                        pltpu.VMEM((8, 128), F32)],
        compiler_params=_cparams(("parallel", "parallel", "arbitrary")),
        name="mlstm_chunkwise",
    )(qk, qk, proj, proj, gates, norm_g.reshape(1, hdv))


def _softplus(z):
    return jnp.maximum(z, 0.0) + jnp.log1p(jnp.exp(-jnp.abs(z)))


def _mlstm_mixer(x, w_in, b_i, b_f, conv_w, conv_b, norm_g):
    bsz, seq, d = x.shape
    m = bsz * seq
    hdv = M_HEADS * M_DV
    n_main = 2 * M_HEADS * M_DK + 2 * hdv
    xf = x.reshape(m, d)
    proj = _matmul(xf, w_in[:, :n_main].astype(BF16), tm=512, tn=1024, out_dtype=F32)
    w_gate = jnp.pad(w_in[:, n_main:], ((0, 0), (0, 128 - 2 * M_HEADS))).astype(BF16)
    gate_pre = _matmul(xf, w_gate, tm=512, tn=128, out_dtype=F32)[:, :2 * M_HEADS]
    proj = proj.reshape(bsz, seq, n_main)
    qk_pre = proj[..., :2 * M_HEADS * M_DK]
    kc = conv_w.shape[0]
    padded = jnp.pad(qk_pre, ((0, 0), (kc - 1, 0), (0, 0)))
    conv = conv_b
    for j in range(kc):
        conv = conv + conv_w[j] * padded[:, j:j + seq, :]
    qk = jax.nn.silu(conv)
    gate_pre = gate_pre.reshape(bsz, seq, 2 * M_HEADS)
    log_i = M_GATE_CAP * jnp.tanh((gate_pre[..., :M_HEADS] + b_i) / M_GATE_CAP)
    log_f = -_softplus(-(M_GATE_CAP * jnp.tanh((gate_pre[..., M_HEADS:] + b_f) / M_GATE_CAP)))
    lc = min(M_CHUNK, seq)
    gates = jnp.stack([log_f, log_i], axis=0)
    gates = gates.reshape(2, bsz, seq // lc, lc, M_HEADS).transpose(1, 4, 2, 0, 3)
    gates = gates.reshape(bsz * M_HEADS, seq // lc, 2, lc)
    return _mlstm_core(qk, proj, gates, norm_g)


def _split_bf16(x):
    hi = x.astype(BF16)
    return hi, (x - hi.astype(F32)).astype(BF16)


def _dot3(a, b):
    ah, al = _split_bf16(a)
    bh, bl = _split_bf16(b)
    return _dot(jnp.concatenate([ah, ah, al], axis=1), jnp.concatenate([bh, bl, bh], axis=0))


def _wkv_body(r_ref, k_ref, v_ref, dw_ref, ap_ref, g_ref, w0_ref, a0_ref, kk_ref, ka_ref, rk_ref,
              gng_ref, gnb_ref, o_ref, s_ref):
    lc = r_ref.shape[1]
    width = r_ref.shape[2]
    gw = R_GROUP * R_N
    rows = R_GROUP * lc

    @pl.when(pl.program_id(2) == 0)
    def _():
        s_ref[...] = jnp.zeros_like(s_ref)

    lane = lax.broadcasted_iota(jnp.int32, (1, width), 1)
    head_of_lane = lane // R_N

    def seg_sum(x):
        out = jnp.zeros_like(x)
        for h in range(width // R_N):
            msk = head_of_lane == h
            out = jnp.where(msk, jnp.sum(jnp.where(msk, x, 0.0), axis=-1, keepdims=True), out)
        return out

    r = r_ref[0]
    k = k_ref[0]
    v = v_ref[0]
    log_w = -_softplus(-(w0_ref[...] + dw_ref[0])) - 0.5
    ld = -jnp.exp(log_w)
    a_lr = jax.nn.sigmoid(a0_ref[...] + ap_ref[0])
    kk = k * kk_ref[...]
    kk = kk / jnp.maximum(jnp.sqrt(seg_sum(jnp.square(kk))), 1e-12)
    k = k * (1.0 + (a_lr - 1.0) * ka_ref[...])
    b_ = kk * a_lr

    ti = lax.broadcasted_iota(jnp.int32, (lc, lc), 0)
    tj = lax.broadcasted_iota(jnp.int32, (lc, lc), 1)
    cum = _dot(jnp.where(tj <= ti, 1.0, 0.0).astype(F32), ld, HIGHEST)
    c_last = cum[lc - 1:lc, :]
    e_pos = jnp.exp(cum)
    e_neg = jnp.exp(-cum)
    e_end = jnp.exp(c_last - cum)
    w_end = jnp.exp(c_last)
    a_til = -kk * jnp.exp(cum - ld)
    r_til = r * e_pos
    b_til = b_ * e_neg
    k_til = k * e_neg
    b_hat = b_ * e_end
    k_hat = k * e_end

    pair_lane = lax.broadcasted_iota(jnp.int32, (1, gw), 1)
    pair_masks = [pair_lane // R_N == h for h in range(R_GROUP)]

    def stack(x):
        return jnp.concatenate([jnp.where(msk, x, 0.0) for msk in pair_masks], axis=0)

    ri = lax.broadcasted_iota(jnp.int32, (rows, rows), 0)
    ci = lax.broadcasted_iota(jnp.int32, (rows, rows), 1)
    strict = ci < ri
    ri2 = lax.broadcasted_iota(jnp.int32, (2 * rows, rows), 0)
    ci2 = lax.broadcasted_iota(jnp.int32, (2 * rows, rows), 1)
    causal2 = ci2 < jnp.where(ri2 < rows, ri2, ri2 - rows + 1)
    eye = jnp.where(ci == ri, 1.0, 0.0).astype(F32)
    level_masks = []
    s = 1
    while s < lc:
        lo, hi = s.bit_length() - 1, s.bit_length()
        level_masks.append(jnp.logical_and((ri >> hi) == (ci >> hi), (ri >> lo) != (ci >> lo)))
        s *= 2

    pairs = range(width // gw)
    sls = [slice(p * gw, (p + 1) * gw) for p in pairs]
    ar_b = [jnp.concatenate([stack(a_til[:, sl]), stack(r_til[:, sl])], axis=0).astype(BF16) for sl in sls]
    bk_b = [jnp.concatenate([stack(b_til[:, sl]), stack(k_til[:, sl])], axis=0).astype(BF16) for sl in sls]
    v_sb = [stack(v[:, sl]).astype(BF16) for sl in sls]
    bk_hat = [jnp.concatenate([stack(b_hat[:, sl]), stack(k_hat[:, sl])], axis=0).astype(BF16) for sl in sls]

    pm = [_dot_nt(ar_b[p], bk_b[p]) for p in pairs]
    n_ab = [jnp.where(strict, pm[p][:rows, :rows], 0.0) for p in pairs]
    m_xk = [jnp.where(causal2, pm[p][:, rows:], 0.0).astype(BF16) for p in pairs]
    m_rb = [jnp.where(causal2[rows:], pm[p][rows:, :rows], 0.0).astype(BF16) for p in pairs]

    t_inv = [eye + jnp.where(level_masks[0], n_ab[p], 0.0) for p in pairs]
    for msk in level_masks[1:]:
        half = [_dot3(t_inv[p], jnp.where(msk, n_ab[p], 0.0)) for p in pairs]
        t_inv = [t_inv[p] + _dot3(half[p], t_inv[p]) for p in pairs]

    s0 = [s_ref[p] for p in pairs]
    zy = [_dot_nt(ar_b[p], s0[p].astype(BF16)) + _dot(m_xk[p], v_sb[p]) for p in pairs]
    ub = []
    for p in pairs:
        t_hi, t_lo = _split_bf16(t_inv[p])
        zb = zy[p][:rows].astype(BF16)
        ub.append(_dot(jnp.concatenate([t_hi, t_lo], axis=1), jnp.concatenate([zb, zb], axis=0)).astype(BF16))
    y_bd = [zy[p][rows:] + _dot(m_rb[p], ub[p]) for p in pairs]
    for p in pairs:
        s_ref[p] = s0[p] * w_end[:, sls[p]] + _dot_tn(jnp.concatenate([ub[p], v_sb[p]], axis=0), bk_hat[p])
    y = jnp.concatenate([functools.reduce(jnp.add, [y_bd[p][h * lc:(h + 1) * lc] for h in range(R_GROUP)])
                         for p in pairs], axis=1)

    mu = seg_sum(y) * (1.0 / R_N)
    yc = y - mu
    var = seg_sum(jnp.square(yc)) * (1.0 / R_N)
    yn = yc * lax.rsqrt(var + R_GN_EPS) * gng_ref[...] + gnb_ref[...]
    bonus = seg_sum(r * k * rk_ref[...]) * v
    o_ref[0] = ((yn + bonus) * g_ref[0]).astype(o_ref.dtype)


def _wkv_core(proj, lowrank, w0, a0, k_k, k_a, r_k, gn_g, gn_b):
    bsz, seq, _ = proj.shape
    d = w0.shape[-1]
    gw = R_GROUP * R_N
    width = R_PAIRS * gw
    nb = d // width
    lc = min(R_CHUNK, seq)
    nc = seq // lc
    tile = lambda off: pl.BlockSpec((1, lc, width), lambda b, g, c: (b, c, off * nb + g))
    par = pl.BlockSpec((1, width), lambda b, g, c: (0, g))
    row = lambda p: p.reshape(1, d)
    return pl.pallas_call(
        _wkv_body,
        grid=(bsz, nb, nc),
        in_specs=[tile(0), tile(1), tile(2), tile(0), tile(1), tile(2)] + [par] * 7,
        out_specs=pl.BlockSpec((1, lc, width), lambda b, g, c: (b, c, g)),
        out_shape=jax.ShapeDtypeStruct((bsz, seq, d), BF16),
        scratch_shapes=[pltpu.VMEM((R_PAIRS, gw, gw), F32)],
        compiler_params=_cparams(("parallel", "parallel", "arbitrary")),
        name="wkv7_chunked",
    )(proj, proj, proj, lowrank, lowrank, lowrank,
      row(w0), row(a0), row(k_k), row(k_a), row(r_k), row(gn_g), row(gn_b))


def _rwkv_mixer(x, w_in, mu, w0, w2, a0, a2, g2, k_k, k_a, r_k, gn_g, gn_b):
    bsz, seq, d = x.shape
    m = bsz * seq
    n_proj = w_in.shape[1]
    proj = _matmul(x.reshape(m, d), w_in.astype(BF16), tm=512, tn=n_proj // 13, out_dtype=F32)
    proj = proj.reshape(bsz, seq, n_proj)
    shifted = jnp.pad(proj, ((0, 0), (1, 0), (0, 0)))[:, :-1]
    proj = proj + mu * (shifted - proj)
    lw, la, lg = w2.shape[0], a2.shape[0], g2.shape[0]
    xw = proj[..., 3 * d:3 * d + lw]
    xa = proj[..., 3 * d + lw:3 * d + lw + la]
    xg = proj[..., 3 * d + lw + la:]
    feats = jnp.concatenate([jnp.tanh(xw), xa, jax.nn.sigmoid(xg)], axis=-1).reshape(m, lw + la + lg)
    w_lr = jnp.zeros((lw + la + lg, 3 * d), F32)
    w_lr = w_lr.at[:lw, :d].set(w2).at[lw:lw + la, d:2 * d].set(a2).at[lw + la:, 2 * d:].set(g2)
    lowrank = _matmul(feats, w_lr.astype(BF16), tm=512, tn=1024, out_dtype=F32).reshape(bsz, seq, 3 * d)
    return _wkv_core(proj, lowrank, w0, a0, k_k, k_a, r_k, gn_g, gn_b)


def _ffn(xf, w1, w2, g, b):
    hidden = _matmul(xf, w1.astype(BF16), tm=512, tn=1024, out_dtype=BF16, act="relu2")
    return _matmul_res_ln(hidden, w2.astype(BF16), xf, g, b, tm=256)


def kernel(x, mlstm_w_in, mlstm_b_i, mlstm_b_f, mlstm_conv_w, mlstm_conv_b, mlstm_norm_g, mlstm_w_out, rwkv_w_in, rwkv_mu, rwkv_w0, rwkv_w2, rwkv_a0, rwkv_a2, rwkv_g2, rwkv_k_k, rwkv_k_a, rwkv_r_k, rwkv_gn_g, rwkv_gn_b, rwkv_w_out, ln_mix_g, ln_mix_b, mlp_w1, mlp_w2, ln_ffn_g, ln_ffn_b):
    bsz, seq, d = x.shape
    m = bsz * seq
    xf = x.reshape(m, d)
    for layer in range(DEPTH):
        j = layer // 2
        if layer % 2 == 0:
            mix = _mlstm_mixer(xf.reshape(bsz, seq, d), mlstm_w_in[j], mlstm_b_i[j], mlstm_b_f[j],
                               mlstm_conv_w[j], mlstm_conv_b[j], mlstm_norm_g[j])
            w_out = mlstm_w_out[j]
        else:
            mix = _rwkv_mixer(xf.reshape(bsz, seq, d), rwkv_w_in[j], rwkv_mu[j], rwkv_w0[j], rwkv_w2[j],
                              rwkv_a0[j], rwkv_a2[j], rwkv_g2[j], rwkv_k_k[j], rwkv_k_a[j],
                              rwkv_r_k[j].reshape(-1), rwkv_gn_g[j], rwkv_gn_b[j])
            w_out = rwkv_w_out[j]
        xf = _matmul_res_ln(mix.reshape(m, -1), w_out.astype(BF16), xf, ln_mix_g[layer], ln_mix_b[layer], tm=512)
        xf = _ffn(xf, mlp_w1[layer], mlp_w2[layer], ln_ffn_g[layer], ln_ffn_b[layer])
    return xf.reshape(bsz, seq, d)
```

```python
import functools

import jax
import jax.numpy as jnp
from jax import lax
from jax.experimental import pallas as pl
from jax.experimental.pallas import tpu as pltpu

F32 = jnp.float32
BF16 = jnp.bfloat16
HIGHEST = lax.Precision.HIGHEST

DEPTH = 2
DN_ALPHA = (2.0 * DEPTH) ** 0.25
LN_EPS = 1e-5

M_HEADS = 4
M_DK = 128
M_DV = 256
M_GATE_CAP = 15.0
M_CHUNK = 256

R_N = 64
R_GN_EPS = 64e-5
R_CHUNK = 64
R_GROUP = 2
R_PAIRS = 4

VMEM_LIMIT = 48 * 1024 * 1024


def _cparams(sem):
    return pltpu.CompilerParams(dimension_semantics=sem, vmem_limit_bytes=VMEM_LIMIT)


def _dot(a, b, precision=None):
    return jnp.dot(a, b, preferred_element_type=F32, precision=precision)


def _dot_nt(a, b, precision=None):
    return lax.dot_general(a, b, (((1,), (1,)), ((), ())), preferred_element_type=F32, precision=precision)


def _dot_tn(a, b, precision=None):
    return lax.dot_general(a, b, (((0,), (0,)), ((), ())), preferred_element_type=F32, precision=precision)


def _mm_body(x_ref, w_ref, o_ref, *, act):
    acc = _dot(x_ref[...].astype(BF16), w_ref[...])
    if act == "relu2":
        acc = jnp.square(jnp.maximum(acc, 0.0))
    o_ref[...] = acc.astype(o_ref.dtype)


def _matmul(x, w, *, tm, tn, out_dtype, act=None):
    m, k = x.shape
    n = w.shape[1]
    tm = min(tm, m)
    assert m % tm == 0 and n % tn == 0
    return pl.pallas_call(
        functools.partial(_mm_body, act=act),
        grid=(m // tm, n // tn),
        in_specs=[pl.BlockSpec((tm, k), lambda i, j: (i, 0)),
                  pl.BlockSpec((k, tn), lambda i, j: (0, j))],
        out_specs=pl.BlockSpec((tm, tn), lambda i, j: (i, j)),
        out_shape=jax.ShapeDtypeStruct((m, n), out_dtype),
        compiler_params=_cparams(("parallel", "parallel")),
    )(x, w)


def _mm_lerp_body(x_ref, w_ref, mu_ref, o_ref, xb_ref, carry_ref, *, tiles_per_seq):
    i = pl.program_id(0)
    j = pl.program_id(1)

    @pl.when(j == 0)
    def _():
        xb_ref[...] = x_ref[...].astype(BF16)

    @pl.when(i % tiles_per_seq == 0)
    def _():
        carry_ref[j] = jnp.zeros(carry_ref.shape[1:], F32)

    acc = _dot(xb_ref[...], w_ref[...])
    tm = acc.shape[0]
    rolled = pltpu.roll(acc, 1, 0)
    first = lax.broadcasted_iota(jnp.int32, (8, acc.shape[1]), 0) == 0
    top = jnp.where(first, carry_ref[j][0:1, :], rolled[0:8])
    shifted = jnp.concatenate([top, rolled[8:]], axis=0)
    o_ref[...] = acc + mu_ref[...] * (shifted - acc)
    carry_ref[j] = jnp.broadcast_to(acc[tm - 1:tm, :], carry_ref.shape[1:])


def _matmul_shift_lerp(x, w, mu, *, seq, tm, tn):
    m, k = x.shape
    n = w.shape[1]
    tm = min(tm, seq)
    assert seq % tm == 0 and n % tn == 0
    return pl.pallas_call(
        functools.partial(_mm_lerp_body, tiles_per_seq=seq // tm),
        grid=(m // tm, n // tn),
        in_specs=[pl.BlockSpec((tm, k), lambda i, j: (i, 0)),
                  pl.BlockSpec((k, tn), lambda i, j: (0, j)),
                  pl.BlockSpec((1, tn), lambda i, j: (0, j))],
        out_specs=pl.BlockSpec((tm, tn), lambda i, j: (i, j)),
        out_shape=jax.ShapeDtypeStruct((m, n), F32),
        scratch_shapes=[pltpu.VMEM((tm, k), BF16),
                        pltpu.VMEM((n // tn, 8, tn), F32)],
        compiler_params=_cparams(("arbitrary", "arbitrary")),
        name="proj_token_shift",
    )(x, w, mu.reshape(1, n))


def _mm_ln_body(x_ref, w_ref, r_ref, g_ref, b_ref, o_ref):
    acc = _dot(x_ref[...].astype(BF16), w_ref[...])
    y = DN_ALPHA * r_ref[...] + acc
    mu = jnp.mean(y, axis=-1, keepdims=True)
    yc = y - mu
    var = jnp.mean(jnp.square(yc), axis=-1, keepdims=True)
    o_ref[...] = yc * lax.rsqrt(var + LN_EPS) * g_ref[...] + b_ref[...]


def _matmul_res_ln(x, w, res, g, b, *, tm):
    m, k = x.shape
    n = w.shape[1]
    tm = min(tm, m)
    assert m % tm == 0
    return pl.pallas_call(
        _mm_ln_body,
        grid=(m // tm,),
        in_specs=[pl.BlockSpec((tm, k), lambda i: (i, 0)),
                  pl.BlockSpec((k, n), lambda i: (0, 0)),
                  pl.BlockSpec((tm, n), lambda i: (i, 0)),
                  pl.BlockSpec((1, n), lambda i: (0, 0)),
                  pl.BlockSpec((1, n), lambda i: (0, 0))],
        out_specs=pl.BlockSpec((tm, n), lambda i: (i, 0)),
        out_shape=jax.ShapeDtypeStruct((m, n), F32),
        compiler_params=_cparams(("parallel",)),
    )(x, w, res, g.reshape(1, n), b.reshape(1, n))


def _mlstm_body(q_ref, k_ref, v_ref, o_ref, gate_ref, ng_ref, h_ref, c_ref, n_ref, m_ref):
    lc = q_ref.shape[1]

    @pl.when(pl.program_id(2) == 0)
    def _():
        c_ref[...] = jnp.zeros_like(c_ref)
        n_ref[...] = jnp.zeros_like(n_ref)
        m_ref[...] = jnp.zeros_like(m_ref)

    q = q_ref[0] * (M_DK ** -0.5)
    k = k_ref[0]
    v = v_ref[0]
    gates = gate_ref[0, 0]
    f_row = gates[0:1, :]
    i_row = gates[1:2, :]
    ri = lax.broadcasted_iota(jnp.int32, (lc, lc), 0)
    ci = lax.broadcasted_iota(jnp.int32, (lc, lc), 1)
    causal = ci <= ri
    tri_u = jnp.where(ri <= ci, 1.0, 0.0).astype(F32)
    bcum_row = _dot(jnp.broadcast_to(f_row, (8, lc)), tri_u, HIGHEST)[0:1, :]
    bcum_col = jnp.sum(jnp.where(causal, f_row, 0.0), axis=-1, keepdims=True)
    i_col = jnp.sum(jnp.where(ci == ri, i_row, 0.0), axis=-1, keepdims=True)
    m_prev = m_ref[0:1, 0:1]
    c_prev = c_ref[...]
    n_prev = n_ref[0:1, :]

    log_d = jnp.where(causal, bcum_col - bcum_row + i_row, -jnp.inf)
    log_inter = bcum_col + m_prev
    m_row = jnp.maximum(jnp.max(log_d, axis=-1, keepdims=True), log_inter)
    d_mat = jnp.exp(log_d - m_row)
    inter = jnp.exp(log_inter - m_row)
    qb = q.astype(BF16)
    kb = k.astype(BF16)
    s = _dot_nt(qb, kb) * d_mat
    num = _dot(s.astype(BF16), v.astype(BF16)) + inter * _dot(qb, c_prev.astype(BF16))
    den = jnp.sum(s, axis=-1, keepdims=True) + inter * jnp.sum(q * n_prev, axis=-1, keepdims=True)
    hc = num / jnp.maximum(jnp.abs(den), jnp.exp(-m_row))

    b_last = bcum_col[lc - 1:lc, :]
    log_w = b_last - bcum_col + i_col
    m_new = jnp.maximum(b_last + m_prev, jnp.max(log_w, axis=0, keepdims=True))
    decay = jnp.exp(b_last + m_prev - m_new)
    kw = k * jnp.exp(log_w - m_new)
    c_ref[...] = decay * c_prev + _dot_tn(kw.astype(BF16), v.astype(BF16))
    n_ref[...] = jnp.broadcast_to(decay * n_prev + jnp.sum(kw, axis=0, keepdims=True), n_ref.shape)
    m_ref[...] = jnp.broadcast_to(m_new, m_ref.shape)

    hn = hc * lax.rsqrt(jnp.mean(jnp.square(hc), axis=-1, keepdims=True) + 1e-6)
    h_ref[0] = (hn * ng_ref[...] * jax.nn.sigmoid(o_ref[0])).astype(h_ref.dtype)


def _mlstm_core(qk, proj, gates, norm_g):
    bsz, seq, _ = qk.shape
    lc = gates.shape[-1]
    nc = seq // lc
    hdv = M_HEADS * M_DV
    return pl.pallas_call(
        _mlstm_body,
        grid=(bsz, M_HEADS, nc),
        in_specs=[pl.BlockSpec((1, lc, M_DK), lambda b, h, c: (b, c, h)),
                  pl.BlockSpec((1, lc, M_DK), lambda b, h, c: (b, c, M_HEADS + h)),
                  pl.BlockSpec((1, lc, M_DV), lambda b, h, c: (b, c, M_HEADS + h)),
                  pl.BlockSpec((1, lc, M_DV), lambda b, h, c: (b, c, 2 * M_HEADS + h)),
                  pl.BlockSpec((1, 1, 2, lc), lambda b, h, c: (b * M_HEADS + h, c, 0, 0)),
                  pl.BlockSpec((1, M_DV), lambda b, h, c: (0, h))],
        out_specs=pl.BlockSpec((1, lc, M_DV), lambda b, h, c: (b, c, h)),
        out_shape=jax.ShapeDtypeStruct((bsz, seq, hdv), BF16),
        scratch_shapes=[pltpu.VMEM((M_DK, M_DV), F32),
                        pltpu.VMEM((8, M_DK), F32),
                        pltpu.VMEM((8, 128), F32)],
        compiler_params=_cparams(("parallel", "parallel", "arbitrary")),
        name="mlstm_chunkwise",
    )(qk, qk, proj, proj, gates, norm_g.reshape(1, hdv))


def _softplus(z):
    return jnp.maximum(z, 0.0) + jnp.log1p(jnp.exp(-jnp.abs(z)))


def _mlstm_mixer(x, w_in, b_i, b_f, conv_w, conv_b, norm_g):
    bsz, seq, d = x.shape
    m = bsz * seq
    hdv = M_HEADS * M_DV
    n_main = 2 * M_HEADS * M_DK + 2 * hdv
    xf = x.reshape(m, d)
    proj = _matmul(xf, w_in[:, :n_main].astype(BF16), tm=512, tn=1024, out_dtype=F32)
    w_gate = jnp.pad(w_in[:, n_main:], ((0, 0), (0, 128 - 2 * M_HEADS))).astype(BF16)
    gate_pre = _matmul(xf, w_gate, tm=512, tn=128, out_dtype=F32)[:, :2 * M_HEADS]
    proj = proj.reshape(bsz, seq, n_main)
    qk_pre = proj[..., :2 * M_HEADS * M_DK]
    kc = conv_w.shape[0]
    padded = jnp.pad(qk_pre, ((0, 0), (kc - 1, 0), (0, 0)))
    conv = conv_b
    for j in range(kc):
        conv = conv + conv_w[j] * padded[:, j:j + seq, :]
    qk = jax.nn.silu(conv)
    gate_pre = gate_pre.reshape(bsz, seq, 2 * M_HEADS)
    log_i = M_GATE_CAP * jnp.tanh((gate_pre[..., :M_HEADS] + b_i) / M_GATE_CAP)
    log_f = -_softplus(-(M_GATE_CAP * jnp.tanh((gate_pre[..., M_HEADS:] + b_f) / M_GATE_CAP)))
    lc = min(M_CHUNK, seq)
    gates = jnp.stack([log_f, log_i], axis=0)
    gates = gates.reshape(2, bsz, seq // lc, lc, M_HEADS).transpose(1, 4, 2, 0, 3)
    gates = gates.reshape(bsz * M_HEADS, seq // lc, 2, lc)
    return _mlstm_core(qk, proj, gates, norm_g)


def _split_bf16(x):
    hi = x.astype(BF16)
    return hi, (x - hi.astype(F32)).astype(BF16)


def _dot3(a, b):
    ah, al = _split_bf16(a)
    bh, bl = _split_bf16(b)
    return _dot(jnp.concatenate([ah, ah, al], axis=1), jnp.concatenate([bh, bl, bh], axis=0))


def _wkv_body(r_ref, k_ref, v_ref, tail_ref, w2_ref, a2_ref, g2_ref, w0_ref, a0_ref, kk_ref, ka_ref, rk_ref,
              gng_ref, gnb_ref, o_ref, s_ref, *, n_tanh):
    lc = r_ref.shape[1]
    width = r_ref.shape[2]
    gw = R_GROUP * R_N
    rows = R_GROUP * lc

    @pl.when(pl.program_id(2) == 0)
    def _():
        s_ref[...] = jnp.zeros_like(s_ref)

    lane = lax.broadcasted_iota(jnp.int32, (1, width), 1)
    head_of_lane = lane // R_N

    def seg_sum(x):
        out = jnp.zeros_like(x)
        for h in range(width // R_N):
            msk = head_of_lane == h
            out = jnp.where(msk, jnp.sum(jnp.where(msk, x, 0.0), axis=-1, keepdims=True), out)
        return out

    r = r_ref[0]
    k = k_ref[0]
    v = v_ref[0]
    n_wa = w2_ref.shape[0]
    tail = tail_ref[0]
    t_wa = tail[:, :n_wa]
    is_tanh = lax.broadcasted_iota(jnp.int32, (1, n_wa), 1) < n_tanh
    f_wa = jnp.where(is_tanh, jnp.tanh(t_wa), t_wa).astype(BF16)
    gate = _dot(jax.nn.sigmoid(tail[:, n_wa:]).astype(BF16), g2_ref[...])
    log_w = -_softplus(-(w0_ref[...] + _dot(f_wa, w2_ref[...]))) - 0.5
    ld = -jnp.exp(log_w)
    a_lr = jax.nn.sigmoid(a0_ref[...] + _dot(f_wa, a2_ref[...]))
    kk = k * kk_ref[...]
    kk = kk / jnp.maximum(jnp.sqrt(seg_sum(jnp.square(kk))), 1e-12)
    k = k * (1.0 + (a_lr - 1.0) * ka_ref[...])
    b_ = kk * a_lr

    ti = lax.broadcasted_iota(jnp.int32, (lc, lc), 0)
    tj = lax.broadcasted_iota(jnp.int32, (lc, lc), 1)
    cum = _dot(jnp.where(tj <= ti, 1.0, 0.0).astype(F32), ld, HIGHEST)
    c_last = cum[lc - 1:lc, :]
    e_pos = jnp.exp(cum)
    e_neg = jnp.exp(-cum)
    e_end = jnp.exp(c_last - cum)
    w_end = jnp.exp(c_last)
    a_til = -kk * jnp.exp(cum - ld)
    r_til = r * e_pos
    b_til = b_ * e_neg
    k_til = k * e_neg
    b_hat = b_ * e_end
    k_hat = k * e_end

    pair_lane = lax.broadcasted_iota(jnp.int32, (1, gw), 1)
    pair_masks = [pair_lane // R_N == h for h in range(R_GROUP)]

    def stack(x):
        return jnp.concatenate([jnp.where(msk, x, 0.0) for msk in pair_masks], axis=0)

    ri = lax.broadcasted_iota(jnp.int32, (rows, rows), 0)
    ci = lax.broadcasted_iota(jnp.int32, (rows, rows), 1)
    strict = ci < ri
    ri2 = lax.broadcasted_iota(jnp.int32, (2 * rows, rows), 0)
    ci2 = lax.broadcasted_iota(jnp.int32, (2 * rows, rows), 1)
    causal2 = ci2 < jnp.where(ri2 < rows, ri2, ri2 - rows + 1)
    eye = jnp.where(ci == ri, 1.0, 0.0).astype(F32)
    level_masks = []
    s = 1
    while s < lc:
        lo, hi = s.bit_length() - 1, s.bit_length()
        level_masks.append(jnp.logical_and((ri >> hi) == (ci >> hi), (ri >> lo) != (ci >> lo)))
        s *= 2

    pairs = range(width // gw)
    sls = [slice(p * gw, (p + 1) * gw) for p in pairs]
    ar_b = [jnp.concatenate([stack(a_til[:, sl]), stack(r_til[:, sl])], axis=0).astype(BF16) for sl in sls]
    bk_b = [jnp.concatenate([stack(b_til[:, sl]), stack(k_til[:, sl])], axis=0).astype(BF16) for sl in sls]
    v_sb = [stack(v[:, sl]).astype(BF16) for sl in sls]
    bk_hat = [jnp.concatenate([stack(b_hat[:, sl]), stack(k_hat[:, sl])], axis=0).astype(BF16) for sl in sls]

    pm = [_dot_nt(ar_b[p], bk_b[p]) for p in pairs]
    n_ab = [jnp.where(strict, pm[p][:rows, :rows], 0.0) for p in pairs]
    m_xk = [jnp.where(causal2, pm[p][:, rows:], 0.0).astype(BF16) for p in pairs]
    m_rb = [jnp.where(causal2[rows:], pm[p][rows:, :rows], 0.0).astype(BF16) for p in pairs]

    t_inv = [eye + jnp.where(level_masks[0], n_ab[p], 0.0) for p in pairs]
    for msk in level_masks[1:]:
        half = [_dot3(t_inv[p], jnp.where(msk, n_ab[p], 0.0)) for p in pairs]
        t_inv = [t_inv[p] + _dot3(half[p], t_inv[p]) for p in pairs]

    s0 = [s_ref[p] for p in pairs]
    zy = [_dot_nt(ar_b[p], s0[p].astype(BF16)) + _dot(m_xk[p], v_sb[p]) for p in pairs]
    ub = []
    for p in pairs:
        t_hi, t_lo = _split_bf16(t_inv[p])
        zb = zy[p][:rows].astype(BF16)
        ub.append(_dot(jnp.concatenate([t_hi, t_lo], axis=1), jnp.concatenate([zb, zb], axis=0)).astype(BF16))
    y_bd = [zy[p][rows:] + _dot(m_rb[p], ub[p]) for p in pairs]
    for p in pairs:
        s_ref[p] = s0[p] * w_end[:, sls[p]] + _dot_tn(jnp.concatenate([ub[p], v_sb[p]], axis=0), bk_hat[p])
    y = jnp.concatenate([functools.reduce(jnp.add, [y_bd[p][h * lc:(h + 1) * lc] for h in range(R_GROUP)])
                         for p in pairs], axis=1)

    mu = seg_sum(y) * (1.0 / R_N)
    yc = y - mu
    var = seg_sum(jnp.square(yc)) * (1.0 / R_N)
    yn = yc * lax.rsqrt(var + R_GN_EPS) * gng_ref[...] + gnb_ref[...]
    bonus = seg_sum(r * k * rk_ref[...]) * v
    o_ref[0] = ((yn + bonus) * gate).astype(o_ref.dtype)


def _wkv_core(proj, w2, a2, g2, w0, a0, k_k, k_a, r_k, gn_g, gn_b):
    bsz, seq, n_proj = proj.shape
    d = w0.shape[-1]
    lw, la, lg = w2.shape[0], a2.shape[0], g2.shape[0]
    n_tail = lw + la + lg
    assert n_proj == 3 * d + n_tail and (3 * d) % n_tail == 0
    gw = R_GROUP * R_N
    width = R_PAIRS * gw
    nb = d // width
    lc = min(R_CHUNK, seq)
    nc = seq // lc
    w2p = jnp.concatenate([w2, jnp.zeros_like(a2)], axis=0).astype(BF16)
    a2p = jnp.concatenate([jnp.zeros_like(w2), a2], axis=0).astype(BF16)
    tile = lambda off: pl.BlockSpec((1, lc, width), lambda b, g, c: (b, c, off * nb + g))
    tail = pl.BlockSpec((1, lc, n_tail), lambda b, g, c: (b, c, 3 * d // n_tail))
    wa = pl.BlockSpec((lw + la, width), lambda b, g, c: (0, g))
    wg = pl.BlockSpec((lg, width), lambda b, g, c: (0, g))
    par = pl.BlockSpec((1, width), lambda b, g, c: (0, g))
    row = lambda p: p.reshape(1, d)
    return pl.pallas_call(
        functools.partial(_wkv_body, n_tanh=lw),
        grid=(bsz, nb, nc),
        in_specs=[tile(0), tile(1), tile(2), tail, wa, wa, wg] + [par] * 7,
        out_specs=pl.BlockSpec((1, lc, width), lambda b, g, c: (b, c, g)),
        out_shape=jax.ShapeDtypeStruct((bsz, seq, d), BF16),
        scratch_shapes=[pltpu.VMEM((R_PAIRS, gw, gw), F32)],
        compiler_params=_cparams(("parallel", "parallel", "arbitrary")),
        name="wkv7_chunked",
    )(proj, proj, proj, proj, w2p, a2p, g2.astype(BF16),
      row(w0), row(a0), row(k_k), row(k_a), row(r_k), row(gn_g), row(gn_b))


def _rwkv_mixer(x, w_in, mu, w0, w2, a0, a2, g2, k_k, k_a, r_k, gn_g, gn_b):
    bsz, seq, d = x.shape
    n_proj = w_in.shape[1]
    proj = _matmul_shift_lerp(x.reshape(bsz * seq, d), w_in.astype(BF16), mu, seq=seq, tm=512, tn=n_proj // 2)
    return _wkv_core(proj.reshape(bsz, seq, n_proj), w2, a2, g2, w0, a0, k_k, k_a, r_k, gn_g, gn_b)


def _ffn(xf, w1, w2, g, b):
    hidden = _matmul(xf, w1.astype(BF16), tm=512, tn=1024, out_dtype=BF16, act="relu2")
    return _matmul_res_ln(hidden, w2.astype(BF16), xf, g, b, tm=256)


def kernel(x, mlstm_w_in, mlstm_b_i, mlstm_b_f, mlstm_conv_w, mlstm_conv_b, mlstm_norm_g, mlstm_w_out, rwkv_w_in, rwkv_mu, rwkv_w0, rwkv_w2, rwkv_a0, rwkv_a2, rwkv_g2, rwkv_k_k, rwkv_k_a, rwkv_r_k, rwkv_gn_g, rwkv_gn_b, rwkv_w_out, ln_mix_g, ln_mix_b, mlp_w1, mlp_w2, ln_ffn_g, ln_ffn_b):
    bsz, seq, d = x.shape
    m = bsz * seq
    xf = x.reshape(m, d)
    for layer in range(DEPTH):
        j = layer // 2
        if layer % 2 == 0:
            mix = _mlstm_mixer(xf.reshape(bsz, seq, d), mlstm_w_in[j], mlstm_b_i[j], mlstm_b_f[j],
                               mlstm_conv_w[j], mlstm_conv_b[j], mlstm_norm_g[j])
            w_out = mlstm_w_out[j]
        else:
            mix = _rwkv_mixer(xf.reshape(bsz, seq, d), rwkv_w_in[j], rwkv_mu[j], rwkv_w0[j], rwkv_w2[j],
                              rwkv_a0[j], rwkv_a2[j], rwkv_g2[j], rwkv_k_k[j], rwkv_k_a[j],
                              rwkv_r_k[j].reshape(-1), rwkv_gn_g[j], rwkv_gn_b[j])
            w_out = rwkv_w_out[j]
        xf = _matmul_res_ln(mix.reshape(m, -1), w_out.astype(BF16), xf, ln_mix_g[layer], ln_mix_b[layer], tm=512)
        xf = _ffn(xf, mlp_w1[layer], mlp_w2[layer], ln_ffn_g[layer], ln_ffn_b[layer])
    return xf.reshape(bsz, seq, d)
```

```python
import functools

import jax
import jax.numpy as jnp
from jax import lax
from jax.experimental import pallas as pl
from jax.experimental.pallas import tpu as pltpu

F32 = jnp.float32
BF16 = jnp.bfloat16
HIGHEST = lax.Precision.HIGHEST

DEPTH = 2
DN_ALPHA = (2.0 * DEPTH) ** 0.25
LN_EPS = 1e-5

M_HEADS = 4
M_DK = 128
M_DV = 256
M_GATE_CAP = 15.0
M_CHUNK = 256

R_N = 64
R_GN_EPS = 64e-5
R_CHUNK = 64
R_GROUP = 2
R_PAIRS = 8

VMEM_LIMIT = 48 * 1024 * 1024


def _cparams(sem):
    return pltpu.CompilerParams(dimension_semantics=sem, vmem_limit_bytes=VMEM_LIMIT)


def _dot(a, b, precision=None):
    return jnp.dot(a, b, preferred_element_type=F32, precision=precision)


def _dot_nt(a, b, precision=None):
    return lax.dot_general(a, b, (((1,), (1,)), ((), ())), preferred_element_type=F32, precision=precision)


def _dot_tn(a, b, precision=None):
    return lax.dot_general(a, b, (((0,), (0,)), ((), ())), preferred_element_type=F32, precision=precision)


def _mm_body(x_ref, w_ref, o_ref, *, act):
    acc = _dot(x_ref[...].astype(BF16), w_ref[...])
    if act == "relu2":
        acc = jnp.square(jnp.maximum(acc, 0.0))
    o_ref[...] = acc.astype(o_ref.dtype)


def _matmul(x, w, *, tm, tn, out_dtype, act=None):
    m, k = x.shape
    n = w.shape[1]
    tm = min(tm, m)
    assert m % tm == 0 and n % tn == 0
    return pl.pallas_call(
        functools.partial(_mm_body, act=act),
        grid=(m // tm, n // tn),
        in_specs=[pl.BlockSpec((tm, k), lambda i, j: (i, 0)),
                  pl.BlockSpec((k, tn), lambda i, j: (0, j))],
        out_specs=pl.BlockSpec((tm, tn), lambda i, j: (i, j)),
        out_shape=jax.ShapeDtypeStruct((m, n), out_dtype),
        compiler_params=_cparams(("parallel", "parallel")),
    )(x, w)


def _mm_lerp_body(x_ref, w_ref, mu_ref, o_ref, xb_ref, carry_ref, *, tiles_per_seq):
    i = pl.program_id(0)
    j = pl.program_id(1)

    @pl.when(j == 0)
    def _():
        xb_ref[...] = x_ref[...].astype(BF16)

    @pl.when(i % tiles_per_seq == 0)
    def _():
        carry_ref[j] = jnp.zeros(carry_ref.shape[1:], F32)

    acc = _dot(xb_ref[...], w_ref[...])
    tm = acc.shape[0]
    rolled = pltpu.roll(acc, 1, 0)
    first = lax.broadcasted_iota(jnp.int32, (8, acc.shape[1]), 0) == 0
    top = jnp.where(first, carry_ref[j][0:1, :], rolled[0:8])
    shifted = jnp.concatenate([top, rolled[8:]], axis=0)
    o_ref[...] = acc + mu_ref[...] * (shifted - acc)
    carry_ref[j] = jnp.broadcast_to(acc[tm - 1:tm, :], carry_ref.shape[1:])


def _matmul_shift_lerp(x, w, mu, *, seq, tm, tn):
    m, k = x.shape
    n = w.shape[1]
    tm = min(tm, seq)
    assert seq % tm == 0 and n % tn == 0
    return pl.pallas_call(
        functools.partial(_mm_lerp_body, tiles_per_seq=seq // tm),
        grid=(m // tm, n // tn),
        in_specs=[pl.BlockSpec((tm, k), lambda i, j: (i, 0)),
                  pl.BlockSpec((k, tn), lambda i, j: (0, j)),
                  pl.BlockSpec((1, tn), lambda i, j: (0, j))],
        out_specs=pl.BlockSpec((tm, tn), lambda i, j: (i, j)),
        out_shape=jax.ShapeDtypeStruct((m, n), F32),
        scratch_shapes=[pltpu.VMEM((tm, k), BF16),
                        pltpu.VMEM((n // tn, 8, tn), F32)],
        compiler_params=_cparams(("arbitrary", "arbitrary")),
        name="proj_token_shift",
    )(x, w, mu.reshape(1, n))


def _mm_ln_body(x_ref, w_ref, r_ref, g_ref, b_ref, o_ref):
    acc = _dot(x_ref[...].astype(BF16), w_ref[...])
    y = DN_ALPHA * r_ref[...] + acc
    mu = jnp.mean(y, axis=-1, keepdims=True)
    yc = y - mu
    var = jnp.mean(jnp.square(yc), axis=-1, keepdims=True)
    o_ref[...] = yc * lax.rsqrt(var + LN_EPS) * g_ref[...] + b_ref[...]


def _matmul_res_ln(x, w, res, g, b, *, tm):
    m, k = x.shape
    n = w.shape[1]
    tm = min(tm, m)
    assert m % tm == 0
    return pl.pallas_call(
        _mm_ln_body,
        grid=(m // tm,),
        in_specs=[pl.BlockSpec((tm, k), lambda i: (i, 0)),
                  pl.BlockSpec((k, n), lambda i: (0, 0)),
                  pl.BlockSpec((tm, n), lambda i: (i, 0)),
                  pl.BlockSpec((1, n), lambda i: (0, 0)),
                  pl.BlockSpec((1, n), lambda i: (0, 0))],
        out_specs=pl.BlockSpec((tm, n), lambda i: (i, 0)),
        out_shape=jax.ShapeDtypeStruct((m, n), F32),
        compiler_params=_cparams(("parallel",)),
    )(x, w, res, g.reshape(1, n), b.reshape(1, n))


def _mlstm_body(q_ref, k_ref, v_ref, o_ref, gate_ref, ng_ref, h_ref, c_ref, n_ref, m_ref):
    lc = q_ref.shape[1]

    @pl.when(pl.program_id(2) == 0)
    def _():
        c_ref[...] = jnp.zeros_like(c_ref)
        n_ref[...] = jnp.zeros_like(n_ref)
        m_ref[...] = jnp.zeros_like(m_ref)

    q = q_ref[0] * (M_DK ** -0.5)
    k = k_ref[0]
    v = v_ref[0]
    gates = gate_ref[0, 0]
    f_row = gates[0:1, :]
    i_row = gates[1:2, :]
    ri = lax.broadcasted_iota(jnp.int32, (lc, lc), 0)
    ci = lax.broadcasted_iota(jnp.int32, (lc, lc), 1)
    causal = ci <= ri
    tri_u = jnp.where(ri <= ci, 1.0, 0.0).astype(F32)
    bcum_row = _dot(jnp.broadcast_to(f_row, (8, lc)), tri_u, HIGHEST)[0:1, :]
    bcum_col = jnp.sum(jnp.where(causal, f_row, 0.0), axis=-1, keepdims=True)
    i_col = jnp.sum(jnp.where(ci == ri, i_row, 0.0), axis=-1, keepdims=True)
    m_prev = m_ref[0:1, 0:1]
    c_prev = c_ref[...]
    n_prev = n_ref[0:1, :]

    log_d = jnp.where(causal, bcum_col - bcum_row + i_row, -jnp.inf)
    log_inter = bcum_col + m_prev
    m_row = jnp.maximum(jnp.max(log_d, axis=-1, keepdims=True), log_inter)
    d_mat = jnp.exp(log_d - m_row)
    inter = jnp.exp(log_inter - m_row)
    qb = q.astype(BF16)
    kb = k.astype(BF16)
    s = _dot_nt(qb, kb) * d_mat
    num = _dot(s.astype(BF16), v.astype(BF16)) + inter * _dot(qb, c_prev.astype(BF16))
    den = jnp.sum(s, axis=-1, keepdims=True) + inter * jnp.sum(q * n_prev, axis=-1, keepdims=True)
    hc = num / jnp.maximum(jnp.abs(den), jnp.exp(-m_row))

    b_last = bcum_col[lc - 1:lc, :]
    log_w = b_last - bcum_col + i_col
    m_new = jnp.maximum(b_last + m_prev, jnp.max(log_w, axis=0, keepdims=True))
    decay = jnp.exp(b_last + m_prev - m_new)
    kw = k * jnp.exp(log_w - m_new)
    c_ref[...] = decay * c_prev + _dot_tn(kw.astype(BF16), v.astype(BF16))
    n_ref[...] = jnp.broadcast_to(decay * n_prev + jnp.sum(kw, axis=0, keepdims=True), n_ref.shape)
    m_ref[...] = jnp.broadcast_to(m_new, m_ref.shape)

    hn = hc * lax.rsqrt(jnp.mean(jnp.square(hc), axis=-1, keepdims=True) + 1e-6)
    h_ref[0] = (hn * ng_ref[...] * jax.nn.sigmoid(o_ref[0])).astype(h_ref.dtype)


def _mlstm_core(qk, proj, gates, norm_g):
    bsz, seq, _ = qk.shape
    lc = gates.shape[-1]
    nc = seq // lc
    hdv = M_HEADS * M_DV
    return pl.pallas_call(
        _mlstm_body,
        grid=(bsz, M_HEADS, nc),
        in_specs=[pl.BlockSpec((1, lc, M_DK), lambda b, h, c: (b, c, h)),
                  pl.BlockSpec((1, lc, M_DK), lambda b, h, c: (b, c, M_HEADS + h)),
                  pl.BlockSpec((1, lc, M_DV), lambda b, h, c: (b, c, M_HEADS + h)),
                  pl.BlockSpec((1, lc, M_DV), lambda b, h, c: (b, c, 2 * M_HEADS + h)),
                  pl.BlockSpec((1, 1, 2, lc), lambda b, h, c: (b * M_HEADS + h, c, 0, 0)),
                  pl.BlockSpec((1, M_DV), lambda b, h, c: (0, h))],
        out_specs=pl.BlockSpec((1, lc, M_DV), lambda b, h, c: (b, c, h)),
        out_shape=jax.ShapeDtypeStruct((bsz, seq, hdv), BF16),
        scratch_shapes=[pltpu.VMEM((M_DK, M_DV), F32),
                        pltpu.VMEM((8, M_DK), F32),
                        pltpu.VMEM((8, 128), F32)],
        compiler_params=_cparams(("parallel", "parallel", "arbitrary")),
        name="mlstm_chunkwise",
    )(qk, qk, proj, proj, gates, norm_g.reshape(1, hdv))


def _softplus(z):
    return jnp.maximum(z, 0.0) + jnp.log1p(jnp.exp(-jnp.abs(z)))


def _mlstm_mixer(x, w_in, b_i, b_f, conv_w, conv_b, norm_g):
    bsz, seq, d = x.shape
    m = bsz * seq
    hdv = M_HEADS * M_DV
    n_main = 2 * M_HEADS * M_DK + 2 * hdv
    xf = x.reshape(m, d)
    proj = _matmul(xf, w_in[:, :n_main].astype(BF16), tm=512, tn=1024, out_dtype=F32)
    w_gate = jnp.pad(w_in[:, n_main:], ((0, 0), (0, 128 - 2 * M_HEADS))).astype(BF16)
    gate_pre = _matmul(xf, w_gate, tm=512, tn=128, out_dtype=F32)[:, :2 * M_HEADS]
    proj = proj.reshape(bsz, seq, n_main)
    qk_pre = proj[..., :2 * M_HEADS * M_DK]
    kc = conv_w.shape[0]
    padded = jnp.pad(qk_pre, ((0, 0), (kc - 1, 0), (0, 0)))
    conv = conv_b
    for j in range(kc):
        conv = conv + conv_w[j] * padded[:, j:j + seq, :]
    qk = jax.nn.silu(conv)
    gate_pre = gate_pre.reshape(bsz, seq, 2 * M_HEADS)
    log_i = M_GATE_CAP * jnp.tanh((gate_pre[..., :M_HEADS] + b_i) / M_GATE_CAP)
    log_f = -_softplus(-(M_GATE_CAP * jnp.tanh((gate_pre[..., M_HEADS:] + b_f) / M_GATE_CAP)))
    lc = min(M_CHUNK, seq)
    gates = jnp.stack([log_f, log_i], axis=0)
    gates = gates.reshape(2, bsz, seq // lc, lc, M_HEADS).transpose(1, 4, 2, 0, 3)
    gates = gates.reshape(bsz * M_HEADS, seq // lc, 2, lc)
    return _mlstm_core(qk, proj, gates, norm_g)


def _split_bf16(x):
    hi = x.astype(BF16)
    return hi, (x - hi.astype(F32)).astype(BF16)


def _dot3(a, b):
    ah, al = _split_bf16(a)
    bh, bl = _split_bf16(b)
    return _dot(jnp.concatenate([ah, ah, al], axis=1), jnp.concatenate([bh, bl, bh], axis=0))


def _wkv_body(r_ref, k_ref, v_ref, tail_ref, w2_ref, a2_ref, g2_ref, w0_ref, a0_ref, kk_ref, ka_ref, rk_ref,
              gng_ref, gnb_ref, o_ref, s_ref, *, n_tanh):
    lc = r_ref.shape[1]
    width = r_ref.shape[2]
    gw = R_GROUP * R_N
    rows = R_GROUP * lc
    assert rows == gw

    @pl.when(pl.program_id(2) == 0)
    def _():
        s_ref[...] = jnp.zeros_like(s_ref)

    lane = lax.broadcasted_iota(jnp.int32, (1, width), 1)
    head_of_lane = lane // R_N

    def seg_sum(x):
        out = jnp.zeros_like(x)
        for h in range(width // R_N):
            msk = head_of_lane == h
            out = jnp.where(msk, jnp.sum(jnp.where(msk, x, 0.0), axis=-1, keepdims=True), out)
        return out

    r = r_ref[0]
    k = k_ref[0]
    v = v_ref[0]
    n_wa = w2_ref.shape[0]
    tail = tail_ref[0]
    t_wa = tail[:, :n_wa]
    is_tanh = lax.broadcasted_iota(jnp.int32, (1, n_wa), 1) < n_tanh
    f_wa = jnp.where(is_tanh, jnp.tanh(t_wa), t_wa).astype(BF16)
    gate = _dot(jax.nn.sigmoid(tail[:, n_wa:]).astype(BF16), g2_ref[...])
    log_w = -_softplus(-(w0_ref[...] + _dot(f_wa, w2_ref[...]))) - 0.5
    ld = -jnp.exp(log_w)
    a_lr = jax.nn.sigmoid(a0_ref[...] + _dot(f_wa, a2_ref[...]))
    kk = k * kk_ref[...]
    kk = kk / jnp.maximum(jnp.sqrt(seg_sum(jnp.square(kk))), 1e-12)
    k = k * (1.0 + (a_lr - 1.0) * ka_ref[...])
    b_ = kk * a_lr

    ti = lax.broadcasted_iota(jnp.int32, (lc, lc), 0)
    tj = lax.broadcasted_iota(jnp.int32, (lc, lc), 1)
    cum = _dot(jnp.where(tj <= ti, 1.0, 0.0).astype(F32), ld, HIGHEST)
    c_last = cum[lc - 1:lc, :]
    e_pos = jnp.exp(cum)
    e_neg = jnp.exp(-cum)
    e_end = jnp.exp(c_last - cum)
    w_end = jnp.exp(c_last)
    a_til = -kk * jnp.exp(cum - ld)
    r_til = r * e_pos
    b_til = b_ * e_neg
    k_til = k * e_neg
    b_hat = b_ * e_end
    k_hat = k * e_end

    lane_head = lax.broadcasted_iota(jnp.int32, (lc, gw), 1) // R_N
    head_sel = [lane_head == h for h in range(R_GROUP)]

    def stack(x):
        zero = jnp.zeros_like(x)
        return jnp.concatenate([jnp.where(sel, x, zero) for sel in head_sel], axis=0)

    trow = lax.broadcasted_iota(jnp.int32, (lc, gw), 0)
    tcol = lax.broadcasted_iota(jnp.int32, (lc, gw), 1) % lc
    strict = tcol < trow
    causal = tcol <= trow
    eye = jnp.where(tcol == trow, 1.0, 0.0).astype(F32)
    level_masks = []
    s = 1
    while s < lc:
        lo, hi = s.bit_length() - 1, s.bit_length()
        level_masks.append(jnp.logical_and((trow >> hi) == (tcol >> hi), (trow >> lo) != (tcol >> lo)))
        s *= 2
    brow = lax.broadcasted_iota(jnp.int32, (gw, gw), 0) // R_N
    bcol = lax.broadcasted_iota(jnp.int32, (gw, gw), 1) // R_N
    same_head = brow == bcol

    pairs = range(width // gw)
    sls = [slice(p * gw, (p + 1) * gw) for p in pairs]
    ar_b = [jnp.concatenate([a_til[:, sl], r_til[:, sl]], axis=0).astype(BF16) for sl in sls]
    bk_s = [jnp.concatenate([stack(b_til[:, sl].astype(BF16)), stack(k_til[:, sl].astype(BF16))], axis=0)
            for sl in sls]
    v_b = [v[:, sl].astype(BF16) for sl in sls]
    v_s = [stack(v_b[p]) for p in pairs]
    bk_hat = [jnp.concatenate([b_hat[:, sl], k_hat[:, sl]], axis=0).astype(BF16) for sl in sls]

    pm = [_dot_nt(ar_b[p], bk_s[p]) for p in pairs]
    n_ab = [jnp.where(strict, pm[p][:lc, :rows], 0.0) for p in pairs]
    m_xk = [jnp.concatenate([jnp.where(strict, pm[p][:lc, rows:], 0.0),
                             jnp.where(causal, pm[p][lc:, rows:], 0.0)], axis=0).astype(BF16) for p in pairs]
    m_rb = [jnp.where(causal, pm[p][lc:, :rows], 0.0).astype(BF16) for p in pairs]

    t_inv = [eye + jnp.where(level_masks[0], n_ab[p], 0.0) for p in pairs]
    for msk in level_masks[1:]:
        t_hl = [_split_bf16(t_inv[p]) for p in pairs]
        c_s = [stack(jnp.where(msk, n_ab[p], 0.0).astype(BF16)) for p in pairs]
        half = [_dot(jnp.concatenate(t_hl[p], axis=1), jnp.concatenate([c_s[p], c_s[p]], axis=0)) for p in pairs]
        h_hl = [_split_bf16(half[p]) for p in pairs]
        t_inv = [t_inv[p] + _dot(jnp.concatenate([h_hl[p][0], h_hl[p][0], h_hl[p][1]], axis=1),
                                 jnp.concatenate([stack(t_hl[p][0]), stack(t_hl[p][1]), stack(t_hl[p][0])], axis=0))
                 for p in pairs]

    s0 = [s_ref[p] for p in pairs]
    zy = [_dot_nt(ar_b[p], s0[p].astype(BF16)) + _dot(m_xk[p], v_s[p]) for p in pairs]
    ub = []
    for p in pairs:
        z_s = stack(zy[p][:lc].astype(BF16))
        ub.append(_dot(jnp.concatenate(_split_bf16(t_inv[p]), axis=1), jnp.concatenate([z_s, z_s], axis=0)).astype(BF16))
    ys = [zy[p][lc:] + _dot(m_rb[p], stack(ub[p])) for p in pairs]
    for p in pairs:
        upd = _dot_tn(jnp.concatenate([ub[p], v_b[p]], axis=0), bk_hat[p])
        s_ref[p] = s0[p] * w_end[:, sls[p]] + jnp.where(same_head, upd, 0.0)
    y = jnp.concatenate(ys, axis=1) if len(ys) > 1 else ys[0]


    mu = seg_sum(y) * (1.0 / R_N)
    yc = y - mu
    var = seg_sum(jnp.square(yc)) * (1.0 / R_N)
    yn = yc * lax.rsqrt(var + R_GN_EPS) * gng_ref[...] + gnb_ref[...]
    bonus = seg_sum(r * k * rk_ref[...]) * v
    o_ref[0] = ((yn + bonus) * gate).astype(o_ref.dtype)


def _wkv_core(proj, w2, a2, g2, w0, a0, k_k, k_a, r_k, gn_g, gn_b):
    bsz, seq, n_proj = proj.shape
    d = w0.shape[-1]
    lw, la, lg = w2.shape[0], a2.shape[0], g2.shape[0]
    n_tail = lw + la + lg
    assert n_proj == 3 * d + n_tail and (3 * d) % n_tail == 0
    gw = R_GROUP * R_N
    width = R_PAIRS * gw
    nb = d // width
    lc = min(R_CHUNK, seq)
    nc = seq // lc
    w2p = jnp.concatenate([w2, jnp.zeros_like(a2)], axis=0).astype(BF16)
    a2p = jnp.concatenate([jnp.zeros_like(w2), a2], axis=0).astype(BF16)
    tile = lambda off: pl.BlockSpec((1, lc, width), lambda b, g, c: (b, c, off * nb + g))
    tail = pl.BlockSpec((1, lc, n_tail), lambda b, g, c: (b, c, 3 * d // n_tail))
    wa = pl.BlockSpec((lw + la, width), lambda b, g, c: (0, g))
    wg = pl.BlockSpec((lg, width), lambda b, g, c: (0, g))
    par = pl.BlockSpec((1, width), lambda b, g, c: (0, g))
    row = lambda p: p.reshape(1, d)
    return pl.pallas_call(
        functools.partial(_wkv_body, n_tanh=lw),
        grid=(bsz, nb, nc),
        in_specs=[tile(0), tile(1), tile(2), tail, wa, wa, wg] + [par] * 7,
        out_specs=pl.BlockSpec((1, lc, width), lambda b, g, c: (b, c, g)),
        out_shape=jax.ShapeDtypeStruct((bsz, seq, d), BF16),
        scratch_shapes=[pltpu.VMEM((R_PAIRS, gw, gw), F32)],
        compiler_params=_cparams(("parallel", "parallel", "arbitrary")),
        name="wkv7_chunked",
    )(proj, proj, proj, proj, w2p, a2p, g2.astype(BF16),
      row(w0), row(a0), row(k_k), row(k_a), row(r_k), row(gn_g), row(gn_b))


def _rwkv_mixer(x, w_in, mu, w0, w2, a0, a2, g2, k_k, k_a, r_k, gn_g, gn_b):
    bsz, seq, d = x.shape
    n_proj = w_in.shape[1]
    proj = _matmul_shift_lerp(x.reshape(bsz * seq, d), w_in.astype(BF16), mu, seq=seq, tm=512, tn=n_proj // 2)
    return _wkv_core(proj.reshape(bsz, seq, n_proj), w2, a2, g2, w0, a0, k_k, k_a, r_k, gn_g, gn_b)


def _ffn(xf, w1, w2, g, b):
    hidden = _matmul(xf, w1.astype(BF16), tm=512, tn=1024, out_dtype=BF16, act="relu2")
    return _matmul_res_ln(hidden, w2.astype(BF16), xf, g, b, tm=256)


def kernel(x, mlstm_w_in, mlstm_b_i, mlstm_b_f, mlstm_conv_w, mlstm_conv_b, mlstm_norm_g, mlstm_w_out, rwkv_w_in, rwkv_mu, rwkv_w0, rwkv_w2, rwkv_a0, rwkv_a2, rwkv_g2, rwkv_k_k, rwkv_k_a, rwkv_r_k, rwkv_gn_g, rwkv_gn_b, rwkv_w_out, ln_mix_g, ln_mix_b, mlp_w1, mlp_w2, ln_ffn_g, ln_ffn_b):
    bsz, seq, d = x.shape
    m = bsz * seq
    xf = x.reshape(m, d)
    for layer in range(DEPTH):
        j = layer // 2
        if layer % 2 == 0:
            mix = _mlstm_mixer(xf.reshape(bsz, seq, d), mlstm_w_in[j], mlstm_b_i[j], mlstm_b_f[j],
                               mlstm_conv_w[j], mlstm_conv_b[j], mlstm_norm_g[j])
            w_out = mlstm_w_out[j]
        else:
            mix = _rwkv_mixer(xf.reshape(bsz, seq, d), rwkv_w_in[j], rwkv_mu[j], rwkv_w0[j], rwkv_w2[j],
                              rwkv_a0[j], rwkv_a2[j], rwkv_g2[j], rwkv_k_k[j], rwkv_k_a[j],
                              rwkv_r_k[j].reshape(-1), rwkv_gn_g[j], rwkv_gn_b[j])
            w_out = rwkv_w_out[j]
        xf = _matmul_res_ln(mix.reshape(m, -1), w_out.astype(BF16), xf, ln_mix_g[layer], ln_mix_b[layer], tm=512)
        xf = _ffn(xf, mlp_w1[layer], mlp_w2[layer], ln_ffn_g[layer], ln_ffn_b[layer])
    return xf.reshape(bsz, seq, d)
```

```python
import functools

import jax
import jax.numpy as jnp
from jax import lax
from jax.experimental import pallas as pl
from jax.experimental.pallas import tpu as pltpu

F32 = jnp.float32
BF16 = jnp.bfloat16
HIGHEST = lax.Precision.HIGHEST

DEPTH = 2
DN_ALPHA = (2.0 * DEPTH) ** 0.25
LN_EPS = 1e-5

M_HEADS = 4
M_DK = 128
M_DV = 256
M_GATE_CAP = 15.0
M_CHUNK = 256

R_N = 64
R_GN_EPS = 64e-5
R_CHUNK = 64
R_GROUP = 2
R_PAIRS = 8

VMEM_LIMIT = 48 * 1024 * 1024


def _cparams(sem):
    return pltpu.CompilerParams(dimension_semantics=sem, vmem_limit_bytes=VMEM_LIMIT)


def _dot(a, b, precision=None):
    return jnp.dot(a, b, preferred_element_type=F32, precision=precision)


def _dot_nt(a, b, precision=None):
    return lax.dot_general(a, b, (((1,), (1,)), ((), ())), preferred_element_type=F32, precision=precision)


def _dot_tn(a, b, precision=None):
    return lax.dot_general(a, b, (((0,), (0,)), ((), ())), preferred_element_type=F32, precision=precision)


def _mm_body(x_ref, w_ref, o_ref, *, act):
    acc = _dot(x_ref[...].astype(BF16), w_ref[...])
    if act == "relu2":
        acc = jnp.square(jnp.maximum(acc, 0.0))
    o_ref[...] = acc.astype(o_ref.dtype)


def _matmul(x, w, *, tm, tn, out_dtype, act=None):
    m, k = x.shape
    n = w.shape[1]
    tm = min(tm, m)
    assert m % tm == 0 and n % tn == 0
    return pl.pallas_call(
        functools.partial(_mm_body, act=act),
        grid=(m // tm, n // tn),
        in_specs=[pl.BlockSpec((tm, k), lambda i, j: (i, 0)),
                  pl.BlockSpec((k, tn), lambda i, j: (0, j))],
        out_specs=pl.BlockSpec((tm, tn), lambda i, j: (i, j)),
        out_shape=jax.ShapeDtypeStruct((m, n), out_dtype),
        compiler_params=_cparams(("parallel", "parallel")),
    )(x, w)


def _mm_lerp_body(x_ref, w_ref, mu_ref, o_ref, xb_ref, carry_ref, *, tiles_per_seq):
    i = pl.program_id(0)
    j = pl.program_id(1)

    @pl.when(j == 0)
    def _():
        xb_ref[...] = x_ref[...].astype(BF16)

    @pl.when(i % tiles_per_seq == 0)
    def _():
        carry_ref[j] = jnp.zeros(carry_ref.shape[1:], F32)

    acc = _dot(xb_ref[...], w_ref[...])
    tm = acc.shape[0]
    rolled = pltpu.roll(acc, 1, 0)
    first = lax.broadcasted_iota(jnp.int32, (8, acc.shape[1]), 0) == 0
    top = jnp.where(first, carry_ref[j][0:1, :], rolled[0:8])
    shifted = jnp.concatenate([top, rolled[8:]], axis=0)
    o_ref[...] = acc + mu_ref[...] * (shifted - acc)
    carry_ref[j] = jnp.broadcast_to(acc[tm - 1:tm, :], carry_ref.shape[1:])


def _matmul_shift_lerp(x, w, mu, *, seq, tm, tn):
    m, k = x.shape
    n = w.shape[1]
    tm = min(tm, seq)
    assert seq % tm == 0 and n % tn == 0
    return pl.pallas_call(
        functools.partial(_mm_lerp_body, tiles_per_seq=seq // tm),
        grid=(m // tm, n // tn),
        in_specs=[pl.BlockSpec((tm, k), lambda i, j: (i, 0)),
                  pl.BlockSpec((k, tn), lambda i, j: (0, j)),
                  pl.BlockSpec((1, tn), lambda i, j: (0, j))],
        out_specs=pl.BlockSpec((tm, tn), lambda i, j: (i, j)),
        out_shape=jax.ShapeDtypeStruct((m, n), F32),
        scratch_shapes=[pltpu.VMEM((tm, k), BF16),
                        pltpu.VMEM((n // tn, 8, tn), F32)],
        compiler_params=_cparams(("arbitrary", "arbitrary")),
        name="proj_token_shift",
    )(x, w, mu.reshape(1, n))


def _mm_conv_body(x_ref, w_ref, cw_ref, cb_ref, o_ref, xb_ref, carry_ref, *, tiles_per_seq, n_conv_tiles):
    i = pl.program_id(0)
    j = pl.program_id(1)

    @pl.when(j == 0)
    def _():
        xb_ref[...] = x_ref[...].astype(BF16)

    acc = _dot(xb_ref[...], w_ref[...])
    tm = acc.shape[0]

    @pl.when(j < n_conv_tiles)
    def _():
        @pl.when(i % tiles_per_seq == 0)
        def _():
            carry_ref[j] = jnp.zeros(carry_ref.shape[1:], F32)

        taps = cw_ref.shape[0]
        ext = jnp.concatenate([carry_ref[j], acc], axis=0)
        conv = cb_ref[...]
        for tap in range(taps):
            start = 8 - (taps - 1) + tap
            conv = conv + cw_ref[tap:tap + 1, :] * ext[start:start + tm]
        o_ref[...] = conv * jax.nn.sigmoid(conv)
        carry_ref[j] = acc[tm - 8:tm]

    @pl.when(j >= n_conv_tiles)
    def _():
        o_ref[...] = acc


def _matmul_conv_silu(x, w, conv_w, conv_b, *, seq, tm, tn):
    m, k = x.shape
    n = w.shape[1]
    taps, n_conv = conv_w.shape
    tm = min(tm, seq)
    assert seq % tm == 0 and n % tn == 0 and n_conv % tn == 0 and taps <= 8
    n_conv_tiles = n_conv // tn
    conv_tile = lambda i, j: (0, jnp.minimum(j, n_conv_tiles - 1))
    return pl.pallas_call(
        functools.partial(_mm_conv_body, tiles_per_seq=seq // tm, n_conv_tiles=n_conv_tiles),
        grid=(m // tm, n // tn),
        in_specs=[pl.BlockSpec((tm, k), lambda i, j: (i, 0)),
                  pl.BlockSpec((k, tn), lambda i, j: (0, j)),
                  pl.BlockSpec((taps, tn), conv_tile),
                  pl.BlockSpec((1, tn), conv_tile)],
        out_specs=pl.BlockSpec((tm, tn), lambda i, j: (i, j)),
        out_shape=jax.ShapeDtypeStruct((m, n), F32),
        scratch_shapes=[pltpu.VMEM((tm, k), BF16),
                        pltpu.VMEM((n_conv_tiles, 8, tn), F32)],
        compiler_params=_cparams(("arbitrary", "arbitrary")),
        name="proj_conv_silu",
    )(x, w, conv_w, conv_b.reshape(1, n_conv))


def _layer_norm(y, g, b):
    mu = jnp.mean(y, axis=-1, keepdims=True)
    yc = y - mu
    var = jnp.mean(jnp.square(yc), axis=-1, keepdims=True)
    return yc * lax.rsqrt(var + LN_EPS) * g + b


def _mix_ffn_body(mix_ref, wo_ref, res_ref, g1_ref, b1_ref, w1_ref, w2_ref, g2_ref, b2_ref, o_ref, *, ff_chunk):
    x1 = _layer_norm(DN_ALPHA * res_ref[...] + _dot(mix_ref[...], wo_ref[...]), g1_ref[...], b1_ref[...])
    x1b = x1.astype(BF16)
    acc = DN_ALPHA * x1
    for f in range(0, w1_ref.shape[1], ff_chunk):
        hid = jnp.square(jnp.maximum(_dot(x1b, w1_ref[:, f:f + ff_chunk]), 0.0))
        acc = acc + _dot(hid.astype(BF16), w2_ref[f:f + ff_chunk, :])
    o_ref[...] = _layer_norm(acc, g2_ref[...], b2_ref[...])


def _mix_ffn(mix, w_out, res, g1, b1, w1, w2, g2, b2, *, tm, ff_chunk):
    m, k = mix.shape
    d = w_out.shape[1]
    dff = w1.shape[1]
    tm = min(tm, m)
    assert m % tm == 0 and dff % ff_chunk == 0
    resident = lambda shape: pl.BlockSpec(shape, lambda i: (0, 0), pipeline_mode=pl.Buffered(1))
    rows = lambda width: pl.BlockSpec((tm, width), lambda i: (i, 0))
    vec = lambda p: p.reshape(1, d)
    return pl.pallas_call(
        functools.partial(_mix_ffn_body, ff_chunk=ff_chunk),
        grid=(m // tm,),
        in_specs=[rows(k), resident((k, d)), rows(d), resident((1, d)), resident((1, d)),
                  resident((d, dff)), resident((dff, d)), resident((1, d)), resident((1, d))],
        out_specs=rows(d),
        out_shape=jax.ShapeDtypeStruct((m, d), F32),
        compiler_params=_cparams(("parallel",)),
        name="outproj_ln_ffn_ln",
    )(mix, w_out, res, vec(g1), vec(b1), w1, w2, vec(g2), vec(b2))


def _mlstm_body(q_ref, k_ref, v_ref, o_ref, gate_ref, ng_ref, h_ref, c_ref, n_ref, m_ref):
    lc = q_ref.shape[1]

    @pl.when(pl.program_id(2) == 0)
    def _():
        c_ref[...] = jnp.zeros_like(c_ref)
        n_ref[...] = jnp.zeros_like(n_ref)
        m_ref[...] = jnp.zeros_like(m_ref)

    q = q_ref[0] * (M_DK ** -0.5)
    k = k_ref[0]
    v = v_ref[0]
    gates = gate_ref[0, 0]
    f_row = gates[0:1, :]
    i_row = gates[1:2, :]
    ri = lax.broadcasted_iota(jnp.int32, (lc, lc), 0)
    ci = lax.broadcasted_iota(jnp.int32, (lc, lc), 1)
    causal = ci <= ri
    tri_u = jnp.where(ri <= ci, 1.0, 0.0).astype(F32)
    bcum_row = _dot(jnp.broadcast_to(f_row, (8, lc)), tri_u, HIGHEST)[0:1, :]
    bcum_col = jnp.sum(jnp.where(causal, f_row, 0.0), axis=-1, keepdims=True)
    i_col = jnp.sum(jnp.where(ci == ri, i_row, 0.0), axis=-1, keepdims=True)
    m_prev = m_ref[0:1, 0:1]
    c_prev = c_ref[...]
    n_prev = n_ref[0:1, :]

    log_d = jnp.where(causal, bcum_col - bcum_row + i_row, -jnp.inf)
    log_inter = bcum_col + m_prev
    m_row = jnp.maximum(jnp.max(log_d, axis=-1, keepdims=True), log_inter)
    d_mat = jnp.exp(log_d - m_row)
    inter = jnp.exp(log_inter - m_row)
    qb = q.astype(BF16)
    kb = k.astype(BF16)
    s = _dot_nt(qb, kb) * d_mat
    num = _dot(s.astype(BF16), v.astype(BF16)) + inter * _dot(qb, c_prev.astype(BF16))
    den = jnp.sum(s, axis=-1, keepdims=True) + inter * jnp.sum(q * n_prev, axis=-1, keepdims=True)
    hc = num / jnp.maximum(jnp.abs(den), jnp.exp(-m_row))

    b_last = bcum_col[lc - 1:lc, :]
    log_w = b_last - bcum_col + i_col
    m_new = jnp.maximum(b_last + m_prev, jnp.max(log_w, axis=0, keepdims=True))
    decay = jnp.exp(b_last + m_prev - m_new)
    kw = k * jnp.exp(log_w - m_new)
    c_ref[...] = decay * c_prev + _dot_tn(kw.astype(BF16), v.astype(BF16))
    n_ref[...] = jnp.broadcast_to(decay * n_prev + jnp.sum(kw, axis=0, keepdims=True), n_ref.shape)
    m_ref[...] = jnp.broadcast_to(m_new, m_ref.shape)

    hn = hc * lax.rsqrt(jnp.mean(jnp.square(hc), axis=-1, keepdims=True) + 1e-6)
    h_ref[0] = (hn * ng_ref[...] * jax.nn.sigmoid(o_ref[0])).astype(h_ref.dtype)


def _mlstm_core(proj, gates, norm_g):
    bsz, seq, _ = proj.shape
    lc = gates.shape[-1]
    nc = seq // lc
    hdv = M_HEADS * M_DV
    return pl.pallas_call(
        _mlstm_body,
        grid=(bsz, M_HEADS, nc),
        in_specs=[pl.BlockSpec((1, lc, M_DK), lambda b, h, c: (b, c, h)),
                  pl.BlockSpec((1, lc, M_DK), lambda b, h, c: (b, c, M_HEADS + h)),
                  pl.BlockSpec((1, lc, M_DV), lambda b, h, c: (b, c, M_HEADS + h)),
                  pl.BlockSpec((1, lc, M_DV), lambda b, h, c: (b, c, 2 * M_HEADS + h)),
                  pl.BlockSpec((1, 1, 2, lc), lambda b, h, c: (b * M_HEADS + h, c, 0, 0)),
                  pl.BlockSpec((1, M_DV), lambda b, h, c: (0, h))],
        out_specs=pl.BlockSpec((1, lc, M_DV), lambda b, h, c: (b, c, h)),
        out_shape=jax.ShapeDtypeStruct((bsz, seq, hdv), BF16),
        scratch_shapes=[pltpu.VMEM((M_DK, M_DV), F32),
                        pltpu.VMEM((8, M_DK), F32),
                        pltpu.VMEM((8, 128), F32)],
        compiler_params=_cparams(("parallel", "parallel", "arbitrary")),
        name="mlstm_chunkwise",
    )(proj, proj, proj, proj, gates, norm_g.reshape(1, hdv))


def _softplus(z):
    return jnp.maximum(z, 0.0) + jnp.log1p(jnp.exp(-jnp.abs(z)))


def _mlstm_mixer(x, w_in, b_i, b_f, conv_w, conv_b, norm_g):
    bsz, seq, d = x.shape
    m = bsz * seq
    hdv = M_HEADS * M_DV
    n_main = 2 * M_HEADS * M_DK + 2 * hdv
    xf = x.reshape(m, d)
    proj = _matmul_conv_silu(xf, w_in[:, :n_main].astype(BF16), conv_w, conv_b, seq=seq, tm=512, tn=1024)
    w_gate = jnp.pad(w_in[:, n_main:], ((0, 0), (0, 128 - 2 * M_HEADS))).astype(BF16)
    gate_pre = _matmul(xf, w_gate, tm=2048, tn=128, out_dtype=F32)[:, :2 * M_HEADS]
    proj = proj.reshape(bsz, seq, n_main)
    gate_pre = gate_pre.reshape(bsz, seq, 2 * M_HEADS)
    log_i = M_GATE_CAP * jnp.tanh((gate_pre[..., :M_HEADS] + b_i) / M_GATE_CAP)
    log_f = -_softplus(-(M_GATE_CAP * jnp.tanh((gate_pre[..., M_HEADS:] + b_f) / M_GATE_CAP)))
    lc = min(M_CHUNK, seq)
    gates = jnp.stack([log_f, log_i], axis=0)
    gates = gates.reshape(2, bsz, seq // lc, lc, M_HEADS).transpose(1, 4, 2, 0, 3)
    gates = gates.reshape(bsz * M_HEADS, seq // lc, 2, lc)
    return _mlstm_core(proj, gates, norm_g)


def _split_bf16(x):
    hi = x.astype(BF16)
    return hi, (x - hi.astype(F32)).astype(BF16)


def _dot3(a, b):
    ah, al = _split_bf16(a)
    bh, bl = _split_bf16(b)
    return _dot(jnp.concatenate([ah, ah, al], axis=1), jnp.concatenate([bh, bl, bh], axis=0))


def _wkv_body(r_ref, k_ref, v_ref, tail_ref, w2_ref, a2_ref, g2_ref, w0_ref, a0_ref, kk_ref, ka_ref, rk_ref,
              gng_ref, gnb_ref, o_ref, s_ref, *, n_tanh):
    lc = r_ref.shape[1]
    width = r_ref.shape[2]
    gw = R_GROUP * R_N
    rows = R_GROUP * lc
    assert rows == gw

    @pl.when(pl.program_id(2) == 0)
    def _():
        s_ref[...] = jnp.zeros_like(s_ref)

    lane = lax.broadcasted_iota(jnp.int32, (1, width), 1)
    head_of_lane = lane // R_N

    def seg_sum(x):
        out = jnp.zeros_like(x)
        for h in range(width // R_N):
            msk = head_of_lane == h
            out = jnp.where(msk, jnp.sum(jnp.where(msk, x, 0.0), axis=-1, keepdims=True), out)
        return out

    r = r_ref[0]
    k = k_ref[0]
    v = v_ref[0]
    n_wa = w2_ref.shape[0]
    tail = tail_ref[0]
    t_wa = tail[:, :n_wa]
    is_tanh = lax.broadcasted_iota(jnp.int32, (1, n_wa), 1) < n_tanh
    f_wa = jnp.where(is_tanh, jnp.tanh(t_wa), t_wa).astype(BF16)
    gate = _dot(jax.nn.sigmoid(tail[:, n_wa:]).astype(BF16), g2_ref[...])
    log_w = -_softplus(-(w0_ref[...] + _dot(f_wa, w2_ref[...]))) - 0.5
    ld = -jnp.exp(log_w)
    a_lr = jax.nn.sigmoid(a0_ref[...] + _dot(f_wa, a2_ref[...]))
    kk = k * kk_ref[...]
    kk = kk / jnp.maximum(jnp.sqrt(seg_sum(jnp.square(kk))), 1e-12)
    k = k * (1.0 + (a_lr - 1.0) * ka_ref[...])
    b_ = kk * a_lr

    ti = lax.broadcasted_iota(jnp.int32, (lc, lc), 0)
    tj = lax.broadcasted_iota(jnp.int32, (lc, lc), 1)
    cum = _dot(jnp.where(tj <= ti, 1.0, 0.0).astype(F32), ld, HIGHEST)
    c_last = cum[lc - 1:lc, :]
    e_pos = jnp.exp(cum)
    e_neg = jnp.exp(-cum)
    e_end = jnp.exp(c_last - cum)
    w_end = jnp.exp(c_last)
    a_til = -kk * jnp.exp(cum - ld)
    r_til = r * e_pos
    b_til = b_ * e_neg
    k_til = k * e_neg
    b_hat = b_ * e_end
    k_hat = k * e_end

    lane_head = lax.broadcasted_iota(jnp.int32, (lc, gw), 1) // R_N
    head_sel = [lane_head == h for h in range(R_GROUP)]

    def stack(x):
        zero = jnp.zeros_like(x)
        return jnp.concatenate([jnp.where(sel, x, zero) for sel in head_sel], axis=0)

    trow = lax.broadcasted_iota(jnp.int32, (lc, gw), 0)
    tcol = lax.broadcasted_iota(jnp.int32, (lc, gw), 1) % lc
    strict = tcol < trow
    causal = tcol <= trow
    eye = jnp.where(tcol == trow, 1.0, 0.0).astype(F32)
    level_masks = []
    s = 1
    while s < lc:
        lo, hi = s.bit_length() - 1, s.bit_length()
        level_masks.append(jnp.logical_and((trow >> hi) == (tcol >> hi), (trow >> lo) != (tcol >> lo)))
        s *= 2
    brow = lax.broadcasted_iota(jnp.int32, (gw, gw), 0) // R_N
    bcol = lax.broadcasted_iota(jnp.int32, (gw, gw), 1) // R_N
    same_head = brow == bcol

    pairs = range(width // gw)
    sls = [slice(p * gw, (p + 1) * gw) for p in pairs]
    ar_b = [jnp.concatenate([a_til[:, sl], r_til[:, sl]], axis=0).astype(BF16) for sl in sls]
    bk_s = [jnp.concatenate([stack(b_til[:, sl].astype(BF16)), stack(k_til[:, sl].astype(BF16))], axis=0)
            for sl in sls]
    v_b = [v[:, sl].astype(BF16) for sl in sls]
    v_s = [stack(v_b[p]) for p in pairs]
    bk_hat = [jnp.concatenate([b_hat[:, sl], k_hat[:, sl]], axis=0).astype(BF16) for sl in sls]

    pm = [_dot_nt(ar_b[p], bk_s[p]) for p in pairs]
    n_ab = [jnp.where(strict, pm[p][:lc, :rows], 0.0) for p in pairs]
    m_xk = [jnp.concatenate([jnp.where(strict, pm[p][:lc, rows:], 0.0),
                             jnp.where(causal, pm[p][lc:, rows:], 0.0)], axis=0).astype(BF16) for p in pairs]
    m_rb = [jnp.where(causal, pm[p][lc:, :rows], 0.0).astype(BF16) for p in pairs]

    t_inv = [eye + jnp.where(level_masks[0], n_ab[p], 0.0) for p in pairs]
    for msk in level_masks[1:]:
        t_hl = [_split_bf16(t_inv[p]) for p in pairs]
        c_s = [stack(jnp.where(msk, n_ab[p], 0.0).astype(BF16)) for p in pairs]
        half = [_dot(jnp.concatenate(t_hl[p], axis=1), jnp.concatenate([c_s[p], c_s[p]], axis=0)) for p in pairs]
        h_hl = [_split_bf16(half[p]) for p in pairs]
        t_inv = [t_inv[p] + _dot(jnp.concatenate([h_hl[p][0], h_hl[p][0], h_hl[p][1]], axis=1),
                                 jnp.concatenate([stack(t_hl[p][0]), stack(t_hl[p][1]), stack(t_hl[p][0])], axis=0))
                 for p in pairs]

    s0 = [s_ref[p] for p in pairs]
    zy = [_dot_nt(ar_b[p], s0[p].astype(BF16)) + _dot(m_xk[p], v_s[p]) for p in pairs]
    ub = []
    for p in pairs:
        z_s = stack(zy[p][:lc].astype(BF16))
        ub.append(_dot(jnp.concatenate(_split_bf16(t_inv[p]), axis=1), jnp.concatenate([z_s, z_s], axis=0)).astype(BF16))
    ys = [zy[p][lc:] + _dot(m_rb[p], stack(ub[p])) for p in pairs]
    for p in pairs:
        upd = _dot_tn(jnp.concatenate([ub[p], v_b[p]], axis=0), bk_hat[p])
        s_ref[p] = s0[p] * w_end[:, sls[p]] + jnp.where(same_head, upd, 0.0)
    y = jnp.concatenate(ys, axis=1) if len(ys) > 1 else ys[0]


    mu = seg_sum(y) * (1.0 / R_N)
    yc = y - mu
    var = seg_sum(jnp.square(yc)) * (1.0 / R_N)
    yn = yc * lax.rsqrt(var + R_GN_EPS) * gng_ref[...] + gnb_ref[...]
    bonus = seg_sum(r * k * rk_ref[...]) * v
    o_ref[0] = ((yn + bonus) * gate).astype(o_ref.dtype)


def _wkv_core(proj, w2, a2, g2, w0, a0, k_k, k_a, r_k, gn_g, gn_b):
    bsz, seq, n_proj = proj.shape
    d = w0.shape[-1]
    lw, la, lg = w2.shape[0], a2.shape[0], g2.shape[0]
    n_tail = lw + la + lg
    assert n_proj == 3 * d + n_tail and (3 * d) % n_tail == 0
    gw = R_GROUP * R_N
    width = R_PAIRS * gw
    nb = d // width
    lc = min(R_CHUNK, seq)
    nc = seq // lc
    w2p = jnp.concatenate([w2, jnp.zeros_like(a2)], axis=0).astype(BF16)
    a2p = jnp.concatenate([jnp.zeros_like(w2), a2], axis=0).astype(BF16)
    tile = lambda off: pl.BlockSpec((1, lc, width), lambda b, g, c: (b, c, off * nb + g))
    tail = pl.BlockSpec((1, lc, n_tail), lambda b, g, c: (b, c, 3 * d // n_tail))
    wa = pl.BlockSpec((lw + la, width), lambda b, g, c: (0, g))
    wg = pl.BlockSpec((lg, width), lambda b, g, c: (0, g))
    par = pl.BlockSpec((1, width), lambda b, g, c: (0, g))
    row = lambda p: p.reshape(1, d)
    return pl.pallas_call(
        functools.partial(_wkv_body, n_tanh=lw),
        grid=(bsz, nb, nc),
        in_specs=[tile(0), tile(1), tile(2), tail, wa, wa, wg] + [par] * 7,
        out_specs=pl.BlockSpec((1, lc, width), lambda b, g, c: (b, c, g)),
        out_shape=jax.ShapeDtypeStruct((bsz, seq, d), BF16),
        scratch_shapes=[pltpu.VMEM((R_PAIRS, gw, gw), F32)],
        compiler_params=_cparams(("parallel", "parallel", "arbitrary")),
        name="wkv7_chunked",
    )(proj, proj, proj, proj, w2p, a2p, g2.astype(BF16),
      row(w0), row(a0), row(k_k), row(k_a), row(r_k), row(gn_g), row(gn_b))


def _rwkv_mixer(x, w_in, mu, w0, w2, a0, a2, g2, k_k, k_a, r_k, gn_g, gn_b):
    bsz, seq, d = x.shape
    n_proj = w_in.shape[1]
    proj = _matmul_shift_lerp(x.reshape(bsz * seq, d), w_in.astype(BF16), mu, seq=seq, tm=512, tn=n_proj // 2)
    return _wkv_core(proj.reshape(bsz, seq, n_proj), w2, a2, g2, w0, a0, k_k, k_a, r_k, gn_g, gn_b)


def kernel(x, mlstm_w_in, mlstm_b_i, mlstm_b_f, mlstm_conv_w, mlstm_conv_b, mlstm_norm_g, mlstm_w_out, rwkv_w_in, rwkv_mu, rwkv_w0, rwkv_w2, rwkv_a0, rwkv_a2, rwkv_g2, rwkv_k_k, rwkv_k_a, rwkv_r_k, rwkv_gn_g, rwkv_gn_b, rwkv_w_out, ln_mix_g, ln_mix_b, mlp_w1, mlp_w2, ln_ffn_g, ln_ffn_b):
    bsz, seq, d = x.shape
    m = bsz * seq
    xf = x.reshape(m, d)
    for layer in range(DEPTH):
        j = layer // 2
        if layer % 2 == 0:
            mix = _mlstm_mixer(xf.reshape(bsz, seq, d), mlstm_w_in[j], mlstm_b_i[j], mlstm_b_f[j],
                               mlstm_conv_w[j], mlstm_conv_b[j], mlstm_norm_g[j])
            w_out = mlstm_w_out[j]
        else:
            mix = _rwkv_mixer(xf.reshape(bsz, seq, d), rwkv_w_in[j], rwkv_mu[j], rwkv_w0[j], rwkv_w2[j],
                              rwkv_a0[j], rwkv_a2[j], rwkv_g2[j], rwkv_k_k[j], rwkv_k_a[j],
                              rwkv_r_k[j].reshape(-1), rwkv_gn_g[j], rwkv_gn_b[j])
            w_out = rwkv_w_out[j]
        xf = _mix_ffn(mix.reshape(m, -1), w_out.astype(BF16), xf, ln_mix_g[layer], ln_mix_b[layer],
                      mlp_w1[layer].astype(BF16), mlp_w2[layer].astype(BF16), ln_ffn_g[layer], ln_ffn_b[layer],
                      tm=512, ff_chunk=1024)
    return xf.reshape(bsz, seq, d)
```

```python
import functools
import math

import jax
import jax.numpy as jnp
from jax import lax
from jax.experimental import pallas as pl
from jax.experimental.pallas import tpu as pltpu

F32 = jnp.float32
BF16 = jnp.bfloat16
HIGHEST = lax.Precision.HIGHEST

DEPTH = 2
DN_ALPHA = (2.0 * DEPTH) ** 0.25
LN_EPS = 1e-5

M_HEADS = 4
M_DK = 128
M_DV = 256
M_GATE_CAP = 15.0
M_CHUNK = 256

R_N = 64
R_GN_EPS = 64e-5
R_DECAY_SCALE = math.exp(-0.5)
R_CHUNK = 64
R_GROUP = 2
R_PAIRS = 8

VMEM_LIMIT = 48 * 1024 * 1024


def _cparams(sem):
    return pltpu.CompilerParams(dimension_semantics=sem, vmem_limit_bytes=VMEM_LIMIT)


def _dot(a, b, precision=None):
    return jnp.dot(a, b, preferred_element_type=F32, precision=precision)


def _dot_nt(a, b, precision=None):
    return lax.dot_general(a, b, (((1,), (1,)), ((), ())), preferred_element_type=F32, precision=precision)


def _dot_tn(a, b, precision=None):
    return lax.dot_general(a, b, (((0,), (0,)), ((), ())), preferred_element_type=F32, precision=precision)


def _mm_body(x_ref, w_ref, o_ref, *, act):
    acc = _dot(x_ref[...].astype(BF16), w_ref[...])
    if act == "relu2":
        acc = jnp.square(jnp.maximum(acc, 0.0))
    o_ref[...] = acc.astype(o_ref.dtype)


def _matmul(x, w, *, tm, tn, out_dtype, act=None):
    m, k = x.shape
    n = w.shape[1]
    tm = min(tm, m)
    assert m % tm == 0 and n % tn == 0
    return pl.pallas_call(
        functools.partial(_mm_body, act=act),
        grid=(m // tm, n // tn),
        in_specs=[pl.BlockSpec((tm, k), lambda i, j: (i, 0)),
                  pl.BlockSpec((k, tn), lambda i, j: (0, j))],
        out_specs=pl.BlockSpec((tm, tn), lambda i, j: (i, j)),
        out_shape=jax.ShapeDtypeStruct((m, n), out_dtype),
        compiler_params=_cparams(("parallel", "parallel")),
    )(x, w)


def _mm_lerp_body(x_ref, w_ref, mu_ref, o_ref, xb_ref, carry_ref, *, tiles_per_seq):
    i = pl.program_id(0)
    j = pl.program_id(1)

    @pl.when(j == 0)
    def _():
        xb_ref[...] = x_ref[...].astype(BF16)

    @pl.when(i % tiles_per_seq == 0)
    def _():
        carry_ref[j] = jnp.zeros(carry_ref.shape[1:], F32)

    acc = _dot(xb_ref[...], w_ref[...])
    tm = acc.shape[0]
    rolled = pltpu.roll(acc, 1, 0)
    first = lax.broadcasted_iota(jnp.int32, (8, acc.shape[1]), 0) == 0
    top = jnp.where(first, carry_ref[j][0:1, :], rolled[0:8])
    shifted = jnp.concatenate([top, rolled[8:]], axis=0)
    o_ref[...] = acc + mu_ref[...] * (shifted - acc)
    carry_ref[j] = jnp.broadcast_to(acc[tm - 1:tm, :], carry_ref.shape[1:])


def _matmul_shift_lerp(x, w, mu, *, seq, tm, tn):
    m, k = x.shape
    n = w.shape[1]
    tm = min(tm, seq)
    assert seq % tm == 0 and n % tn == 0
    return pl.pallas_call(
        functools.partial(_mm_lerp_body, tiles_per_seq=seq // tm),
        grid=(m // tm, n // tn),
        in_specs=[pl.BlockSpec((tm, k), lambda i, j: (i, 0)),
                  pl.BlockSpec((k, tn), lambda i, j: (0, j)),
                  pl.BlockSpec((1, tn), lambda i, j: (0, j))],
        out_specs=pl.BlockSpec((tm, tn), lambda i, j: (i, j)),
        out_shape=jax.ShapeDtypeStruct((m, n), F32),
        scratch_shapes=[pltpu.VMEM((tm, k), BF16),
                        pltpu.VMEM((n // tn, 8, tn), F32)],
        compiler_params=_cparams(("arbitrary", "arbitrary")),
        name="proj_token_shift",
    )(x, w, mu.reshape(1, n))


def _mm_conv_body(x_ref, w_ref, cw_ref, cb_ref, o_ref, xb_ref, carry_ref, *, tiles_per_seq, n_conv_tiles):
    i = pl.program_id(0)
    j = pl.program_id(1)

    @pl.when(j == 0)
    def _():
        xb_ref[...] = x_ref[...].astype(BF16)

    acc = _dot(xb_ref[...], w_ref[...])
    tm = acc.shape[0]

    @pl.when(j < n_conv_tiles)
    def _():
        @pl.when(i % tiles_per_seq == 0)
        def _():
            carry_ref[j] = jnp.zeros(carry_ref.shape[1:], F32)

        taps = cw_ref.shape[0]
        ext = jnp.concatenate([carry_ref[j], acc], axis=0)
        conv = cb_ref[...]
        for tap in range(taps):
            start = 8 - (taps - 1) + tap
            conv = conv + cw_ref[tap:tap + 1, :] * ext[start:start + tm]
        o_ref[...] = conv * jax.nn.sigmoid(conv)
        carry_ref[j] = acc[tm - 8:tm]

    @pl.when(j >= n_conv_tiles)
    def _():
        o_ref[...] = acc


def _matmul_conv_silu(x, w, conv_w, conv_b, *, seq, tm, tn):
    m, k = x.shape
    n = w.shape[1]
    taps, n_conv = conv_w.shape
    tm = min(tm, seq)
    assert seq % tm == 0 and n % tn == 0 and n_conv % tn == 0 and taps <= 8
    n_conv_tiles = n_conv // tn
    conv_tile = lambda i, j: (0, jnp.minimum(j, n_conv_tiles - 1))
    return pl.pallas_call(
        functools.partial(_mm_conv_body, tiles_per_seq=seq // tm, n_conv_tiles=n_conv_tiles),
        grid=(m // tm, n // tn),
        in_specs=[pl.BlockSpec((tm, k), lambda i, j: (i, 0)),
                  pl.BlockSpec((k, tn), lambda i, j: (0, j)),
                  pl.BlockSpec((taps, tn), conv_tile),
                  pl.BlockSpec((1, tn), conv_tile)],
        out_specs=pl.BlockSpec((tm, tn), lambda i, j: (i, j)),
        out_shape=jax.ShapeDtypeStruct((m, n), F32),
        scratch_shapes=[pltpu.VMEM((tm, k), BF16),
                        pltpu.VMEM((n_conv_tiles, 8, tn), F32)],
        compiler_params=_cparams(("arbitrary", "arbitrary")),
        name="proj_conv_silu",
    )(x, w, conv_w, conv_b.reshape(1, n_conv))


def _layer_norm(y, g, b):
    mu = jnp.mean(y, axis=-1, keepdims=True)
    yc = y - mu
    var = jnp.mean(jnp.square(yc), axis=-1, keepdims=True)
    return yc * lax.rsqrt(var + LN_EPS) * g + b


def _mix_ffn_body(mix_ref, wo_ref, res_ref, g1_ref, b1_ref, w1_ref, w2_ref, g2_ref, b2_ref, o_ref, *, ff_chunk):
    x1 = _layer_norm(DN_ALPHA * res_ref[...] + _dot(mix_ref[...], wo_ref[...]), g1_ref[...], b1_ref[...])
    x1b = x1.astype(BF16)
    acc = DN_ALPHA * x1
    for f in range(0, w1_ref.shape[1], ff_chunk):
        hid = jnp.square(jnp.maximum(_dot(x1b, w1_ref[:, f:f + ff_chunk]), 0.0))
        acc = acc + _dot(hid.astype(BF16), w2_ref[f:f + ff_chunk, :])
    o_ref[...] = _layer_norm(acc, g2_ref[...], b2_ref[...])


def _mix_ffn(mix, w_out, res, g1, b1, w1, w2, g2, b2, *, tm, ff_chunk):
    m, k = mix.shape
    d = w_out.shape[1]
    dff = w1.shape[1]
    tm = min(tm, m)
    assert m % tm == 0 and dff % ff_chunk == 0
    resident = lambda shape: pl.BlockSpec(shape, lambda i: (0, 0), pipeline_mode=pl.Buffered(1))
    rows = lambda width: pl.BlockSpec((tm, width), lambda i: (i, 0))
    vec = lambda p: p.reshape(1, d)
    return pl.pallas_call(
        functools.partial(_mix_ffn_body, ff_chunk=ff_chunk),
        grid=(m // tm,),
        in_specs=[rows(k), resident((k, d)), rows(d), resident((1, d)), resident((1, d)),
                  resident((d, dff)), resident((dff, d)), resident((1, d)), resident((1, d))],
        out_specs=rows(d),
        out_shape=jax.ShapeDtypeStruct((m, d), F32),
        compiler_params=_cparams(("parallel",)),
        name="outproj_ln_ffn_ln",
    )(mix, w_out, res, vec(g1), vec(b1), w1, w2, vec(g2), vec(b2))


def _mlstm_body(q_ref, k_ref, v_ref, o_ref, gate_ref, ng_ref, h_ref, c_ref, n_ref, m_ref):
    lc = q_ref.shape[1]
    heads = range(M_HEADS)

    @pl.when(pl.program_id(1) == 0)
    def _():
        c_ref[...] = jnp.zeros_like(c_ref)
        n_ref[...] = jnp.zeros_like(n_ref)
        m_ref[...] = jnp.zeros_like(m_ref)

    ri = lax.broadcasted_iota(jnp.int32, (lc, lc), 0)
    ci = lax.broadcasted_iota(jnp.int32, (lc, lc), 1)
    causal = ci <= ri
    diag = ci == ri
    f_rows = gate_ref[0, 0, 0]
    i_rows = gate_ref[0, 0, 1]
    bcum_rows = _dot(f_rows, jnp.where(ri <= ci, 1.0, 0.0).astype(F32), HIGHEST)
    q_all = q_ref[0] * (M_DK ** -0.5)
    k_all = k_ref[0]
    v_all = v_ref[0]
    q = [q_all[:, h * M_DK:(h + 1) * M_DK] for h in heads]
    k = [k_all[:, h * M_DK:(h + 1) * M_DK] for h in heads]
    vb = [v_all[:, h * M_DV:(h + 1) * M_DV].astype(BF16) for h in heads]
    qb = [q[h].astype(BF16) for h in heads]
    kb = [k[h].astype(BF16) for h in heads]
    f_row = [f_rows[h:h + 1, :] for h in heads]
    i_row = [i_rows[h:h + 1, :] for h in heads]
    bcum_row = [bcum_rows[h:h + 1, :] for h in heads]
    bcum_col = [jnp.sum(jnp.where(causal, f_row[h], 0.0), axis=-1, keepdims=True) for h in heads]
    i_col = [jnp.sum(jnp.where(diag, i_row[h], 0.0), axis=-1, keepdims=True) for h in heads]
    m_prev = [m_ref[h, 0:1, 0:1] for h in heads]
    c_prev = [c_ref[h] for h in heads]
    n_prev = [n_ref[h, 0:1, :] for h in heads]

    log_d = [jnp.where(causal, bcum_col[h] - bcum_row[h] + i_row[h], -jnp.inf) for h in heads]
    log_inter = [bcum_col[h] + m_prev[h] for h in heads]
    m_row = [jnp.maximum(jnp.max(log_d[h], axis=-1, keepdims=True), log_inter[h]) for h in heads]
    inter = [jnp.exp(log_inter[h] - m_row[h]) for h in heads]
    s = [_dot_nt(qb[h], kb[h]) * jnp.exp(log_d[h] - m_row[h]) for h in heads]
    qc = [_dot(qb[h], c_prev[h].astype(BF16)) for h in heads]
    num = [_dot(s[h].astype(BF16), vb[h]) + inter[h] * qc[h] for h in heads]
    den = [jnp.sum(s[h], axis=-1, keepdims=True) + inter[h] * jnp.sum(q[h] * n_prev[h], axis=-1, keepdims=True)
           for h in heads]
    hc = [num[h] / jnp.maximum(jnp.abs(den[h]), jnp.exp(-m_row[h])) for h in heads]

    b_last = [bcum_col[h][lc - 1:lc, :] for h in heads]
    log_w = [b_last[h] - bcum_col[h] + i_col[h] for h in heads]
    m_new = [jnp.maximum(b_last[h] + m_prev[h], jnp.max(log_w[h], axis=0, keepdims=True)) for h in heads]
    decay = [jnp.exp(b_last[h] + m_prev[h] - m_new[h]) for h in heads]
    kw = [k[h] * jnp.exp(log_w[h] - m_new[h]) for h in heads]
    for h in heads:
        c_ref[h] = decay[h] * c_prev[h] + _dot_tn(kw[h].astype(BF16), vb[h])
        n_ref[h] = jnp.broadcast_to(decay[h] * n_prev[h] + jnp.sum(kw[h], axis=0, keepdims=True), n_ref.shape[1:])
        m_ref[h] = jnp.broadcast_to(m_new[h], m_ref.shape[1:])

    hn = jnp.concatenate([hc[h] * lax.rsqrt(jnp.mean(jnp.square(hc[h]), axis=-1, keepdims=True) + 1e-6)
                          for h in heads], axis=1)
    h_ref[0] = (hn * ng_ref[...] * jax.nn.sigmoid(o_ref[0])).astype(h_ref.dtype)


def _mlstm_core(proj, gates, norm_g):
    bsz, seq, _ = proj.shape
    lc = gates.shape[-1]
    nc = seq // lc
    hdk = M_HEADS * M_DK
    hdv = M_HEADS * M_DV
    assert 2 * hdk == hdv
    return pl.pallas_call(
        _mlstm_body,
        grid=(bsz, nc),
        in_specs=[pl.BlockSpec((1, lc, hdk), lambda b, c: (b, c, 0)),
                  pl.BlockSpec((1, lc, hdk), lambda b, c: (b, c, 1)),
                  pl.BlockSpec((1, lc, hdv), lambda b, c: (b, c, 1)),
                  pl.BlockSpec((1, lc, hdv), lambda b, c: (b, c, 2)),
                  pl.BlockSpec((1, 1, 2, 8, lc), lambda b, c: (b, c, 0, 0, 0)),
                  pl.BlockSpec((1, hdv), lambda b, c: (0, 0))],
        out_specs=pl.BlockSpec((1, lc, hdv), lambda b, c: (b, c, 0)),
        out_shape=jax.ShapeDtypeStruct((bsz, seq, hdv), BF16),
        scratch_shapes=[pltpu.VMEM((M_HEADS, M_DK, M_DV), F32),
                        pltpu.VMEM((M_HEADS, 8, M_DK), F32),
                        pltpu.VMEM((M_HEADS, 8, 128), F32)],
        compiler_params=_cparams(("parallel", "arbitrary")),
        name="mlstm_chunkwise",
    )(proj, proj, proj, proj, gates, norm_g.reshape(1, hdv))


def _softplus(z):
    return jnp.maximum(z, 0.0) + jnp.log1p(jnp.exp(-jnp.abs(z)))


def _mlstm_mixer(x, w_in, b_i, b_f, conv_w, conv_b, norm_g):
    bsz, seq, d = x.shape
    m = bsz * seq
    hdv = M_HEADS * M_DV
    n_main = 2 * M_HEADS * M_DK + 2 * hdv
    xf = x.reshape(m, d)
    proj = _matmul_conv_silu(xf, w_in[:, :n_main].astype(BF16), conv_w, conv_b, seq=seq, tm=1024, tn=1024)
    w_gate = jnp.pad(w_in[:, n_main:], ((0, 0), (0, 128 - 2 * M_HEADS))).astype(BF16)
    gate_pre = _matmul(xf, w_gate, tm=2048, tn=128, out_dtype=F32)[:, :2 * M_HEADS]
    proj = proj.reshape(bsz, seq, n_main)
    gate_pre = gate_pre.reshape(bsz, seq, 2 * M_HEADS)
    log_i = M_GATE_CAP * jnp.tanh((gate_pre[..., :M_HEADS] + b_i) / M_GATE_CAP)
    log_f = -_softplus(-(M_GATE_CAP * jnp.tanh((gate_pre[..., M_HEADS:] + b_f) / M_GATE_CAP)))
    lc = min(M_CHUNK, seq)
    gates = jnp.stack([log_f, log_i], axis=0)
    gates = gates.reshape(2, bsz, seq // lc, lc, M_HEADS).transpose(1, 2, 0, 4, 3)
    gates = jnp.pad(gates, ((0, 0), (0, 0), (0, 0), (0, 8 - M_HEADS), (0, 0)))
    return _mlstm_core(proj, gates, norm_g)


def _split_bf16(x):
    hi = x.astype(BF16)
    return hi, (x - hi.astype(F32)).astype(BF16)


def _wkv_body(r_ref, k_ref, v_ref, tail_ref, w2_ref, a2_ref, g2_ref, w0_ref, a0_ref, kk_ref, ka_ref, rk_ref,
              gng_ref, gnb_ref, o_ref, s_ref, *, n_tanh):
    lc = r_ref.shape[1]
    width = r_ref.shape[2]
    gw = R_GROUP * R_N
    rows = R_GROUP * lc
    assert rows == gw

    @pl.when(pl.program_id(2) == 0)
    def _():
        s_ref[...] = jnp.zeros_like(s_ref)

    pairs = range(width // gw)
    sls = [slice(p * gw, (p + 1) * gw) for p in pairs]
    brow = lax.broadcasted_iota(jnp.int32, (gw, gw), 0) // R_N
    bcol = lax.broadcasted_iota(jnp.int32, (gw, gw), 1) // R_N
    same_head = brow == bcol
    head_ones = jnp.where(same_head, 1.0, 0.0).astype(BF16)
    head_ones = jnp.concatenate([head_ones, head_ones], axis=0)

    def seg_sum(x):
        hi, lo = _split_bf16(x)
        return jnp.concatenate([_dot(jnp.concatenate([hi[:, sl], lo[:, sl]], axis=1), head_ones) for sl in sls],
                               axis=1)

    r = r_ref[0]
    k = k_ref[0]
    v = v_ref[0]
    n_wa = w2_ref.shape[0]
    tail = tail_ref[0]
    t_wa = tail[:, :n_wa]
    is_tanh = lax.broadcasted_iota(jnp.int32, (1, n_wa), 1) < n_tanh
    f_wa = jnp.where(is_tanh, jnp.tanh(t_wa), t_wa).astype(BF16)
    gate = _dot(jax.nn.sigmoid(tail[:, n_wa:]).astype(BF16), g2_ref[...])
    ld = (-R_DECAY_SCALE) * jax.nn.sigmoid(w0_ref[...] + _dot(f_wa, w2_ref[...]))
    a_lr = jax.nn.sigmoid(a0_ref[...] + _dot(f_wa, a2_ref[...]))
    kk = k * kk_ref[...]
    kk = kk / jnp.maximum(jnp.sqrt(seg_sum(jnp.square(kk))), 1e-12)
    k = k * (1.0 + (a_lr - 1.0) * ka_ref[...])
    b_ = kk * a_lr

    ti = lax.broadcasted_iota(jnp.int32, (lc, lc), 0)
    tj = lax.broadcasted_iota(jnp.int32, (lc, lc), 1)
    cum = _dot(jnp.where(tj <= ti, 1.0, 0.0).astype(F32), ld, HIGHEST)
    c_last = cum[lc - 1:lc, :]
    e_pos = jnp.exp(cum)
    e_neg = jnp.exp(-cum)
    w_end = jnp.exp(c_last)
    a_til = -kk * jnp.exp(cum - ld)
    r_til = r * e_pos
    b_til = b_ * e_neg
    k_til = k * e_neg
    b_hat = b_til * w_end
    k_hat = k_til * w_end

    lane_head = lax.broadcasted_iota(jnp.int32, (lc, gw), 1) // R_N
    head_sel = [lane_head == h for h in range(R_GROUP)]

    def stack(x):
        zero = jnp.zeros_like(x)
        return jnp.concatenate([jnp.where(sel, x, zero) for sel in head_sel], axis=0)

    trow = lax.broadcasted_iota(jnp.int32, (lc, gw), 0)
    tcol = lax.broadcasted_iota(jnp.int32, (lc, gw), 1) % lc
    strict = tcol < trow
    causal = tcol <= trow
    eye = jnp.where(tcol == trow, 1.0, 0.0).astype(F32)
    level_masks = []
    s = 1
    while s < lc:
        lo, hi = s.bit_length() - 1, s.bit_length()
        level_masks.append(jnp.logical_and((trow >> hi) == (tcol >> hi), (trow >> lo) != (tcol >> lo)))
        s *= 2

    ar_b = [jnp.concatenate([a_til[:, sl], r_til[:, sl]], axis=0).astype(BF16) for sl in sls]
    bk_s = [jnp.concatenate([stack(b_til[:, sl].astype(BF16)), stack(k_til[:, sl].astype(BF16))], axis=0)
            for sl in sls]
    v_b = [v[:, sl].astype(BF16) for sl in sls]
    v_s = [stack(v_b[p]) for p in pairs]
    bk_hat = [jnp.concatenate([b_hat[:, sl], k_hat[:, sl]], axis=0).astype(BF16) for sl in sls]

    pm = [_dot_nt(ar_b[p], bk_s[p]) for p in pairs]
    n_ab = [jnp.where(strict, pm[p][:lc, :rows], 0.0) for p in pairs]
    m_xk = [jnp.concatenate([jnp.where(strict, pm[p][:lc, rows:], 0.0),
                             jnp.where(causal, pm[p][lc:, rows:], 0.0)], axis=0).astype(BF16) for p in pairs]
    m_rb = [jnp.where(causal, pm[p][lc:, :rows], 0.0).astype(BF16) for p in pairs]

    t_inv = [eye + jnp.where(level_masks[0], n_ab[p], 0.0) for p in pairs]
    for msk in level_masks[1:]:
        t_hl = [_split_bf16(t_inv[p]) for p in pairs]
        c_s = [stack(jnp.where(msk, n_ab[p], 0.0).astype(BF16)) for p in pairs]
        half = [_dot(jnp.concatenate(t_hl[p], axis=1), jnp.concatenate([c_s[p], c_s[p]], axis=0)) for p in pairs]
        h_hl = [_split_bf16(half[p]) for p in pairs]
        t_inv = [t_inv[p] + _dot(jnp.concatenate([h_hl[p][0], h_hl[p][0], h_hl[p][1]], axis=1),
                                 jnp.concatenate([stack(t_hl[p][0]), stack(t_hl[p][1]), stack(t_hl[p][0])], axis=0))
                 for p in pairs]

    s0 = [s_ref[p] for p in pairs]
    zy = [_dot_nt(ar_b[p], s0[p].astype(BF16)) + _dot(m_xk[p], v_s[p]) for p in pairs]
    ub = []
    for p in pairs:
        z_s = stack(zy[p][:lc].astype(BF16))
        ub.append(_dot(jnp.concatenate(_split_bf16(t_inv[p]), axis=1), jnp.concatenate([z_s, z_s], axis=0)).astype(BF16))
    ys = [zy[p][lc:] + _dot(m_rb[p], stack(ub[p])) for p in pairs]
    for p in pairs:
        upd = _dot_tn(jnp.concatenate([ub[p], v_b[p]], axis=0), bk_hat[p])
        s_ref[p] = s0[p] * w_end[:, sls[p]] + jnp.where(same_head, upd, 0.0)
    y = jnp.concatenate(ys, axis=1) if len(ys) > 1 else ys[0]


    mu = seg_sum(y) * (1.0 / R_N)
    yc = y - mu
    var = seg_sum(jnp.square(yc)) * (1.0 / R_N)
    yn = yc * lax.rsqrt(var + R_GN_EPS) * gng_ref[...] + gnb_ref[...]
    bonus = seg_sum(r * k * rk_ref[...]) * v
    o_ref[0] = ((yn + bonus) * gate).astype(o_ref.dtype)


def _wkv_core(proj, w2, a2, g2, w0, a0, k_k, k_a, r_k, gn_g, gn_b):
    bsz, seq, n_proj = proj.shape
    d = w0.shape[-1]
    lw, la, lg = w2.shape[0], a2.shape[0], g2.shape[0]
    n_tail = lw + la + lg
    assert n_proj == 3 * d + n_tail and (3 * d) % n_tail == 0
    gw = R_GROUP * R_N
    width = R_PAIRS * gw
    nb = d // width
    lc = min(R_CHUNK, seq)
    nc = seq // lc
    w2p = jnp.concatenate([w2, jnp.zeros_like(a2)], axis=0).astype(BF16)
    a2p = jnp.concatenate([jnp.zeros_like(w2), a2], axis=0).astype(BF16)
    tile = lambda off: pl.BlockSpec((1, lc, width), lambda b, g, c: (b, c, off * nb + g))
    tail = pl.BlockSpec((1, lc, n_tail), lambda b, g, c: (b, c, 3 * d // n_tail))
    wa = pl.BlockSpec((lw + la, width), lambda b, g, c: (0, g))
    wg = pl.BlockSpec((lg, width), lambda b, g, c: (0, g))
    par = pl.BlockSpec((1, width), lambda b, g, c: (0, g))
    row = lambda p: p.reshape(1, d)
    return pl.pallas_call(
        functools.partial(_wkv_body, n_tanh=lw),
        grid=(bsz, nb, nc),
        in_specs=[tile(0), tile(1), tile(2), tail, wa, wa, wg] + [par] * 7,
        out_specs=pl.BlockSpec((1, lc, width), lambda b, g, c: (b, c, g)),
        out_shape=jax.ShapeDtypeStruct((bsz, seq, d), BF16),
        scratch_shapes=[pltpu.VMEM((R_PAIRS, gw, gw), F32)],
        compiler_params=_cparams(("parallel", "parallel", "arbitrary")),
        name="wkv7_chunked",
    )(proj, proj, proj, proj, w2p, a2p, g2.astype(BF16),
      row(w0), row(a0), row(k_k), row(k_a), row(r_k), row(gn_g), row(gn_b))


def _rwkv_mixer(x, w_in, mu, w0, w2, a0, a2, g2, k_k, k_a, r_k, gn_g, gn_b):
    bsz, seq, d = x.shape
    n_proj = w_in.shape[1]
    proj = _matmul_shift_lerp(x.reshape(bsz * seq, d), w_in.astype(BF16), mu, seq=seq, tm=1024, tn=n_proj // 2)
    return _wkv_core(proj.reshape(bsz, seq, n_proj), w2, a2, g2, w0, a0, k_k, k_a, r_k, gn_g, gn_b)


def kernel(x, mlstm_w_in, mlstm_b_i, mlstm_b_f, mlstm_conv_w, mlstm_conv_b, mlstm_norm_g, mlstm_w_out, rwkv_w_in, rwkv_mu, rwkv_w0, rwkv_w2, rwkv_a0, rwkv_a2, rwkv_g2, rwkv_k_k, rwkv_k_a, rwkv_r_k, rwkv_gn_g, rwkv_gn_b, rwkv_w_out, ln_mix_g, ln_mix_b, mlp_w1, mlp_w2, ln_ffn_g, ln_ffn_b):
    bsz, seq, d = x.shape
    m = bsz * seq
    xf = x.reshape(m, d)
    for layer in range(DEPTH):
        j = layer // 2
        if layer % 2 == 0:
            mix = _mlstm_mixer(xf.reshape(bsz, seq, d), mlstm_w_in[j], mlstm_b_i[j], mlstm_b_f[j],
                               mlstm_conv_w[j], mlstm_conv_b[j], mlstm_norm_g[j])
            w_out = mlstm_w_out[j]
        else:
            mix = _rwkv_mixer(xf.reshape(bsz, seq, d), rwkv_w_in[j], rwkv_mu[j], rwkv_w0[j], rwkv_w2[j],
                              rwkv_a0[j], rwkv_a2[j], rwkv_g2[j], rwkv_k_k[j], rwkv_k_a[j],
                              rwkv_r_k[j].reshape(-1), rwkv_gn_g[j], rwkv_gn_b[j])
            w_out = rwkv_w_out[j]
        xf = _mix_ffn(mix.reshape(m, -1), w_out.astype(BF16), xf, ln_mix_g[layer], ln_mix_b[layer],
                      mlp_w1[layer].astype(BF16), mlp_w2[layer].astype(BF16), ln_ffn_g[layer], ln_ffn_b[layer],
                      tm=512, ff_chunk=1024)
    return xf.reshape(bsz, seq, d)
```

```python
import functools
import math

import jax
import jax.numpy as jnp
from jax import lax
from jax.experimental import pallas as pl
from jax.experimental.pallas import tpu as pltpu

F32 = jnp.float32
BF16 = jnp.bfloat16
HIGHEST = lax.Precision.HIGHEST

DEPTH = 2
DN_ALPHA = (2.0 * DEPTH) ** 0.25
LN_EPS = 1e-5

M_HEADS = 4
M_DK = 128
M_DV = 256
M_GATE_CAP = 15.0
M_CHUNK = 256

R_N = 64
R_GN_EPS = 64e-5
R_DECAY_SCALE = math.exp(-0.5)
R_CHUNK = 64
R_GROUP = 2
R_PAIRS = 8

VMEM_LIMIT = 48 * 1024 * 1024


def _cparams(sem):
    return pltpu.CompilerParams(dimension_semantics=sem, vmem_limit_bytes=VMEM_LIMIT)


def _dot(a, b, precision=None):
    return jnp.dot(a, b, preferred_element_type=F32, precision=precision)


def _dot_nt(a, b, precision=None):
    return lax.dot_general(a, b, (((1,), (1,)), ((), ())), preferred_element_type=F32, precision=precision)


def _dot_tn(a, b, precision=None):
    return lax.dot_general(a, b, (((0,), (0,)), ((), ())), preferred_element_type=F32, precision=precision)


def _mm_body(x_ref, w_ref, o_ref, *, act):
    acc = _dot(x_ref[...].astype(BF16), w_ref[...])
    if act == "relu2":
        acc = jnp.square(jnp.maximum(acc, 0.0))
    o_ref[...] = acc.astype(o_ref.dtype)


def _matmul(x, w, *, tm, tn, out_dtype, act=None):
    m, k = x.shape
    n = w.shape[1]
    tm = min(tm, m)
    assert m % tm == 0 and n % tn == 0
    return pl.pallas_call(
        functools.partial(_mm_body, act=act),
        grid=(m // tm, n // tn),
        in_specs=[pl.BlockSpec((tm, k), lambda i, j: (i, 0)),
                  pl.BlockSpec((k, tn), lambda i, j: (0, j))],
        out_specs=pl.BlockSpec((tm, tn), lambda i, j: (i, j)),
        out_shape=jax.ShapeDtypeStruct((m, n), out_dtype),
        compiler_params=_cparams(("parallel", "parallel")),
    )(x, w)


def _mm_lerp_body(x_ref, w_ref, mu_ref, o_ref, xb_ref, carry_ref, *, tiles_per_seq):
    i = pl.program_id(0)
    j = pl.program_id(1)

    @pl.when(j == 0)
    def _():
        xb_ref[...] = x_ref[...].astype(BF16)

    @pl.when(i % tiles_per_seq == 0)
    def _():
        carry_ref[j] = jnp.zeros(carry_ref.shape[1:], F32)

    acc = _dot(xb_ref[...], w_ref[...])
    tm = acc.shape[0]
    rolled = pltpu.roll(acc, 1, 0)
    first = lax.broadcasted_iota(jnp.int32, (8, acc.shape[1]), 0) == 0
    top = jnp.where(first, carry_ref[j][0:1, :], rolled[0:8])
    shifted = jnp.concatenate([top, rolled[8:]], axis=0)
    o_ref[...] = (acc + mu_ref[...] * (shifted - acc)).astype(o_ref.dtype)
    carry_ref[j] = jnp.broadcast_to(acc[tm - 1:tm, :], carry_ref.shape[1:])


def _matmul_shift_lerp(x, w, mu, *, seq, tm, tn):
    m, k = x.shape
    n = w.shape[1]
    tm = min(tm, seq)
    assert seq % tm == 0 and n % tn == 0
    return pl.pallas_call(
        functools.partial(_mm_lerp_body, tiles_per_seq=seq // tm),
        grid=(m // tm, n // tn),
        in_specs=[pl.BlockSpec((tm, k), lambda i, j: (i, 0)),
                  pl.BlockSpec((k, tn), lambda i, j: (0, j)),
                  pl.BlockSpec((1, tn), lambda i, j: (0, j))],
        out_specs=pl.BlockSpec((tm, tn), lambda i, j: (i, j)),
        out_shape=jax.ShapeDtypeStruct((m, n), BF16),
        scratch_shapes=[pltpu.VMEM((tm, k), BF16),
                        pltpu.VMEM((n // tn, 8, tn), F32)],
        compiler_params=_cparams(("arbitrary", "arbitrary")),
        name="proj_token_shift",
    )(x, w, mu.reshape(1, n))


def _mm_conv_body(x_ref, w_ref, cw_ref, cb_ref, o_ref, xb_ref, carry_ref, *, tiles_per_seq, n_conv_tiles):
    i = pl.program_id(0)
    j = pl.program_id(1)

    @pl.when(j == 0)
    def _():
        xb_ref[...] = x_ref[...].astype(BF16)

    acc = _dot(xb_ref[...], w_ref[...])
    tm = acc.shape[0]

    @pl.when(j < n_conv_tiles)
    def _():
        @pl.when(i % tiles_per_seq == 0)
        def _():
            carry_ref[j] = jnp.zeros(carry_ref.shape[1:], F32)

        taps = cw_ref.shape[0]
        ext = jnp.concatenate([carry_ref[j], acc], axis=0)
        conv = cw_ref[0:1, :] * ext
        for tap in range(1, taps):
            conv = pltpu.roll(conv, 1, 0) + cw_ref[tap:tap + 1, :] * ext
        conv = conv[8:] + cb_ref[...]
        o_ref[...] = (conv * jax.nn.sigmoid(conv)).astype(o_ref.dtype)
        carry_ref[j] = acc[tm - 8:tm]

    @pl.when(j >= n_conv_tiles)
    def _():
        o_ref[...] = acc.astype(o_ref.dtype)


def _matmul_conv_silu(x, w, conv_w, conv_b, *, seq, tm, tn):
    m, k = x.shape
    n = w.shape[1]
    taps, n_conv = conv_w.shape
    tm = min(tm, seq)
    assert seq % tm == 0 and n % tn == 0 and n_conv % tn == 0 and taps <= 8
    n_conv_tiles = n_conv // tn
    conv_tile = lambda i, j: (0, jnp.minimum(j, n_conv_tiles - 1))
    return pl.pallas_call(
        functools.partial(_mm_conv_body, tiles_per_seq=seq // tm, n_conv_tiles=n_conv_tiles),
        grid=(m // tm, n // tn),
        in_specs=[pl.BlockSpec((tm, k), lambda i, j: (i, 0)),
                  pl.BlockSpec((k, tn), lambda i, j: (0, j)),
                  pl.BlockSpec((taps, tn), conv_tile),
                  pl.BlockSpec((1, tn), conv_tile)],
        out_specs=pl.BlockSpec((tm, tn), lambda i, j: (i, j)),
        out_shape=jax.ShapeDtypeStruct((m, n), BF16),
        scratch_shapes=[pltpu.VMEM((tm, k), BF16),
                        pltpu.VMEM((n_conv_tiles, 8, tn), F32)],
        compiler_params=_cparams(("arbitrary", "arbitrary")),
        name="proj_conv_silu",
    )(x, w, conv_w, conv_b.reshape(1, n_conv))


def _layer_norm(y, g, b):
    mu = jnp.mean(y, axis=-1, keepdims=True)
    yc = y - mu
    var = jnp.mean(jnp.square(yc), axis=-1, keepdims=True)
    return yc * lax.rsqrt(var + LN_EPS) * g + b


def _mix_ffn_body(mix_ref, wo_ref, res_ref, g1_ref, b1_ref, w1_ref, w2_ref, g2_ref, b2_ref, o_ref, *, ff_chunk):
    x1 = _layer_norm(DN_ALPHA * res_ref[...] + _dot(mix_ref[...], wo_ref[...]), g1_ref[...], b1_ref[...])
    x1b = x1.astype(BF16)
    acc = DN_ALPHA * x1
    for f in range(0, w1_ref.shape[1], ff_chunk):
        hid = jnp.square(jnp.maximum(_dot(x1b, w1_ref[:, f:f + ff_chunk]), 0.0))
        acc = acc + _dot(hid.astype(BF16), w2_ref[f:f + ff_chunk, :])
    o_ref[...] = _layer_norm(acc, g2_ref[...], b2_ref[...])


def _mix_ffn(mix, w_out, res, g1, b1, w1, w2, g2, b2, *, tm, ff_chunk):
    m, k = mix.shape
    d = w_out.shape[1]
    dff = w1.shape[1]
    tm = min(tm, m)
    assert m % tm == 0 and dff % ff_chunk == 0
    resident = lambda shape: pl.BlockSpec(shape, lambda i: (0, 0), pipeline_mode=pl.Buffered(1))
    rows = lambda width: pl.BlockSpec((tm, width), lambda i: (i, 0))
    vec = lambda p: p.reshape(1, d)
    return pl.pallas_call(
        functools.partial(_mix_ffn_body, ff_chunk=ff_chunk),
        grid=(m // tm,),
        in_specs=[rows(k), resident((k, d)), rows(d), resident((1, d)), resident((1, d)),
                  resident((d, dff)), resident((dff, d)), resident((1, d)), resident((1, d))],
        out_specs=rows(d),
        out_shape=jax.ShapeDtypeStruct((m, d), F32),
        compiler_params=_cparams(("parallel",)),
        name="outproj_ln_ffn_ln",
    )(mix, w_out, res, vec(g1), vec(b1), w1, w2, vec(g2), vec(b2))


def _mlstm_body(q_ref, k_ref, v_ref, o_ref, gate_ref, ng_ref, h_ref, c_ref, n_ref, m_ref):
    lc = q_ref.shape[1]
    heads = range(M_HEADS)

    @pl.when(pl.program_id(1) == 0)
    def _():
        c_ref[...] = jnp.zeros_like(c_ref)
        n_ref[...] = jnp.zeros_like(n_ref)
        m_ref[...] = jnp.zeros_like(m_ref)

    ri = lax.broadcasted_iota(jnp.int32, (lc, lc), 0)
    ci = lax.broadcasted_iota(jnp.int32, (lc, lc), 1)
    causal = ci <= ri
    diag = ci == ri
    f_rows = gate_ref[0, 0, 0]
    i_rows = gate_ref[0, 0, 1]
    bcum_rows = _dot(f_rows, jnp.where(ri <= ci, 1.0, 0.0).astype(F32), HIGHEST)
    scale = M_DK ** -0.5
    q_all = q_ref[0]
    k_all = k_ref[0]
    v_all = v_ref[0]
    qb = [q_all[:, h * M_DK:(h + 1) * M_DK] for h in heads]
    kb = [k_all[:, h * M_DK:(h + 1) * M_DK] for h in heads]
    vb = [v_all[:, h * M_DV:(h + 1) * M_DV] for h in heads]
    q = [qb[h].astype(F32) for h in heads]
    k = [kb[h].astype(F32) for h in heads]
    f_row = [f_rows[h:h + 1, :] for h in heads]
    i_row = [i_rows[h:h + 1, :] for h in heads]
    bcum_row = [bcum_rows[h:h + 1, :] for h in heads]
    bcum_col = [jnp.sum(jnp.where(causal, f_row[h], 0.0), axis=-1, keepdims=True) for h in heads]
    i_col = [jnp.sum(jnp.where(diag, i_row[h], 0.0), axis=-1, keepdims=True) for h in heads]
    m_prev = [m_ref[h, 0:1, 0:1] for h in heads]
    c_prev = [c_ref[h] for h in heads]
    n_prev = [n_ref[h, 0:1, :] for h in heads]

    log_d = [jnp.where(causal, bcum_col[h] - bcum_row[h] + i_row[h], -jnp.inf) for h in heads]
    log_inter = [bcum_col[h] + m_prev[h] for h in heads]
    m_row = [jnp.maximum(jnp.max(log_d[h], axis=-1, keepdims=True), log_inter[h]) for h in heads]
    inter = [jnp.exp(log_inter[h] - m_row[h]) for h in heads]
    s = [_dot_nt(qb[h], kb[h]) * (scale * jnp.exp(log_d[h] - m_row[h])) for h in heads]
    inter_s = [scale * inter[h] for h in heads]
    qc = [_dot(qb[h], c_prev[h].astype(BF16)) for h in heads]
    num = [_dot(s[h].astype(BF16), vb[h]) + inter_s[h] * qc[h] for h in heads]
    den = [jnp.sum(s[h], axis=-1, keepdims=True) + inter_s[h] * jnp.sum(q[h] * n_prev[h], axis=-1, keepdims=True)
           for h in heads]
    hc = [num[h] / jnp.maximum(jnp.abs(den[h]), jnp.exp(-m_row[h])) for h in heads]

    b_last = [bcum_col[h][lc - 1:lc, :] for h in heads]
    log_w = [b_last[h] - bcum_col[h] + i_col[h] for h in heads]
    m_new = [jnp.maximum(b_last[h] + m_prev[h], jnp.max(log_w[h], axis=0, keepdims=True)) for h in heads]
    decay = [jnp.exp(b_last[h] + m_prev[h] - m_new[h]) for h in heads]
    kw = [k[h] * jnp.exp(log_w[h] - m_new[h]) for h in heads]
    for h in heads:
        c_ref[h] = decay[h] * c_prev[h] + _dot_tn(kw[h].astype(BF16), vb[h])
        n_ref[h] = jnp.broadcast_to(decay[h] * n_prev[h] + jnp.sum(kw[h], axis=0, keepdims=True), n_ref.shape[1:])
        m_ref[h] = jnp.broadcast_to(m_new[h], m_ref.shape[1:])

    hn = jnp.concatenate([hc[h] * lax.rsqrt(jnp.mean(jnp.square(hc[h]), axis=-1, keepdims=True) + 1e-6)
                          for h in heads], axis=1)
    h_ref[0] = (hn * ng_ref[...] * jax.nn.sigmoid(o_ref[0].astype(F32))).astype(h_ref.dtype)


def _mlstm_core(proj, gates, norm_g):
    bsz, seq, _ = proj.shape
    lc = gates.shape[-1]
    nc = seq // lc
    hdk = M_HEADS * M_DK
    hdv = M_HEADS * M_DV
    assert 2 * hdk == hdv
    return pl.pallas_call(
        _mlstm_body,
        grid=(bsz, nc),
        in_specs=[pl.BlockSpec((1, lc, hdk), lambda b, c: (b, c, 0)),
                  pl.BlockSpec((1, lc, hdk), lambda b, c: (b, c, 1)),
                  pl.BlockSpec((1, lc, hdv), lambda b, c: (b, c, 1)),
                  pl.BlockSpec((1, lc, hdv), lambda b, c: (b, c, 2)),
                  pl.BlockSpec((1, 1, 2, 8, lc), lambda b, c: (b, c, 0, 0, 0)),
                  pl.BlockSpec((1, hdv), lambda b, c: (0, 0))],
        out_specs=pl.BlockSpec((1, lc, hdv), lambda b, c: (b, c, 0)),
        out_shape=jax.ShapeDtypeStruct((bsz, seq, hdv), BF16),
        scratch_shapes=[pltpu.VMEM((M_HEADS, M_DK, M_DV), F32),
                        pltpu.VMEM((M_HEADS, 8, M_DK), F32),
                        pltpu.VMEM((M_HEADS, 8, 128), F32)],
        compiler_params=_cparams(("parallel", "arbitrary")),
        name="mlstm_chunkwise",
    )(proj, proj, proj, proj, gates, norm_g.reshape(1, hdv))


def _softplus(z):
    return jnp.maximum(z, 0.0) + jnp.log1p(jnp.exp(-jnp.abs(z)))


def _mlstm_mixer(x, w_in, b_i, b_f, conv_w, conv_b, norm_g):
    bsz, seq, d = x.shape
    m = bsz * seq
    hdv = M_HEADS * M_DV
    n_main = 2 * M_HEADS * M_DK + 2 * hdv
    xf = x.reshape(m, d)
    proj = _matmul_conv_silu(xf, w_in[:, :n_main].astype(BF16), conv_w, conv_b, seq=seq, tm=1024, tn=1024)
    w_gate = jnp.pad(w_in[:, n_main:], ((0, 0), (0, 128 - 2 * M_HEADS))).astype(BF16)
    gate_pre = _matmul(xf, w_gate, tm=2048, tn=128, out_dtype=F32)[:, :2 * M_HEADS]
    proj = proj.reshape(bsz, seq, n_main)
    gate_pre = gate_pre.reshape(bsz, seq, 2 * M_HEADS)
    log_i = M_GATE_CAP * jnp.tanh((gate_pre[..., :M_HEADS] + b_i) / M_GATE_CAP)
    log_f = -_softplus(-(M_GATE_CAP * jnp.tanh((gate_pre[..., M_HEADS:] + b_f) / M_GATE_CAP)))
    lc = min(M_CHUNK, seq)
    gates = jnp.stack([log_f, log_i], axis=0)
    gates = gates.reshape(2, bsz, seq // lc, lc, M_HEADS).transpose(1, 2, 0, 4, 3)
    gates = jnp.pad(gates, ((0, 0), (0, 0), (0, 0), (0, 8 - M_HEADS), (0, 0)))
    return _mlstm_core(proj, gates, norm_g)


def _split_bf16(x):
    hi = x.astype(BF16)
    return hi, (x - hi.astype(F32)).astype(BF16)


def _wkv_body(r_ref, k_ref, v_ref, tail_ref, w2_ref, a2_ref, g2_ref, w0_ref, a0_ref, kk_ref, ka_ref, rk_ref,
              gng_ref, gnb_ref, o_ref, s_ref, *, n_tanh):
    lc = r_ref.shape[1]
    width = r_ref.shape[2]
    gw = R_GROUP * R_N
    rows = R_GROUP * lc
    assert rows == gw

    @pl.when(pl.program_id(2) == 0)
    def _():
        s_ref[...] = jnp.zeros_like(s_ref)

    pairs = range(width // gw)
    sls = [slice(p * gw, (p + 1) * gw) for p in pairs]
    brow = lax.broadcasted_iota(jnp.int32, (gw, gw), 0) // R_N
    bcol = lax.broadcasted_iota(jnp.int32, (gw, gw), 1) // R_N
    same_head = brow == bcol
    head_ones = jnp.where(same_head, 1.0, 0.0).astype(BF16)
    head_ones = jnp.concatenate([head_ones, head_ones], axis=0)

    def seg_sum(x):
        hi, lo = _split_bf16(x)
        return jnp.concatenate([_dot(jnp.concatenate([hi[:, sl], lo[:, sl]], axis=1), head_ones) for sl in sls],
                               axis=1)

    r = r_ref[0].astype(F32)
    k = k_ref[0].astype(F32)
    v = v_ref[0].astype(F32)
    n_wa = w2_ref.shape[0]
    tail = tail_ref[0].astype(F32)
    t_wa = tail[:, :n_wa]
    is_tanh = lax.broadcasted_iota(jnp.int32, (1, n_wa), 1) < n_tanh
    f_wa = jnp.where(is_tanh, jnp.tanh(t_wa), t_wa).astype(BF16)
    gate = _dot(jax.nn.sigmoid(tail[:, n_wa:]).astype(BF16), g2_ref[...])
    ld = (-R_DECAY_SCALE) * jax.nn.sigmoid(w0_ref[...] + _dot(f_wa, w2_ref[...]))
    a_lr = jax.nn.sigmoid(a0_ref[...] + _dot(f_wa, a2_ref[...]))
    kk = k * kk_ref[...]
    kk = kk / jnp.maximum(jnp.sqrt(seg_sum(jnp.square(kk))), 1e-12)
    k = k * (1.0 + (a_lr - 1.0) * ka_ref[...])
    b_ = kk * a_lr

    ti = lax.broadcasted_iota(jnp.int32, (lc, lc), 0)
    tj = lax.broadcasted_iota(jnp.int32, (lc, lc), 1)
    cum = _dot(jnp.where(tj <= ti, 1.0, 0.0).astype(F32), ld, HIGHEST)
    c_last = cum[lc - 1:lc, :]
    e_pos = jnp.exp(cum)
    e_neg = jnp.exp(-cum)
    w_end = jnp.exp(c_last)
    a_til = -kk * jnp.exp(cum - ld)
    r_til = r * e_pos
    b_til = b_ * e_neg
    k_til = k * e_neg
    b_hat = b_til * w_end
    k_hat = k_til * w_end

    lane_head = lax.broadcasted_iota(jnp.int32, (lc, gw), 1) // R_N
    head_sel = [lane_head == h for h in range(R_GROUP)]

    def stack(x):
        zero = jnp.zeros_like(x)
        return jnp.concatenate([jnp.where(sel, x, zero) for sel in head_sel], axis=0)

    trow = lax.broadcasted_iota(jnp.int32, (lc, gw), 0)
    tcol = lax.broadcasted_iota(jnp.int32, (lc, gw), 1) % lc
    strict = tcol < trow
    causal = tcol <= trow
    eye = jnp.where(tcol == trow, 1.0, 0.0).astype(F32)
    level_masks = []
    s = 1
    while s < lc:
        lo, hi = s.bit_length() - 1, s.bit_length()
        level_masks.append(jnp.logical_and((trow >> hi) == (tcol >> hi), (trow >> lo) != (tcol >> lo)))
        s *= 2

    ar_b = [jnp.concatenate([a_til[:, sl], r_til[:, sl]], axis=0).astype(BF16) for sl in sls]
    bk_s = [jnp.concatenate([stack(b_til[:, sl].astype(BF16)), stack(k_til[:, sl].astype(BF16))], axis=0)
            for sl in sls]
    v_b = [v[:, sl].astype(BF16) for sl in sls]
    v_s = [stack(v_b[p]) for p in pairs]
    bk_hat = [jnp.concatenate([b_hat[:, sl], k_hat[:, sl]], axis=0).astype(BF16) for sl in sls]

    pm = [_dot_nt(ar_b[p], bk_s[p]) for p in pairs]
    n_ab = [jnp.where(strict, pm[p][:lc, :rows], 0.0) for p in pairs]
    m_xk = [jnp.concatenate([jnp.where(strict, pm[p][:lc, rows:], 0.0),
                             jnp.where(causal, pm[p][lc:, rows:], 0.0)], axis=0).astype(BF16) for p in pairs]
    m_rb = [jnp.where(causal, pm[p][lc:, :rows], 0.0).astype(BF16) for p in pairs]

    t_inv = [eye + jnp.where(level_masks[0], n_ab[p], 0.0) for p in pairs]
    for msk in level_masks[1:]:
        t_hl = [_split_bf16(t_inv[p]) for p in pairs]
        c_s = [stack(jnp.where(msk, n_ab[p], 0.0).astype(BF16)) for p in pairs]
        half = [_dot(jnp.concatenate(t_hl[p], axis=1), jnp.concatenate([c_s[p], c_s[p]], axis=0)) for p in pairs]
        h_hl = [_split_bf16(half[p]) for p in pairs]
        t_inv = [t_inv[p] + _dot(jnp.concatenate([h_hl[p][0], h_hl[p][0], h_hl[p][1]], axis=1),
                                 jnp.concatenate([stack(t_hl[p][0]), stack(t_hl[p][1]), stack(t_hl[p][0])], axis=0))
                 for p in pairs]

    s0 = [s_ref[p] for p in pairs]
    zy = [_dot_nt(ar_b[p], s0[p].astype(BF16)) + _dot(m_xk[p], v_s[p]) for p in pairs]
    ub = []
    for p in pairs:
        z_s = stack(zy[p][:lc].astype(BF16))
        ub.append(_dot(jnp.concatenate(_split_bf16(t_inv[p]), axis=1), jnp.concatenate([z_s, z_s], axis=0)).astype(BF16))
    ys = [zy[p][lc:] + _dot(m_rb[p], stack(ub[p])) for p in pairs]
    for p in pairs:
        upd = _dot_tn(jnp.concatenate([ub[p], v_b[p]], axis=0), bk_hat[p])
        s_ref[p] = s0[p] * w_end[:, sls[p]] + jnp.where(same_head, upd, 0.0)
    y = jnp.concatenate(ys, axis=1) if len(ys) > 1 else ys[0]


    mu = seg_sum(y) * (1.0 / R_N)
    yc = y - mu
    var = seg_sum(jnp.square(yc)) * (1.0 / R_N)
    yn = yc * lax.rsqrt(var + R_GN_EPS) * gng_ref[...] + gnb_ref[...]
    bonus = seg_sum(r * k * rk_ref[...]) * v
    o_ref[0] = ((yn + bonus) * gate).astype(o_ref.dtype)


def _wkv_core(proj, w2, a2, g2, w0, a0, k_k, k_a, r_k, gn_g, gn_b):
    bsz, seq, n_proj = proj.shape
    d = w0.shape[-1]
    lw, la, lg = w2.shape[0], a2.shape[0], g2.shape[0]
    n_tail = lw + la + lg
    assert n_proj == 3 * d + n_tail and (3 * d) % n_tail == 0
    gw = R_GROUP * R_N
    width = R_PAIRS * gw
    nb = d // width
    lc = min(R_CHUNK, seq)
    nc = seq // lc
    w2p = jnp.concatenate([w2, jnp.zeros_like(a2)], axis=0).astype(BF16)
    a2p = jnp.concatenate([jnp.zeros_like(w2), a2], axis=0).astype(BF16)
    tile = lambda off: pl.BlockSpec((1, lc, width), lambda b, g, c: (b, c, off * nb + g))
    tail = pl.BlockSpec((1, lc, n_tail), lambda b, g, c: (b, c, 3 * d // n_tail))
    wa = pl.BlockSpec((lw + la, width), lambda b, g, c: (0, g))
    wg = pl.BlockSpec((lg, width), lambda b, g, c: (0, g))
    par = pl.BlockSpec((1, width), lambda b, g, c: (0, g))
    row = lambda p: p.reshape(1, d)
    return pl.pallas_call(
        functools.partial(_wkv_body, n_tanh=lw),
        grid=(bsz, nb, nc),
        in_specs=[tile(0), tile(1), tile(2), tail, wa, wa, wg] + [par] * 7,
        out_specs=pl.BlockSpec((1, lc, width), lambda b, g, c: (b, c, g)),
        out_shape=jax.ShapeDtypeStruct((bsz, seq, d), BF16),
        scratch_shapes=[pltpu.VMEM((R_PAIRS, gw, gw), F32)],
        compiler_params=_cparams(("parallel", "parallel", "arbitrary")),
        name="wkv7_chunked",
    )(proj, proj, proj, proj, w2p, a2p, g2.astype(BF16),
      row(w0), row(a0), row(k_k), row(k_a), row(r_k), row(gn_g), row(gn_b))


def _rwkv_mixer(x, w_in, mu, w0, w2, a0, a2, g2, k_k, k_a, r_k, gn_g, gn_b):
    bsz, seq, d = x.shape
    n_proj = w_in.shape[1]
    proj = _matmul_shift_lerp(x.reshape(bsz * seq, d), w_in.astype(BF16), mu, seq=seq, tm=1024, tn=n_proj // 2)
    return _wkv_core(proj.reshape(bsz, seq, n_proj), w2, a2, g2, w0, a0, k_k, k_a, r_k, gn_g, gn_b)


def kernel(x, mlstm_w_in, mlstm_b_i, mlstm_b_f, mlstm_conv_w, mlstm_conv_b, mlstm_norm_g, mlstm_w_out, rwkv_w_in, rwkv_mu, rwkv_w0, rwkv_w2, rwkv_a0, rwkv_a2, rwkv_g2, rwkv_k_k, rwkv_k_a, rwkv_r_k, rwkv_gn_g, rwkv_gn_b, rwkv_w_out, ln_mix_g, ln_mix_b, mlp_w1, mlp_w2, ln_ffn_g, ln_ffn_b):
    bsz, seq, d = x.shape
    m = bsz * seq
    xf = x.reshape(m, d)
    for layer in range(DEPTH):
        j = layer // 2
        if layer % 2 == 0:
            mix = _mlstm_mixer(xf.reshape(bsz, seq, d), mlstm_w_in[j], mlstm_b_i[j], mlstm_b_f[j],
                               mlstm_conv_w[j], mlstm_conv_b[j], mlstm_norm_g[j])
            w_out = mlstm_w_out[j]
        else:
            mix = _rwkv_mixer(xf.reshape(bsz, seq, d), rwkv_w_in[j], rwkv_mu[j], rwkv_w0[j], rwkv_w2[j],
                              rwkv_a0[j], rwkv_a2[j], rwkv_g2[j], rwkv_k_k[j], rwkv_k_a[j],
                              rwkv_r_k[j].reshape(-1), rwkv_gn_g[j], rwkv_gn_b[j])
            w_out = rwkv_w_out[j]
        xf = _mix_ffn(mix.reshape(m, -1), w_out.astype(BF16), xf, ln_mix_g[layer], ln_mix_b[layer],
                      mlp_w1[layer].astype(BF16), mlp_w2[layer].astype(BF16), ln_ffn_g[layer], ln_ffn_b[layer],
                      tm=512, ff_chunk=1024)
    return xf.reshape(bsz, seq, d)
```

```python
import functools
import math

import jax
import jax.numpy as jnp
from jax import lax
from jax.experimental import pallas as pl
from jax.experimental.pallas import tpu as pltpu

F32 = jnp.float32
BF16 = jnp.bfloat16
HIGHEST = lax.Precision.HIGHEST

DEPTH = 2
DN_ALPHA = (2.0 * DEPTH) ** 0.25
LN_EPS = 1e-5

M_HEADS = 4
M_DK = 128
M_DV = 256
M_GATE_CAP = 15.0
M_CHUNK = 256

R_N = 64
R_GN_EPS = 64e-5
R_DECAY_SCALE = math.exp(-0.5)
R_CHUNK = 64
R_GROUP = 2
R_PAIRS = 8
R_CHUNKS_PER_STEP = 2

VMEM_LIMIT = 48 * 1024 * 1024


def _cparams(sem):
    return pltpu.CompilerParams(dimension_semantics=sem, vmem_limit_bytes=VMEM_LIMIT)


def _dot(a, b, precision=None):
    return jnp.dot(a, b, preferred_element_type=F32, precision=precision)


def _dot_nt(a, b, precision=None):
    return lax.dot_general(a, b, (((1,), (1,)), ((), ())), preferred_element_type=F32, precision=precision)


def _dot_tn(a, b, precision=None):
    return lax.dot_general(a, b, (((0,), (0,)), ((), ())), preferred_element_type=F32, precision=precision)


def _mm_body(x_ref, w_ref, o_ref, *, act):
    acc = _dot(x_ref[...].astype(BF16), w_ref[...])
    if act == "relu2":
        acc = jnp.square(jnp.maximum(acc, 0.0))
    o_ref[...] = acc.astype(o_ref.dtype)


def _matmul(x, w, *, tm, tn, out_dtype, act=None):
    m, k = x.shape
    n = w.shape[1]
    tm = min(tm, m)
    assert m % tm == 0 and n % tn == 0
    return pl.pallas_call(
        functools.partial(_mm_body, act=act),
        grid=(m // tm, n // tn),
        in_specs=[pl.BlockSpec((tm, k), lambda i, j: (i, 0)),
                  pl.BlockSpec((k, tn), lambda i, j: (0, j))],
        out_specs=pl.BlockSpec((tm, tn), lambda i, j: (i, j)),
        out_shape=jax.ShapeDtypeStruct((m, n), out_dtype),
        compiler_params=_cparams(("parallel", "parallel")),
    )(x, w)


def _proj_lerp_body(x_ref, w_ref, mu_ref, o_ref, carry_ref, *, tiles_per_seq, col_chunk):
    @pl.when(pl.program_id(0) % tiles_per_seq == 0)
    def _():
        carry_ref[...] = jnp.zeros_like(carry_ref)

    xb = x_ref[...].astype(BF16)
    tm = xb.shape[0]
    first = lax.broadcasted_iota(jnp.int32, (8, col_chunk), 0) == 0
    for c0 in range(0, w_ref.shape[1], col_chunk):
        cols = slice(c0, c0 + col_chunk)
        acc = _dot(xb, w_ref[:, cols])
        rolled = pltpu.roll(acc, 1, 0)
        top = jnp.where(first, carry_ref[0:1, cols], rolled[0:8])
        shifted = jnp.concatenate([top, rolled[8:]], axis=0)
        o_ref[:, cols] = (acc + mu_ref[:, cols] * (shifted - acc)).astype(o_ref.dtype)
        carry_ref[:, cols] = jnp.broadcast_to(acc[tm - 1:tm, :], (8, col_chunk))


def _proj_shift_lerp(x, w, mu, *, seq, tm, col_chunk):
    m, k = x.shape
    n = w.shape[1]
    tm = min(tm, seq)
    assert seq % tm == 0 and n % col_chunk == 0
    return pl.pallas_call(
        functools.partial(_proj_lerp_body, tiles_per_seq=seq // tm, col_chunk=col_chunk),
        grid=(m // tm,),
        in_specs=[pl.BlockSpec((tm, k), lambda i: (i, 0)),
                  pl.BlockSpec((k, n), lambda i: (0, 0), pipeline_mode=pl.Buffered(1)),
                  pl.BlockSpec((1, n), lambda i: (0, 0), pipeline_mode=pl.Buffered(1))],
        out_specs=pl.BlockSpec((tm, n), lambda i: (i, 0)),
        out_shape=jax.ShapeDtypeStruct((m, n), F32),
        scratch_shapes=[pltpu.VMEM((8, n), F32)],
        compiler_params=_cparams(("arbitrary",)),
        name="proj_token_shift",
    )(x, w, mu.reshape(1, n))


def _proj_conv_body(x_ref, w_ref, cw_ref, cb_ref, o_ref, carry_ref, *, tiles_per_seq, col_chunk):
    @pl.when(pl.program_id(0) % tiles_per_seq == 0)
    def _():
        carry_ref[...] = jnp.zeros_like(carry_ref)

    xb = x_ref[...].astype(BF16)
    tm = xb.shape[0]
    taps, n_conv = cw_ref.shape
    for c0 in range(0, w_ref.shape[1], col_chunk):
        cols = slice(c0, c0 + col_chunk)
        acc = _dot(xb, w_ref[:, cols])
        if c0 < n_conv:
            ext = jnp.concatenate([carry_ref[:, cols], acc], axis=0)
            conv = cw_ref[0:1, cols] * ext
            for tap in range(1, taps):
                conv = pltpu.roll(conv, 1, 0) + cw_ref[tap:tap + 1, cols] * ext
            conv = conv[8:] + cb_ref[:, cols]
            o_ref[:, cols] = (conv * jax.nn.sigmoid(conv)).astype(o_ref.dtype)
            carry_ref[:, cols] = acc[tm - 8:tm]
        else:
            o_ref[:, cols] = acc.astype(o_ref.dtype)


def _proj_conv_silu(x, w, conv_w, conv_b, *, seq, tm, col_chunk):
    m, k = x.shape
    n = w.shape[1]
    taps, n_conv = conv_w.shape
    tm = min(tm, seq)
    assert seq % tm == 0 and n % col_chunk == 0 and n_conv % col_chunk == 0 and taps <= 8
    resident = lambda shape: pl.BlockSpec(shape, lambda i: (0, 0), pipeline_mode=pl.Buffered(1))
    return pl.pallas_call(
        functools.partial(_proj_conv_body, tiles_per_seq=seq // tm, col_chunk=col_chunk),
        grid=(m // tm,),
        in_specs=[pl.BlockSpec((tm, k), lambda i: (i, 0)), resident((k, n)),
                  resident((taps, n_conv)), resident((1, n_conv))],
        out_specs=pl.BlockSpec((tm, n), lambda i: (i, 0)),
        out_shape=jax.ShapeDtypeStruct((m, n), BF16),
        scratch_shapes=[pltpu.VMEM((8, n_conv), F32)],
        compiler_params=_cparams(("arbitrary",)),
        name="proj_conv_silu",
    )(x, w, conv_w, conv_b.reshape(1, n_conv))


def _layer_norm(y, g, b):
    mu = jnp.mean(y, axis=-1, keepdims=True)
    yc = y - mu
    var = jnp.mean(jnp.square(yc), axis=-1, keepdims=True)
    return yc * lax.rsqrt(var + LN_EPS) * g + b


def _mix_ffn_body(mix_ref, wo_ref, res_ref, g1_ref, b1_ref, w1_ref, w2_ref, g2_ref, b2_ref, o_ref, *, ff_chunk):
    x1 = _layer_norm(DN_ALPHA * res_ref[...] + _dot(mix_ref[...], wo_ref[...]), g1_ref[...], b1_ref[...])
    x1b = x1.astype(BF16)
    acc = DN_ALPHA * x1
    for f in range(0, w1_ref.shape[1], ff_chunk):
        hid = jnp.square(jnp.maximum(_dot(x1b, w1_ref[:, f:f + ff_chunk]), 0.0))
        acc = acc + _dot(hid.astype(BF16), w2_ref[f:f + ff_chunk, :])
    o_ref[...] = _layer_norm(acc, g2_ref[...], b2_ref[...])


def _mix_ffn(mix, w_out, res, g1, b1, w1, w2, g2, b2, *, tm, ff_chunk):
    m, k = mix.shape
    d = w_out.shape[1]
    dff = w1.shape[1]
    tm = min(tm, m)
    assert m % tm == 0 and dff % ff_chunk == 0
    resident = lambda shape: pl.BlockSpec(shape, lambda i: (0, 0), pipeline_mode=pl.Buffered(1))
    rows = lambda width: pl.BlockSpec((tm, width), lambda i: (i, 0))
    vec = lambda p: p.reshape(1, d)
    return pl.pallas_call(
        functools.partial(_mix_ffn_body, ff_chunk=ff_chunk),
        grid=(m // tm,),
        in_specs=[rows(k), resident((k, d)), rows(d), resident((1, d)), resident((1, d)),
                  resident((d, dff)), resident((dff, d)), resident((1, d)), resident((1, d))],
        out_specs=rows(d),
        out_shape=jax.ShapeDtypeStruct((m, d), F32),
        compiler_params=_cparams(("parallel",)),
        name="outproj_ln_ffn_ln",
    )(mix, w_out, res, vec(g1), vec(b1), w1, w2, vec(g2), vec(b2))


def _mlstm_body(q_ref, k_ref, v_ref, o_ref, gate_ref, ng_ref, h_ref, c_ref, n_ref, m_ref):
    lc = q_ref.shape[1]
    heads = range(M_HEADS)

    @pl.when(pl.program_id(1) == 0)
    def _():
        c_ref[...] = jnp.zeros_like(c_ref)
        n_ref[...] = jnp.zeros_like(n_ref)
        m_ref[...] = jnp.zeros_like(m_ref)

    ri = lax.broadcasted_iota(jnp.int32, (lc, lc), 0)
    ci = lax.broadcasted_iota(jnp.int32, (lc, lc), 1)
    causal = ci <= ri
    diag = ci == ri
    f_rows = gate_ref[0, 0, 0]
    i_rows = gate_ref[0, 0, 1]
    bcum_rows = _dot(f_rows, jnp.where(ri <= ci, 1.0, 0.0).astype(F32), HIGHEST)
    scale = M_DK ** -0.5
    q_all = q_ref[0]
    k_all = k_ref[0]
    v_all = v_ref[0]
    qb = [q_all[:, h * M_DK:(h + 1) * M_DK] for h in heads]
    kb = [k_all[:, h * M_DK:(h + 1) * M_DK] for h in heads]
    vb = [v_all[:, h * M_DV:(h + 1) * M_DV] for h in heads]
    q = [qb[h].astype(F32) for h in heads]
    k = [kb[h].astype(F32) for h in heads]
    f_row = [f_rows[h:h + 1, :] for h in heads]
    i_row = [i_rows[h:h + 1, :] for h in heads]
    bcum_row = [bcum_rows[h:h + 1, :] for h in heads]
    bcum_col = [jnp.sum(jnp.where(causal, f_row[h], 0.0), axis=-1, keepdims=True) for h in heads]
    i_col = [jnp.sum(jnp.where(diag, i_row[h], 0.0), axis=-1, keepdims=True) for h in heads]
    m_prev = [m_ref[h, 0:1, 0:1] for h in heads]
    c_prev = [c_ref[h] for h in heads]
    n_prev = [n_ref[h, 0:1, :] for h in heads]

    log_d = [jnp.where(causal, bcum_col[h] - bcum_row[h] + i_row[h], -jnp.inf) for h in heads]
    log_inter = [bcum_col[h] + m_prev[h] for h in heads]
    m_row = [jnp.maximum(jnp.max(log_d[h], axis=-1, keepdims=True), log_inter[h]) for h in heads]
    inter = [jnp.exp(log_inter[h] - m_row[h]) for h in heads]
    s = [_dot_nt(qb[h], kb[h]) * (scale * jnp.exp(log_d[h] - m_row[h])) for h in heads]
    inter_s = [scale * inter[h] for h in heads]
    qc = [_dot(qb[h], c_prev[h].astype(BF16)) for h in heads]
    num = [_dot(s[h].astype(BF16), vb[h]) + inter_s[h] * qc[h] for h in heads]
    den = [jnp.sum(s[h], axis=-1, keepdims=True) + inter_s[h] * jnp.sum(q[h] * n_prev[h], axis=-1, keepdims=True)
           for h in heads]
    hc = [num[h] / jnp.maximum(jnp.abs(den[h]), jnp.exp(-m_row[h])) for h in heads]

    b_last = [bcum_col[h][lc - 1:lc, :] for h in heads]
    log_w = [b_last[h] - bcum_col[h] + i_col[h] for h in heads]
    m_new = [jnp.maximum(b_last[h] + m_prev[h], jnp.max(log_w[h], axis=0, keepdims=True)) for h in heads]
    decay = [jnp.exp(b_last[h] + m_prev[h] - m_new[h]) for h in heads]
    kw = [k[h] * jnp.exp(log_w[h] - m_new[h]) for h in heads]
    for h in heads:
        c_ref[h] = decay[h] * c_prev[h] + _dot_tn(kw[h].astype(BF16), vb[h])
        n_ref[h] = jnp.broadcast_to(decay[h] * n_prev[h] + jnp.sum(kw[h], axis=0, keepdims=True), n_ref.shape[1:])
        m_ref[h] = jnp.broadcast_to(m_new[h], m_ref.shape[1:])

    hn = jnp.concatenate([hc[h] * lax.rsqrt(jnp.mean(jnp.square(hc[h]), axis=-1, keepdims=True) + 1e-6)
                          for h in heads], axis=1)
    h_ref[0] = (hn * ng_ref[...] * jax.nn.sigmoid(o_ref[0].astype(F32))).astype(h_ref.dtype)


def _mlstm_core(proj, gates, norm_g):
    bsz, seq, _ = proj.shape
    lc = gates.shape[-1]
    nc = seq // lc
    hdk = M_HEADS * M_DK
    hdv = M_HEADS * M_DV
    assert 2 * hdk == hdv
    return pl.pallas_call(
        _mlstm_body,
        grid=(bsz, nc),
        in_specs=[pl.BlockSpec((1, lc, hdk), lambda b, c: (b, c, 0)),
                  pl.BlockSpec((1, lc, hdk), lambda b, c: (b, c, 1)),
                  pl.BlockSpec((1, lc, hdv), lambda b, c: (b, c, 1)),
                  pl.BlockSpec((1, lc, hdv), lambda b, c: (b, c, 2)),
                  pl.BlockSpec((1, 1, 2, 8, lc), lambda b, c: (b, c, 0, 0, 0)),
                  pl.BlockSpec((1, hdv), lambda b, c: (0, 0))],
        out_specs=pl.BlockSpec((1, lc, hdv), lambda b, c: (b, c, 0)),
        out_shape=jax.ShapeDtypeStruct((bsz, seq, hdv), BF16),
        scratch_shapes=[pltpu.VMEM((M_HEADS, M_DK, M_DV), F32),
                        pltpu.VMEM((M_HEADS, 8, M_DK), F32),
                        pltpu.VMEM((M_HEADS, 8, 128), F32)],
        compiler_params=_cparams(("parallel", "arbitrary")),
        name="mlstm_chunkwise",
    )(proj, proj, proj, proj, gates, norm_g.reshape(1, hdv))


def _softplus(z):
    return jnp.maximum(z, 0.0) + jnp.log1p(jnp.exp(-jnp.abs(z)))


def _mlstm_mixer(x, w_in, b_i, b_f, conv_w, conv_b, norm_g):
    bsz, seq, d = x.shape
    m = bsz * seq
    hdv = M_HEADS * M_DV
    n_main = 2 * M_HEADS * M_DK + 2 * hdv
    xf = x.reshape(m, d)
    proj = _proj_conv_silu(xf, w_in[:, :n_main].astype(BF16), conv_w, conv_b, seq=seq, tm=512, col_chunk=1024)
    w_gate = jnp.pad(w_in[:, n_main:], ((0, 0), (0, 128 - 2 * M_HEADS))).astype(BF16)
    gate_pre = _matmul(xf, w_gate, tm=2048, tn=128, out_dtype=F32)[:, :2 * M_HEADS]
    proj = proj.reshape(bsz, seq, n_main)
    gate_pre = gate_pre.reshape(bsz, seq, 2 * M_HEADS)
    log_i = M_GATE_CAP * jnp.tanh((gate_pre[..., :M_HEADS] + b_i) / M_GATE_CAP)
    log_f = -_softplus(-(M_GATE_CAP * jnp.tanh((gate_pre[..., M_HEADS:] + b_f) / M_GATE_CAP)))
    lc = min(M_CHUNK, seq)
    gates = jnp.stack([log_f, log_i], axis=0)
    gates = gates.reshape(2, bsz, seq // lc, lc, M_HEADS).transpose(1, 2, 0, 4, 3)
    gates = jnp.pad(gates, ((0, 0), (0, 0), (0, 0), (0, 8 - M_HEADS), (0, 0)))
    return _mlstm_core(proj, gates, norm_g)


def _split_bf16(x):
    hi = x.astype(BF16)
    return hi, (x - hi.astype(F32)).astype(BF16)


def _wkv_body(r_ref, k_ref, v_ref, tail_ref, w2_ref, a2_ref, g2_ref, w0_ref, a0_ref, kk_ref, ka_ref, rk_ref,
              gng_ref, gnb_ref, o_ref, s_ref, *, n_tanh):
    tb = r_ref.shape[1]
    lc = min(R_CHUNK, tb)
    chunks = range(tb // lc)
    width = r_ref.shape[2]
    gw = R_GROUP * R_N
    rows = R_GROUP * lc
    assert rows == gw and tb % lc == 0

    @pl.when(pl.program_id(2) == 0)
    def _():
        s_ref[...] = jnp.zeros_like(s_ref)

    pairs = range(width // gw)
    sls = [slice(p * gw, (p + 1) * gw) for p in pairs]
    brow = lax.broadcasted_iota(jnp.int32, (gw, gw), 0) // R_N
    bcol = lax.broadcasted_iota(jnp.int32, (gw, gw), 1) // R_N
    same_head = brow == bcol
    head_ones = jnp.where(same_head, 1.0, 0.0).astype(BF16)
    head_ones = jnp.concatenate([head_ones, head_ones], axis=0)

    def seg_sum(x):
        hi, lo = _split_bf16(x)
        return jnp.concatenate([_dot(jnp.concatenate([hi[:, sl], lo[:, sl]], axis=1), head_ones) for sl in sls],
                               axis=1)

    r = r_ref[0].astype(F32)
    k = k_ref[0].astype(F32)
    v = v_ref[0].astype(F32)
    n_wa = w2_ref.shape[0]
    tail = tail_ref[0].astype(F32)
    t_wa = tail[:, :n_wa]
    is_tanh = lax.broadcasted_iota(jnp.int32, (1, n_wa), 1) < n_tanh
    f_wa = jnp.where(is_tanh, jnp.tanh(t_wa), t_wa).astype(BF16)
    gate = _dot(jax.nn.sigmoid(tail[:, n_wa:]).astype(BF16), g2_ref[...])
    ld = (-R_DECAY_SCALE) * jax.nn.sigmoid(w0_ref[...] + _dot(f_wa, w2_ref[...]))
    a_lr = jax.nn.sigmoid(a0_ref[...] + _dot(f_wa, a2_ref[...]))
    kk = k * kk_ref[...]
    kk = kk / jnp.maximum(jnp.sqrt(seg_sum(jnp.square(kk))), 1e-12)
    k = k * (1.0 + (a_lr - 1.0) * ka_ref[...])
    b_ = kk * a_lr

    ti = lax.broadcasted_iota(jnp.int32, (tb, tb), 0)
    tj = lax.broadcasted_iota(jnp.int32, (tb, tb), 1)
    in_chunk_tri = jnp.logical_and(tj <= ti, ti // lc == tj // lc)
    tri = jnp.where(in_chunk_tri, 1.0, 0.0).astype(BF16)
    cum = _dot(jnp.concatenate([tri, tri], axis=1), jnp.concatenate(_split_bf16(ld), axis=0))
    e_pos = jnp.exp(cum)
    e_neg = jnp.exp(-cum)
    w_end = [jnp.exp(cum[(c + 1) * lc - 1:(c + 1) * lc, :]) for c in chunks]
    w_end_rows = jnp.concatenate([jnp.broadcast_to(w_end[c], (lc, width)) for c in chunks], axis=0)
    a_til = -kk * jnp.exp(cum - ld)
    r_til = r * e_pos
    b_til = b_ * e_neg
    k_til = k * e_neg
    b_hat = b_til * w_end_rows
    k_hat = k_til * w_end_rows

    lane_head = lax.broadcasted_iota(jnp.int32, (lc, gw), 1) // R_N
    head_sel = [lane_head == h for h in range(R_GROUP)]

    def stack(x):
        zero = jnp.zeros_like(x)
        return jnp.concatenate([jnp.where(sel, x, zero) for sel in head_sel], axis=0)

    trow = lax.broadcasted_iota(jnp.int32, (lc, gw), 0)
    tcol = lax.broadcasted_iota(jnp.int32, (lc, gw), 1) % lc
    strict = tcol < trow
    causal = tcol <= trow
    eye = jnp.where(tcol == trow, 1.0, 0.0).astype(F32)
    level_masks = []
    s = 1
    while s < lc:
        lo, hi = s.bit_length() - 1, s.bit_length()
        level_masks.append(jnp.logical_and((trow >> hi) == (tcol >> hi), (trow >> lo) != (tcol >> lo)))
        s *= 2

    units = [(c, p) for c in chunks for p in pairs]
    tile = lambda x, c, p: x[c * lc:(c + 1) * lc, sls[p]]
    ar_b = {u: jnp.concatenate([tile(a_til, *u), tile(r_til, *u)], axis=0).astype(BF16) for u in units}
    bk_s = {u: jnp.concatenate([stack(tile(b_til, *u).astype(BF16)), stack(tile(k_til, *u).astype(BF16))], axis=0)
            for u in units}
    v_b = {u: tile(v, *u).astype(BF16) for u in units}
    v_s = {u: stack(v_b[u]) for u in units}
    bk_hat = {u: jnp.concatenate([tile(b_hat, *u), tile(k_hat, *u)], axis=0).astype(BF16) for u in units}

    pm = {u: _dot_nt(ar_b[u], bk_s[u]) for u in units}
    n_ab = {u: jnp.where(strict, pm[u][:lc, :rows], 0.0) for u in units}
    m_xk = {u: jnp.concatenate([jnp.where(strict, pm[u][:lc, rows:], 0.0),
                                jnp.where(causal, pm[u][lc:, rows:], 0.0)], axis=0).astype(BF16) for u in units}
    m_rb = {u: jnp.where(causal, pm[u][lc:, :rows], 0.0).astype(BF16) for u in units}

    n_b = {u: n_ab[u].astype(BF16) for u in units}
    zero_b = jnp.zeros((lc, gw), BF16)
    t_inv = {u: eye + jnp.where(level_masks[0], n_ab[u], 0.0) for u in units}
    for msk in level_masks[1:]:
        t_b = {u: t_inv[u].astype(BF16) for u in units}
        half = {u: _dot(t_b[u], stack(jnp.where(msk, n_b[u], zero_b))) for u in units}
        t_inv = {u: t_inv[u] + _dot(half[u].astype(BF16), stack(t_b[u])) for u in units}
    t_b = {u: t_inv[u].astype(BF16) for u in units}
    resid = {u: eye - t_b[u].astype(F32) + _dot(n_b[u], stack(t_b[u])) for u in units}
    t_fix = {u: _dot(t_b[u], stack(resid[u].astype(BF16))).astype(BF16) for u in units}

    state = [s_ref[p] for p in pairs]
    y_rows = []
    for c in chunks:
        us = [(c, p) for p in pairs]
        zy = [_dot_nt(ar_b[u], state[u[1]].astype(BF16)) + _dot(m_xk[u], v_s[u]) for u in us]
        ub = []
        for i, u in enumerate(us):
            z_s = stack(zy[i][:lc].astype(BF16))
            ub.append(_dot(jnp.concatenate([t_b[u], t_fix[u]], axis=1), jnp.concatenate([z_s, z_s], axis=0)).astype(BF16))
        y_rows.append(jnp.concatenate([zy[i][lc:] + _dot(m_rb[u], stack(ub[i])) for i, u in enumerate(us)], axis=1))
        for i, u in enumerate(us):
            upd = _dot_tn(jnp.concatenate([ub[i], v_b[u]], axis=0), bk_hat[u])
            state[u[1]] = state[u[1]] * w_end[c][:, sls[u[1]]] + jnp.where(same_head, upd, 0.0)
    for p in pairs:
        s_ref[p] = state[p]
    y = jnp.concatenate(y_rows, axis=0)

    mu = seg_sum(y) * (1.0 / R_N)
    yc = y - mu
    var = seg_sum(jnp.square(yc)) * (1.0 / R_N)
    yn = yc * lax.rsqrt(var + R_GN_EPS) * gng_ref[...] + gnb_ref[...]
    bonus = seg_sum(r * k * rk_ref[...]) * v
    o_ref[0] = ((yn + bonus) * gate).astype(o_ref.dtype)


def _wkv_core(proj, w2, a2, g2, w0, a0, k_k, k_a, r_k, gn_g, gn_b):
    bsz, seq, n_proj = proj.shape
    d = w0.shape[-1]
    lw, la, lg = w2.shape[0], a2.shape[0], g2.shape[0]
    n_tail = lw + la + lg
    assert n_proj == 3 * d + n_tail and (3 * d) % n_tail == 0
    gw = R_GROUP * R_N
    width = R_PAIRS * gw
    nb = d // width
    tb = min(R_CHUNK * R_CHUNKS_PER_STEP, seq)
    nc = seq // tb
    w2p = jnp.concatenate([w2, jnp.zeros_like(a2)], axis=0).astype(BF16)
    a2p = jnp.concatenate([jnp.zeros_like(w2), a2], axis=0).astype(BF16)
    tile = lambda off: pl.BlockSpec((1, tb, width), lambda b, g, c: (b, c, off * nb + g))
    tail = pl.BlockSpec((1, tb, n_tail), lambda b, g, c: (b, c, 3 * d // n_tail))
    wa = pl.BlockSpec((lw + la, width), lambda b, g, c: (0, g))
    wg = pl.BlockSpec((lg, width), lambda b, g, c: (0, g))
    par = pl.BlockSpec((1, width), lambda b, g, c: (0, g))
    row = lambda p: p.reshape(1, d)
    return pl.pallas_call(
        functools.partial(_wkv_body, n_tanh=lw),
        grid=(bsz, nb, nc),
        in_specs=[tile(0), tile(1), tile(2), tail, wa, wa, wg] + [par] * 7,
        out_specs=pl.BlockSpec((1, tb, width), lambda b, g, c: (b, c, g)),
        out_shape=jax.ShapeDtypeStruct((bsz, seq, d), BF16),
        scratch_shapes=[pltpu.VMEM((R_PAIRS, gw, gw), F32)],
        compiler_params=_cparams(("parallel", "parallel", "arbitrary")),
        name="wkv7_chunked",
    )(proj, proj, proj, proj, w2p, a2p, g2.astype(BF16),
      row(w0), row(a0), row(k_k), row(k_a), row(r_k), row(gn_g), row(gn_b))


def _rwkv_mixer(x, w_in, mu, w0, w2, a0, a2, g2, k_k, k_a, r_k, gn_g, gn_b):
    bsz, seq, d = x.shape
    n_proj = w_in.shape[1]
    proj = _proj_shift_lerp(x.reshape(bsz * seq, d), w_in.astype(BF16), mu, seq=seq, tm=512, col_chunk=n_proj // 2)
    return _wkv_core(proj.reshape(bsz, seq, n_proj), w2, a2, g2, w0, a0, k_k, k_a, r_k, gn_g, gn_b)


def kernel(x, mlstm_w_in, mlstm_b_i, mlstm_b_f, mlstm_conv_w, mlstm_conv_b, mlstm_norm_g, mlstm_w_out, rwkv_w_in, rwkv_mu, rwkv_w0, rwkv_w2, rwkv_a0, rwkv_a2, rwkv_g2, rwkv_k_k, rwkv_k_a, rwkv_r_k, rwkv_gn_g, rwkv_gn_b, rwkv_w_out, ln_mix_g, ln_mix_b, mlp_w1, mlp_w2, ln_ffn_g, ln_ffn_b):
    bsz, seq, d = x.shape
    m = bsz * seq
    xf = x.reshape(m, d)
    for layer in range(DEPTH):
        j = layer // 2
        if layer % 2 == 0:
            mix = _mlstm_mixer(xf.reshape(bsz, seq, d), mlstm_w_in[j], mlstm_b_i[j], mlstm_b_f[j],
                               mlstm_conv_w[j], mlstm_conv_b[j], mlstm_norm_g[j])
            w_out = mlstm_w_out[j]
        else:
            mix = _rwkv_mixer(xf.reshape(bsz, seq, d), rwkv_w_in[j], rwkv_mu[j], rwkv_w0[j], rwkv_w2[j],
                              rwkv_a0[j], rwkv_a2[j], rwkv_g2[j], rwkv_k_k[j], rwkv_k_a[j],
                              rwkv_r_k[j].reshape(-1), rwkv_gn_g[j], rwkv_gn_b[j])
            w_out = rwkv_w_out[j]
        xf = _mix_ffn(mix.reshape(m, -1), w_out.astype(BF16), xf, ln_mix_g[layer], ln_mix_b[layer],
                      mlp_w1[layer].astype(BF16), mlp_w2[layer].astype(BF16), ln_ffn_g[layer], ln_ffn_b[layer],
                      tm=512, ff_chunk=1024)
    return xf.reshape(bsz, seq, d)
```

```python
import functools
import math

import jax
import jax.numpy as jnp
from jax import lax
from jax.experimental import pallas as pl
from jax.experimental.pallas import tpu as pltpu

F32 = jnp.float32
BF16 = jnp.bfloat16
HIGHEST = lax.Precision.HIGHEST

DEPTH = 2
DN_ALPHA = (2.0 * DEPTH) ** 0.25
LN_EPS = 1e-5

M_HEADS = 4
M_DK = 128
M_DV = 256
M_GATE_CAP = 15.0
M_CHUNK = 256

R_N = 64
R_GN_EPS = 64e-5
R_DECAY_SCALE = math.exp(-0.5)
R_CHUNK = 64
R_GROUP = 2
R_PAIRS = 8
R_CHUNKS_PER_STEP = 2

VMEM_LIMIT = 48 * 1024 * 1024


def _cparams(sem):
    return pltpu.CompilerParams(dimension_semantics=sem, vmem_limit_bytes=VMEM_LIMIT)


def _dot(a, b, precision=None):
    return jnp.dot(a, b, preferred_element_type=F32, precision=precision)


def _dot_nt(a, b, precision=None):
    return lax.dot_general(a, b, (((1,), (1,)), ((), ())), preferred_element_type=F32, precision=precision)


def _dot_tn(a, b, precision=None):
    return lax.dot_general(a, b, (((0,), (0,)), ((), ())), preferred_element_type=F32, precision=precision)


def _proj_lerp_body(x_ref, w_ref, mu_ref, o_ref, carry_ref, *, tiles_per_seq, col_chunk):
    @pl.when(pl.program_id(0) % tiles_per_seq == 0)
    def _():
        carry_ref[...] = jnp.zeros_like(carry_ref)

    xb = x_ref[...].astype(BF16)
    tm = xb.shape[0]
    first = lax.broadcasted_iota(jnp.int32, (8, col_chunk), 0) == 0
    for c0 in range(0, w_ref.shape[1], col_chunk):
        cols = slice(c0, c0 + col_chunk)
        acc = _dot(xb, w_ref[:, cols])
        rolled = pltpu.roll(acc, 1, 0)
        top = jnp.where(first, carry_ref[0:1, cols], rolled[0:8])
        shifted = jnp.concatenate([top, rolled[8:]], axis=0)
        o_ref[:, cols] = (acc + mu_ref[:, cols] * (shifted - acc)).astype(o_ref.dtype)
        carry_ref[:, cols] = jnp.broadcast_to(acc[tm - 1:tm, :], (8, col_chunk))


def _proj_shift_lerp(x, w, mu, *, seq, tm, col_chunk):
    m, k = x.shape
    n = w.shape[1]
    tm = min(tm, seq)
    assert seq % tm == 0 and n % col_chunk == 0
    return pl.pallas_call(
        functools.partial(_proj_lerp_body, tiles_per_seq=seq // tm, col_chunk=col_chunk),
        grid=(m // tm,),
        in_specs=[pl.BlockSpec((tm, k), lambda i: (i, 0)),
                  pl.BlockSpec((k, n), lambda i: (0, 0), pipeline_mode=pl.Buffered(1)),
                  pl.BlockSpec((1, n), lambda i: (0, 0), pipeline_mode=pl.Buffered(1))],
        out_specs=pl.BlockSpec((tm, n), lambda i: (i, 0)),
        out_shape=jax.ShapeDtypeStruct((m, n), F32),
        scratch_shapes=[pltpu.VMEM((8, n), F32)],
        compiler_params=_cparams(("arbitrary",)),
        name="proj_token_shift",
    )(x, w, mu.reshape(1, n))


def _proj_conv_body(x_ref, w_ref, wg_ref, cw_ref, cb_ref, o_ref, g_ref, carry_ref, *, tiles_per_seq, col_chunk):
    @pl.when(pl.program_id(0) % tiles_per_seq == 0)
    def _():
        carry_ref[...] = jnp.zeros_like(carry_ref)

    xb = x_ref[...].astype(BF16)
    tm = xb.shape[0]
    taps, n_conv = cw_ref.shape
    for c0 in range(0, w_ref.shape[1], col_chunk):
        cols = slice(c0, c0 + col_chunk)
        acc = _dot(xb, w_ref[:, cols])
        if c0 < n_conv:
            ext = jnp.concatenate([carry_ref[:, cols], acc], axis=0)
            conv = cw_ref[0:1, cols] * ext
            for tap in range(1, taps):
                conv = pltpu.roll(conv, 1, 0) + cw_ref[tap:tap + 1, cols] * ext
            conv = conv[8:] + cb_ref[:, cols]
            o_ref[:, cols] = (conv * jax.nn.sigmoid(conv)).astype(o_ref.dtype)
            carry_ref[:, cols] = acc[tm - 8:tm]
        else:
            o_ref[:, cols] = acc.astype(o_ref.dtype)
    g_ref[...] = _dot(xb, wg_ref[...])


def _proj_conv_silu(x, w, w_narrow, conv_w, conv_b, *, seq, tm, col_chunk):
    m, k = x.shape
    n = w.shape[1]
    n_narrow = w_narrow.shape[1]
    taps, n_conv = conv_w.shape
    tm = min(tm, seq)
    assert seq % tm == 0 and n % col_chunk == 0 and n_conv % col_chunk == 0 and taps <= 8
    resident = lambda shape: pl.BlockSpec(shape, lambda i: (0, 0), pipeline_mode=pl.Buffered(1))
    return pl.pallas_call(
        functools.partial(_proj_conv_body, tiles_per_seq=seq // tm, col_chunk=col_chunk),
        grid=(m // tm,),
        in_specs=[pl.BlockSpec((tm, k), lambda i: (i, 0)), resident((k, n)), resident((k, n_narrow)),
                  resident((taps, n_conv)), resident((1, n_conv))],
        out_specs=[pl.BlockSpec((tm, n), lambda i: (i, 0)), pl.BlockSpec((tm, n_narrow), lambda i: (i, 0))],
        out_shape=[jax.ShapeDtypeStruct((m, n), BF16), jax.ShapeDtypeStruct((m, n_narrow), F32)],
        scratch_shapes=[pltpu.VMEM((8, n_conv), F32)],
        compiler_params=_cparams(("arbitrary",)),
        name="proj_conv_silu",
    )(x, w, w_narrow, conv_w, conv_b.reshape(1, n_conv))


def _layer_norm(y, g, b):
    mu = jnp.mean(y, axis=-1, keepdims=True)
    yc = y - mu
    var = jnp.mean(jnp.square(yc), axis=-1, keepdims=True)
    return yc * lax.rsqrt(var + LN_EPS) * g + b


def _mix_ffn_body(mix_ref, wo_ref, res_ref, g1_ref, b1_ref, w1_ref, w2_ref, g2_ref, b2_ref, o_ref, *,
                  ff_chunk, row_split):
    sub = mix_ref.shape[0] // row_split

    def stages(rows):
        st = {}

        def out_proj():
            st["y"] = DN_ALPHA * res_ref[rows, :] + _dot(mix_ref[rows, :], wo_ref[...])

        def norm1():
            st["x1"] = _layer_norm(st.pop("y"), g1_ref[...], b1_ref[...])
            st["x1b"] = st["x1"].astype(BF16)
            st["acc"] = DN_ALPHA * st.pop("x1")

        def up(f):
            st["hid"] = _dot(st["x1b"], w1_ref[:, f:f + ff_chunk])

        def act():
            st["hid"] = jnp.square(jnp.maximum(st["hid"], 0.0)).astype(BF16)

        def down(f):
            st["acc"] = st["acc"] + _dot(st.pop("hid"), w2_ref[f:f + ff_chunk, :])

        def norm2():
            o_ref[rows, :] = _layer_norm(st.pop("acc"), g2_ref[...], b2_ref[...])

        seq = [out_proj, norm1]
        for f in range(0, w1_ref.shape[1], ff_chunk):
            seq += [functools.partial(up, f), act, functools.partial(down, f)]
        return seq + [norm2]

    threads = [stages(slice(h * sub, (h + 1) * sub)) for h in range(row_split)]
    for t in range(len(threads[0]) + row_split - 1):
        for h, seq in enumerate(threads):
            if 0 <= t - h < len(seq):
                seq[t - h]()


def _mix_ffn(mix, w_out, res, g1, b1, w1, w2, g2, b2, *, tm, ff_chunk, row_split):
    m, k = mix.shape
    d = w_out.shape[1]
    dff = w1.shape[1]
    tm = min(tm, m)
    assert m % tm == 0 and dff % ff_chunk == 0 and tm % (8 * row_split) == 0
    resident = lambda shape: pl.BlockSpec(shape, lambda i: (0, 0), pipeline_mode=pl.Buffered(1))
    rows = lambda width: pl.BlockSpec((tm, width), lambda i: (i, 0))
    vec = lambda p: p.reshape(1, d)
    return pl.pallas_call(
        functools.partial(_mix_ffn_body, ff_chunk=ff_chunk, row_split=row_split),
        grid=(m // tm,),
        in_specs=[rows(k), resident((k, d)), rows(d), resident((1, d)), resident((1, d)),
                  resident((d, dff)), resident((dff, d)), resident((1, d)), resident((1, d))],
        out_specs=rows(d),
        out_shape=jax.ShapeDtypeStruct((m, d), F32),
        compiler_params=_cparams(("parallel",)),
        name="outproj_ln_ffn_ln",
    )(mix, w_out, res, vec(g1), vec(b1), w1, w2, vec(g2), vec(b2))


def _mlstm_body(q_ref, k_ref, v_ref, o_ref, gate_ref, ng_ref, h_ref, c_ref, n_ref, m_ref):
    lc = q_ref.shape[1]
    heads = range(M_HEADS)

    @pl.when(pl.program_id(1) == 0)
    def _():
        c_ref[...] = jnp.zeros_like(c_ref)
        n_ref[...] = jnp.zeros_like(n_ref)
        m_ref[...] = jnp.zeros_like(m_ref)

    ri = lax.broadcasted_iota(jnp.int32, (lc, lc), 0)
    ci = lax.broadcasted_iota(jnp.int32, (lc, lc), 1)
    causal = ci <= ri
    diag = ci == ri
    f_rows = gate_ref[0, 0, 0]
    i_rows = gate_ref[0, 0, 1]
    bcum_rows = _dot(f_rows, jnp.where(ri <= ci, 1.0, 0.0).astype(F32), HIGHEST)
    scale = M_DK ** -0.5
    q_all = q_ref[0]
    k_all = k_ref[0]
    v_all = v_ref[0]
    qb = [q_all[:, h * M_DK:(h + 1) * M_DK] for h in heads]
    kb = [k_all[:, h * M_DK:(h + 1) * M_DK] for h in heads]
    vb = [v_all[:, h * M_DV:(h + 1) * M_DV] for h in heads]
    q = [qb[h].astype(F32) for h in heads]
    k = [kb[h].astype(F32) for h in heads]
    f_row = [f_rows[h:h + 1, :] for h in heads]
    i_row = [i_rows[h:h + 1, :] for h in heads]
    bcum_row = [bcum_rows[h:h + 1, :] for h in heads]
    bcum_col = [jnp.sum(jnp.where(causal, f_row[h], 0.0), axis=-1, keepdims=True) for h in heads]
    i_col = [jnp.sum(jnp.where(diag, i_row[h], 0.0), axis=-1, keepdims=True) for h in heads]
    m_prev = [m_ref[h, 0:1, 0:1] for h in heads]
    c_prev = [c_ref[h] for h in heads]
    n_prev = [n_ref[h, 0:1, :] for h in heads]

    log_d = [jnp.where(causal, bcum_col[h] - bcum_row[h] + i_row[h], -jnp.inf) for h in heads]
    log_inter = [bcum_col[h] + m_prev[h] for h in heads]
    m_row = [jnp.maximum(jnp.max(log_d[h], axis=-1, keepdims=True), log_inter[h]) for h in heads]
    inter = [jnp.exp(log_inter[h] - m_row[h]) for h in heads]
    s = [_dot_nt(qb[h], kb[h]) * (scale * jnp.exp(log_d[h] - m_row[h])) for h in heads]
    inter_s = [scale * inter[h] for h in heads]
    qc = [_dot(qb[h], c_prev[h].astype(BF16)) for h in heads]
    num = [_dot(s[h].astype(BF16), vb[h]) + inter_s[h] * qc[h] for h in heads]
    den = [jnp.sum(s[h], axis=-1, keepdims=True) + inter_s[h] * jnp.sum(q[h] * n_prev[h], axis=-1, keepdims=True)
           for h in heads]
    hc = [num[h] / jnp.maximum(jnp.abs(den[h]), jnp.exp(-m_row[h])) for h in heads]

    b_last = [bcum_col[h][lc - 1:lc, :] for h in heads]
    log_w = [b_last[h] - bcum_col[h] + i_col[h] for h in heads]
    m_new = [jnp.maximum(b_last[h] + m_prev[h], jnp.max(log_w[h], axis=0, keepdims=True)) for h in heads]
    decay = [jnp.exp(b_last[h] + m_prev[h] - m_new[h]) for h in heads]
    kw = [k[h] * jnp.exp(log_w[h] - m_new[h]) for h in heads]
    for h in heads:
        c_ref[h] = decay[h] * c_prev[h] + _dot_tn(kw[h].astype(BF16), vb[h])
        n_ref[h] = jnp.broadcast_to(decay[h] * n_prev[h] + jnp.sum(kw[h], axis=0, keepdims=True), n_ref.shape[1:])
        m_ref[h] = jnp.broadcast_to(m_new[h], m_ref.shape[1:])

    hn = jnp.concatenate([hc[h] * lax.rsqrt(jnp.mean(jnp.square(hc[h]), axis=-1, keepdims=True) + 1e-6)
                          for h in heads], axis=1)
    h_ref[0] = (hn * ng_ref[...] * jax.nn.sigmoid(o_ref[0].astype(F32))).astype(h_ref.dtype)


def _mlstm_core(proj, gates, norm_g):
    bsz, seq, _ = proj.shape
    lc = gates.shape[-1]
    nc = seq // lc
    hdk = M_HEADS * M_DK
    hdv = M_HEADS * M_DV
    assert 2 * hdk == hdv
    return pl.pallas_call(
        _mlstm_body,
        grid=(bsz, nc),
        in_specs=[pl.BlockSpec((1, lc, hdk), lambda b, c: (b, c, 0)),
                  pl.BlockSpec((1, lc, hdk), lambda b, c: (b, c, 1)),
                  pl.BlockSpec((1, lc, hdv), lambda b, c: (b, c, 1)),
                  pl.BlockSpec((1, lc, hdv), lambda b, c: (b, c, 2)),
                  pl.BlockSpec((1, 1, 2, 8, lc), lambda b, c: (b, c, 0, 0, 0)),
                  pl.BlockSpec((1, hdv), lambda b, c: (0, 0))],
        out_specs=pl.BlockSpec((1, lc, hdv), lambda b, c: (b, c, 0)),
        out_shape=jax.ShapeDtypeStruct((bsz, seq, hdv), BF16),
        scratch_shapes=[pltpu.VMEM((M_HEADS, M_DK, M_DV), F32),
                        pltpu.VMEM((M_HEADS, 8, M_DK), F32),
                        pltpu.VMEM((M_HEADS, 8, 128), F32)],
        compiler_params=_cparams(("parallel", "arbitrary")),
        name="mlstm_chunkwise",
    )(proj, proj, proj, proj, gates, norm_g.reshape(1, hdv))


def _softplus(z):
    return jnp.maximum(z, 0.0) + jnp.log1p(jnp.exp(-jnp.abs(z)))


def _mlstm_mixer(x, w_in, b_i, b_f, conv_w, conv_b, norm_g):
    bsz, seq, d = x.shape
    m = bsz * seq
    hdv = M_HEADS * M_DV
    n_main = 2 * M_HEADS * M_DK + 2 * hdv
    xf = x.reshape(m, d)
    w_gate = jnp.pad(w_in[:, n_main:], ((0, 0), (0, 128 - 2 * M_HEADS))).astype(BF16)
    proj, gate_pre = _proj_conv_silu(xf, w_in[:, :n_main].astype(BF16), w_gate, conv_w, conv_b,
                                     seq=seq, tm=512, col_chunk=1024)
    gate_pre = gate_pre[:, :2 * M_HEADS]
    proj = proj.reshape(bsz, seq, n_main)
    gate_pre = gate_pre.reshape(bsz, seq, 2 * M_HEADS)
    log_i = M_GATE_CAP * jnp.tanh((gate_pre[..., :M_HEADS] + b_i) / M_GATE_CAP)
    log_f = -_softplus(-(M_GATE_CAP * jnp.tanh((gate_pre[..., M_HEADS:] + b_f) / M_GATE_CAP)))
    lc = min(M_CHUNK, seq)
    gates = jnp.stack([log_f, log_i], axis=0)
    gates = gates.reshape(2, bsz, seq // lc, lc, M_HEADS).transpose(1, 2, 0, 4, 3)
    gates = jnp.pad(gates, ((0, 0), (0, 0), (0, 0), (0, 8 - M_HEADS), (0, 0)))
    return _mlstm_core(proj, gates, norm_g)


def _split_bf16(x):
    hi = x.astype(BF16)
    return hi, (x - hi.astype(F32)).astype(BF16)


def _wkv_body(r_ref, k_ref, v_ref, tail_ref, w2_ref, a2_ref, g2_ref, w0_ref, a0_ref, kk_ref, ka_ref, rk_ref,
              gng_ref, gnb_ref, o_ref, s_ref, *, n_tanh):
    tb = r_ref.shape[1]
    lc = min(R_CHUNK, tb)
    chunks = range(tb // lc)
    width = r_ref.shape[2]
    gw = R_GROUP * R_N
    rows = R_GROUP * lc
    assert rows == gw and tb % lc == 0

    @pl.when(pl.program_id(2) == 0)
    def _():
        s_ref[...] = jnp.zeros_like(s_ref)

    pairs = range(width // gw)
    sls = [slice(p * gw, (p + 1) * gw) for p in pairs]
    brow = lax.broadcasted_iota(jnp.int32, (gw, gw), 0) // R_N
    bcol = lax.broadcasted_iota(jnp.int32, (gw, gw), 1) // R_N
    same_head = brow == bcol
    head_ones = jnp.where(same_head, 1.0, 0.0).astype(BF16)

    def seg_sum(x, parts=1):
        pieces = (x.astype(BF16),) if parts == 1 else _split_bf16(x)
        ones = jnp.concatenate([head_ones] * len(pieces), axis=0)
        return jnp.concatenate([_dot(jnp.concatenate([piece[:, sl] for piece in pieces], axis=1), ones)
                                for sl in sls], axis=1)

    r = r_ref[0].astype(F32)
    k = k_ref[0].astype(F32)
    v = v_ref[0].astype(F32)
    n_wa = w2_ref.shape[0]
    tail = tail_ref[0].astype(F32)
    t_wa = tail[:, :n_wa]
    is_tanh = lax.broadcasted_iota(jnp.int32, (1, n_wa), 1) < n_tanh
    f_wa = jnp.where(is_tanh, jnp.tanh(t_wa), t_wa).astype(BF16)
    gate = _dot(jax.nn.sigmoid(tail[:, n_wa:]).astype(BF16), g2_ref[...])
    ld = (-R_DECAY_SCALE) * jax.nn.sigmoid(w0_ref[...] + _dot(f_wa, w2_ref[...]))
    a_lr = jax.nn.sigmoid(a0_ref[...] + _dot(f_wa, a2_ref[...]))
    kk = k * kk_ref[...]
    kk = kk / jnp.maximum(jnp.sqrt(seg_sum(jnp.square(kk))), 1e-12)
    k = k * (1.0 + (a_lr - 1.0) * ka_ref[...])
    b_ = kk * a_lr

    ti = lax.broadcasted_iota(jnp.int32, (tb, tb), 0)
    tj = lax.broadcasted_iota(jnp.int32, (tb, tb), 1)
    in_chunk_tri = jnp.logical_and(tj <= ti, ti // lc == tj // lc)
    tri = jnp.where(in_chunk_tri, 1.0, 0.0).astype(BF16)
    cum = _dot(jnp.concatenate([tri, tri], axis=1), jnp.concatenate(_split_bf16(ld), axis=0))
    e_pos = jnp.exp(cum)
    e_neg = jnp.exp(-cum)
    w_end = [jnp.exp(cum[(c + 1) * lc - 1:(c + 1) * lc, :]) for c in chunks]
    w_end_rows = jnp.concatenate([jnp.broadcast_to(w_end[c], (lc, width)) for c in chunks], axis=0)
    a_til = -kk * jnp.exp(cum - ld)
    r_til = r * e_pos
    b_til = b_ * e_neg
    k_til = k * e_neg
    b_hat = b_til * w_end_rows
    k_hat = k_til * w_end_rows

    lane_head = lax.broadcasted_iota(jnp.int32, (lc, gw), 1) // R_N
    head_sel = [lane_head == h for h in range(R_GROUP)]

    def stack(x):
        zero = jnp.zeros_like(x)
        return jnp.concatenate([jnp.where(sel, x, zero) for sel in head_sel], axis=0)

    trow = lax.broadcasted_iota(jnp.int32, (lc, gw), 0)
    tcol = lax.broadcasted_iota(jnp.int32, (lc, gw), 1) % lc
    strict = tcol < trow
    causal = tcol <= trow
    eye = jnp.where(tcol == trow, 1.0, 0.0).astype(F32)
    level_masks = []
    s = 1
    while s < lc:
        lo, hi = s.bit_length() - 1, s.bit_length()
        level_masks.append(jnp.logical_and((trow >> hi) == (tcol >> hi), (trow >> lo) != (tcol >> lo)))
        s *= 2

    units = [(c, p) for c in chunks for p in pairs]
    tile = lambda x, c, p: x[c * lc:(c + 1) * lc, sls[p]]
    ar_b = {u: jnp.concatenate([tile(a_til, *u), tile(r_til, *u)], axis=0).astype(BF16) for u in units}
    bk_s = {u: jnp.concatenate([stack(tile(b_til, *u).astype(BF16)), stack(tile(k_til, *u).astype(BF16))], axis=0)
            for u in units}
    v_b = {u: tile(v, *u).astype(BF16) for u in units}
    v_s = {u: stack(v_b[u]) for u in units}
    bk_hat = {u: jnp.concatenate([tile(b_hat, *u), tile(k_hat, *u)], axis=0).astype(BF16) for u in units}

    pm = {u: _dot_nt(ar_b[u], bk_s[u]) for u in units}
    n_ab = {u: jnp.where(strict, pm[u][:lc, :rows], 0.0) for u in units}
    m_xk = {u: jnp.concatenate([jnp.where(strict, pm[u][:lc, rows:], 0.0),
                                jnp.where(causal, pm[u][lc:, rows:], 0.0)], axis=0).astype(BF16) for u in units}
    m_rb = {u: jnp.where(causal, pm[u][lc:, :rows], 0.0).astype(BF16) for u in units}

    n_b = {u: n_ab[u].astype(BF16) for u in units}
    zero_b = jnp.zeros((lc, gw), BF16)
    t_inv = {u: eye + jnp.where(level_masks[0], n_ab[u], 0.0) for u in units}
    for msk in level_masks[1:]:
        t_b = {u: t_inv[u].astype(BF16) for u in units}
        half = {u: _dot(t_b[u], stack(jnp.where(msk, n_b[u], zero_b))) for u in units}
        t_inv = {u: t_inv[u] + _dot(half[u].astype(BF16), stack(t_b[u])) for u in units}
    t_b = {u: t_inv[u].astype(BF16) for u in units}
    resid = {u: eye - t_b[u].astype(F32) + _dot(n_b[u], stack(t_b[u])) for u in units}
    t_fix = {u: _dot(t_b[u], stack(resid[u].astype(BF16))).astype(BF16) for u in units}

    state = [s_ref[p] for p in pairs]
    y_rows = []
    for c in chunks:
        us = [(c, p) for p in pairs]
        zy = [_dot_nt(ar_b[u], state[u[1]].astype(BF16)) + _dot(m_xk[u], v_s[u]) for u in us]
        ub = []
        for i, u in enumerate(us):
            z_s = stack(zy[i][:lc].astype(BF16))
            ub.append(_dot(jnp.concatenate([t_b[u], t_fix[u]], axis=1), jnp.concatenate([z_s, z_s], axis=0)).astype(BF16))
        y_rows.append(jnp.concatenate([zy[i][lc:] + _dot(m_rb[u], stack(ub[i])) for i, u in enumerate(us)], axis=1))
        for i, u in enumerate(us):
            upd = _dot_tn(jnp.concatenate([ub[i], v_b[u]], axis=0), bk_hat[u])
            state[u[1]] = state[u[1]] * w_end[c][:, sls[u[1]]] + jnp.where(same_head, upd, 0.0)
    for p in pairs:
        s_ref[p] = state[p]
    y = jnp.concatenate(y_rows, axis=0)

    mu = seg_sum(y, parts=2) * (1.0 / R_N)
    yc = y - mu
    var = seg_sum(jnp.square(yc)) * (1.0 / R_N)
    yn = yc * lax.rsqrt(var + R_GN_EPS) * gng_ref[...] + gnb_ref[...]
    bonus = seg_sum(r * k * rk_ref[...]) * v
    o_ref[0] = ((yn + bonus) * gate).astype(o_ref.dtype)


def _wkv_core(proj, w2, a2, g2, w0, a0, k_k, k_a, r_k, gn_g, gn_b):
    bsz, seq, n_proj = proj.shape
    d = w0.shape[-1]
    lw, la, lg = w2.shape[0], a2.shape[0], g2.shape[0]
    n_tail = lw + la + lg
    assert n_proj == 3 * d + n_tail and (3 * d) % n_tail == 0
    gw = R_GROUP * R_N
    width = R_PAIRS * gw
    nb = d // width
    tb = min(R_CHUNK * R_CHUNKS_PER_STEP, seq)
    nc = seq // tb
    w2p = jnp.concatenate([w2, jnp.zeros_like(a2)], axis=0).astype(BF16)
    a2p = jnp.concatenate([jnp.zeros_like(w2), a2], axis=0).astype(BF16)
    tile = lambda off: pl.BlockSpec((1, tb, width), lambda b, g, c: (b, c, off * nb + g))
    tail = pl.BlockSpec((1, tb, n_tail), lambda b, g, c: (b, c, 3 * d // n_tail))
    wa = pl.BlockSpec((lw + la, width), lambda b, g, c: (0, g))
    wg = pl.BlockSpec((lg, width), lambda b, g, c: (0, g))
    par = pl.BlockSpec((1, width), lambda b, g, c: (0, g))
    row = lambda p: p.reshape(1, d)
    return pl.pallas_call(
        functools.partial(_wkv_body, n_tanh=lw),
        grid=(bsz, nb, nc),
        in_specs=[tile(0), tile(1), tile(2), tail, wa, wa, wg] + [par] * 7,
        out_specs=pl.BlockSpec((1, tb, width), lambda b, g, c: (b, c, g)),
        out_shape=jax.ShapeDtypeStruct((bsz, seq, d), BF16),
        scratch_shapes=[pltpu.VMEM((R_PAIRS, gw, gw), F32)],
        compiler_params=_cparams(("parallel", "parallel", "arbitrary")),
        name="wkv7_chunked",
    )(proj, proj, proj, proj, w2p, a2p, g2.astype(BF16),
      row(w0), row(a0), row(k_k), row(k_a), row(r_k), row(gn_g), row(gn_b))


def _rwkv_mixer(x, w_in, mu, w0, w2, a0, a2, g2, k_k, k_a, r_k, gn_g, gn_b):
    bsz, seq, d = x.shape
    n_proj = w_in.shape[1]
    proj = _proj_shift_lerp(x.reshape(bsz * seq, d), w_in.astype(BF16), mu, seq=seq, tm=512, col_chunk=n_proj // 2)
    return _wkv_core(proj.reshape(bsz, seq, n_proj), w2, a2, g2, w0, a0, k_k, k_a, r_k, gn_g, gn_b)


def kernel(x, mlstm_w_in, mlstm_b_i, mlstm_b_f, mlstm_conv_w, mlstm_conv_b, mlstm_norm_g, mlstm_w_out, rwkv_w_in, rwkv_mu, rwkv_w0, rwkv_w2, rwkv_a0, rwkv_a2, rwkv_g2, rwkv_k_k, rwkv_k_a, rwkv_r_k, rwkv_gn_g, rwkv_gn_b, rwkv_w_out, ln_mix_g, ln_mix_b, mlp_w1, mlp_w2, ln_ffn_g, ln_ffn_b):
    bsz, seq, d = x.shape
    m = bsz * seq
    xf = x.reshape(m, d)
    for layer in range(DEPTH):
        j = layer // 2
        if layer % 2 == 0:
            mix = _mlstm_mixer(xf.reshape(bsz, seq, d), mlstm_w_in[j], mlstm_b_i[j], mlstm_b_f[j],
                               mlstm_conv_w[j], mlstm_conv_b[j], mlstm_norm_g[j])
            w_out = mlstm_w_out[j]
        else:
            mix = _rwkv_mixer(xf.reshape(bsz, seq, d), rwkv_w_in[j], rwkv_mu[j], rwkv_w0[j], rwkv_w2[j],
                              rwkv_a0[j], rwkv_a2[j], rwkv_g2[j], rwkv_k_k[j], rwkv_k_a[j],
                              rwkv_r_k[j].reshape(-1), rwkv_gn_g[j], rwkv_gn_b[j])
            w_out = rwkv_w_out[j]
        xf = _mix_ffn(mix.reshape(m, -1), w_out.astype(BF16), xf, ln_mix_g[layer], ln_mix_b[layer],
                      mlp_w1[layer].astype(BF16), mlp_w2[layer].astype(BF16), ln_ffn_g[layer], ln_ffn_b[layer],
                      tm=512, ff_chunk=1024, row_split=2)
    return xf.reshape(bsz, seq, d)
```

```python
import functools
import math

import jax
import jax.numpy as jnp
from jax import lax
from jax.experimental import pallas as pl
from jax.experimental.pallas import tpu as pltpu

F32 = jnp.float32
BF16 = jnp.bfloat16
HIGHEST = lax.Precision.HIGHEST

DEPTH = 2
DN_ALPHA = (2.0 * DEPTH) ** 0.25
LN_EPS = 1e-5

M_HEADS = 4
M_DK = 128
M_DV = 256
M_GATE_CAP = 15.0
M_CHUNK = 256

R_N = 64
R_GN_EPS = 64e-5
R_DECAY_SCALE = math.exp(-0.5)
R_CHUNK = 64
R_GROUP = 2
R_PAIRS = 8
R_CHUNKS_PER_STEP = 4

VMEM_LIMIT = 48 * 1024 * 1024


def _cparams(sem):
    return pltpu.CompilerParams(dimension_semantics=sem, vmem_limit_bytes=VMEM_LIMIT)


def _dot(a, b, precision=None):
    return jnp.dot(a, b, preferred_element_type=F32, precision=precision)


def _dot_nt(a, b, precision=None):
    return lax.dot_general(a, b, (((1,), (1,)), ((), ())), preferred_element_type=F32, precision=precision)


def _dot_tn(a, b, precision=None):
    return lax.dot_general(a, b, (((0,), (0,)), ((), ())), preferred_element_type=F32, precision=precision)


def _proj_lerp_body(x_ref, w_ref, mu_ref, o_ref, carry_ref, *, tiles_per_seq, col_chunk):
    @pl.when(pl.program_id(0) % tiles_per_seq == 0)
    def _():
        carry_ref[...] = jnp.zeros_like(carry_ref)

    xb = x_ref[...].astype(BF16)
    tm = xb.shape[0]
    first = lax.broadcasted_iota(jnp.int32, (8, col_chunk), 0) == 0

    def epilogue(acc, cols):
        rolled = pltpu.roll(acc, 1, 0)
        top = jnp.where(first, carry_ref[0:1, cols], rolled[0:8])
        shifted = jnp.concatenate([top, rolled[8:]], axis=0)
        o_ref[:, cols] = (acc + mu_ref[:, cols] * (shifted - acc)).astype(o_ref.dtype)
        carry_ref[:, cols] = jnp.broadcast_to(acc[tm - 1:tm, :], (8, col_chunk))

    chunks = [slice(c0, c0 + col_chunk) for c0 in range(0, w_ref.shape[1], col_chunk)]
    pending = None
    for cols in chunks:
        acc = _dot(xb, w_ref[:, cols])
        if pending is not None:
            epilogue(*pending)
        pending = (acc, cols)
    epilogue(*pending)


def _proj_shift_lerp(x, w, mu, *, seq, tm, col_chunk):
    m, k = x.shape
    n = w.shape[1]
    tm = min(tm, seq)
    assert seq % tm == 0 and n % col_chunk == 0
    return pl.pallas_call(
        functools.partial(_proj_lerp_body, tiles_per_seq=seq // tm, col_chunk=col_chunk),
        grid=(m // tm,),
        in_specs=[pl.BlockSpec((tm, k), lambda i: (i, 0)),
                  pl.BlockSpec((k, n), lambda i: (0, 0), pipeline_mode=pl.Buffered(1)),
                  pl.BlockSpec((1, n), lambda i: (0, 0), pipeline_mode=pl.Buffered(1))],
        out_specs=pl.BlockSpec((tm, n), lambda i: (i, 0)),
        out_shape=jax.ShapeDtypeStruct((m, n), F32),
        scratch_shapes=[pltpu.VMEM((8, n), F32)],
        compiler_params=_cparams(("arbitrary",)),
        name="proj_token_shift",
    )(x, w, mu.reshape(1, n))


def _proj_conv_body(x_ref, w_ref, wg_ref, cw_ref, cb_ref, o_ref, g_ref, carry_ref, *, tiles_per_seq, col_chunk):
    @pl.when(pl.program_id(0) % tiles_per_seq == 0)
    def _():
        carry_ref[...] = jnp.zeros_like(carry_ref)

    xb = x_ref[...].astype(BF16)
    tm = xb.shape[0]
    taps, n_conv = cw_ref.shape

    def epilogue(acc, cols):
        if cols.start < n_conv:
            ext = jnp.concatenate([carry_ref[:, cols], acc], axis=0)
            conv = cw_ref[0:1, cols] * ext
            for tap in range(1, taps):
                conv = pltpu.roll(conv, 1, 0) + cw_ref[tap:tap + 1, cols] * ext
            conv = conv[8:] + cb_ref[:, cols]
            o_ref[:, cols] = (conv * jax.nn.sigmoid(conv)).astype(o_ref.dtype)
            carry_ref[:, cols] = acc[tm - 8:tm]
        else:
            o_ref[:, cols] = acc.astype(o_ref.dtype)

    chunks = [slice(c0, c0 + col_chunk) for c0 in range(0, w_ref.shape[1], col_chunk)]
    pending = None
    for cols in chunks:
        acc = _dot(xb, w_ref[:, cols])
        if pending is not None:
            epilogue(*pending)
        pending = (acc, cols)
    g_ref[...] = _dot(xb, wg_ref[...])
    epilogue(*pending)


def _proj_conv_silu(x, w, w_narrow, conv_w, conv_b, *, seq, tm, col_chunk):
    m, k = x.shape
    n = w.shape[1]
    n_narrow = w_narrow.shape[1]
    taps, n_conv = conv_w.shape
    tm = min(tm, seq)
    assert seq % tm == 0 and n % col_chunk == 0 and n_conv % col_chunk == 0 and taps <= 8
    resident = lambda shape: pl.BlockSpec(shape, lambda i: (0, 0), pipeline_mode=pl.Buffered(1))
    return pl.pallas_call(
        functools.partial(_proj_conv_body, tiles_per_seq=seq // tm, col_chunk=col_chunk),
        grid=(m // tm,),
        in_specs=[pl.BlockSpec((tm, k), lambda i: (i, 0)), resident((k, n)), resident((k, n_narrow)),
                  resident((taps, n_conv)), resident((1, n_conv))],
        out_specs=[pl.BlockSpec((tm, n), lambda i: (i, 0)), pl.BlockSpec((tm, n_narrow), lambda i: (i, 0))],
        out_shape=[jax.ShapeDtypeStruct((m, n), BF16), jax.ShapeDtypeStruct((m, n_narrow), F32)],
        scratch_shapes=[pltpu.VMEM((8, n_conv), F32)],
        compiler_params=_cparams(("arbitrary",)),
        name="proj_conv_silu",
    )(x, w, w_narrow, conv_w, conv_b.reshape(1, n_conv))


def _layer_norm(y, g, b):
    mu = jnp.mean(y, axis=-1, keepdims=True)
    yc = y - mu
    var = jnp.mean(jnp.square(yc), axis=-1, keepdims=True)
    return yc * lax.rsqrt(var + LN_EPS) * g + b


def _mix_ffn_body(mix_ref, wo_ref, res_ref, g1_ref, b1_ref, w1_ref, w2_ref, g2_ref, b2_ref, o_ref, *,
                  ff_chunk, row_split):
    sub = mix_ref.shape[0] // row_split

    def stages(rows):
        st = {}

        def out_proj():
            st["y"] = DN_ALPHA * res_ref[rows, :] + _dot(mix_ref[rows, :], wo_ref[...])

        def norm1():
            st["x1"] = _layer_norm(st.pop("y"), g1_ref[...], b1_ref[...])
            st["x1b"] = st["x1"].astype(BF16)
            st["acc"] = DN_ALPHA * st.pop("x1")

        def up(f):
            st["hid"] = _dot(st["x1b"], w1_ref[:, f:f + ff_chunk])

        def act():
            st["hid"] = jnp.square(jnp.maximum(st["hid"], 0.0)).astype(BF16)

        def down(f):
            st["acc"] = st["acc"] + _dot(st.pop("hid"), w2_ref[f:f + ff_chunk, :])

        def norm2():
            o_ref[rows, :] = _layer_norm(st.pop("acc"), g2_ref[...], b2_ref[...])

        seq = [out_proj, norm1]
        for f in range(0, w1_ref.shape[1], ff_chunk):
            seq += [functools.partial(up, f), act, functools.partial(down, f)]
        return seq + [norm2]

    threads = [stages(slice(h * sub, (h + 1) * sub)) for h in range(row_split)]
    for t in range(len(threads[0]) + row_split - 1):
        for h, seq in enumerate(threads):
            if 0 <= t - h < len(seq):
                seq[t - h]()


def _mix_ffn(mix, w_out, res, g1, b1, w1, w2, g2, b2, *, tm, ff_chunk, row_split):
    m, k = mix.shape
    d = w_out.shape[1]
    dff = w1.shape[1]
    tm = min(tm, m)
    assert m % tm == 0 and dff % ff_chunk == 0 and tm % (8 * row_split) == 0
    resident = lambda shape: pl.BlockSpec(shape, lambda i: (0, 0), pipeline_mode=pl.Buffered(1))
    rows = lambda width: pl.BlockSpec((tm, width), lambda i: (i, 0))
    vec = lambda p: p.reshape(1, d)
    return pl.pallas_call(
        functools.partial(_mix_ffn_body, ff_chunk=ff_chunk, row_split=row_split),
        grid=(m // tm,),
        in_specs=[rows(k), resident((k, d)), rows(d), resident((1, d)), resident((1, d)),
                  resident((d, dff)), resident((dff, d)), resident((1, d)), resident((1, d))],
        out_specs=rows(d),
        out_shape=jax.ShapeDtypeStruct((m, d), F32),
        compiler_params=_cparams(("parallel",)),
        name="outproj_ln_ffn_ln",
    )(mix, w_out, res, vec(g1), vec(b1), w1, w2, vec(g2), vec(b2))


def _mlstm_body(q_ref, k_ref, v_ref, o_ref, gate_ref, ng_ref, h_ref, c_ref, n_ref, m_ref):
    lc = q_ref.shape[1]
    heads = range(M_HEADS)

    @pl.when(pl.program_id(1) == 0)
    def _():
        c_ref[...] = jnp.zeros_like(c_ref)
        n_ref[...] = jnp.zeros_like(n_ref)
        m_ref[...] = jnp.zeros_like(m_ref)

    ri = lax.broadcasted_iota(jnp.int32, (lc, lc), 0)
    ci = lax.broadcasted_iota(jnp.int32, (lc, lc), 1)
    causal = ci <= ri
    diag = ci == ri
    f_rows = gate_ref[0, 0, 0]
    i_rows = gate_ref[0, 0, 1]
    bcum_rows = _dot(f_rows, jnp.where(ri <= ci, 1.0, 0.0).astype(F32), HIGHEST)
    scale = M_DK ** -0.5
    q_all = q_ref[0]
    k_all = k_ref[0]
    v_all = v_ref[0]
    qb = [q_all[:, h * M_DK:(h + 1) * M_DK] for h in heads]
    kb = [k_all[:, h * M_DK:(h + 1) * M_DK] for h in heads]
    vb = [v_all[:, h * M_DV:(h + 1) * M_DV] for h in heads]
    q = [qb[h].astype(F32) for h in heads]
    k = [kb[h].astype(F32) for h in heads]
    f_row = [f_rows[h:h + 1, :] for h in heads]
    i_row = [i_rows[h:h + 1, :] for h in heads]
    bcum_row = [bcum_rows[h:h + 1, :] for h in heads]
    bcum_col = [jnp.sum(jnp.where(causal, f_row[h], 0.0), axis=-1, keepdims=True) for h in heads]
    i_col = [jnp.sum(jnp.where(diag, i_row[h], 0.0), axis=-1, keepdims=True) for h in heads]
    m_prev = [m_ref[h, 0:1, 0:1] for h in heads]
    c_prev = [c_ref[h] for h in heads]
    n_prev = [n_ref[h, 0:1, :] for h in heads]

    log_d = [jnp.where(causal, bcum_col[h] - bcum_row[h] + i_row[h], -jnp.inf) for h in heads]
    log_inter = [bcum_col[h] + m_prev[h] for h in heads]
    m_row = [jnp.maximum(jnp.max(log_d[h], axis=-1, keepdims=True), log_inter[h]) for h in heads]
    inter = [jnp.exp(log_inter[h] - m_row[h]) for h in heads]
    s = [_dot_nt(qb[h], kb[h]) * (scale * jnp.exp(log_d[h] - m_row[h])) for h in heads]
    inter_s = [scale * inter[h] for h in heads]
    qc = [_dot(qb[h], c_prev[h].astype(BF16)) for h in heads]
    num = [_dot(s[h].astype(BF16), vb[h]) + inter_s[h] * qc[h] for h in heads]
    den = [jnp.sum(s[h], axis=-1, keepdims=True) + inter_s[h] * jnp.sum(q[h] * n_prev[h], axis=-1, keepdims=True)
           for h in heads]
    hc = [num[h] / jnp.maximum(jnp.abs(den[h]), jnp.exp(-m_row[h])) for h in heads]

    b_last = [bcum_col[h][lc - 1:lc, :] for h in heads]
    log_w = [b_last[h] - bcum_col[h] + i_col[h] for h in heads]
    m_new = [jnp.maximum(b_last[h] + m_prev[h], jnp.max(log_w[h], axis=0, keepdims=True)) for h in heads]
    decay = [jnp.exp(b_last[h] + m_prev[h] - m_new[h]) for h in heads]
    kw = [k[h] * jnp.exp(log_w[h] - m_new[h]) for h in heads]
    for h in heads:
        c_ref[h] = decay[h] * c_prev[h] + _dot_tn(kw[h].astype(BF16), vb[h])
        n_ref[h] = jnp.broadcast_to(decay[h] * n_prev[h] + jnp.sum(kw[h], axis=0, keepdims=True), n_ref.shape[1:])
        m_ref[h] = jnp.broadcast_to(m_new[h], m_ref.shape[1:])

    hn = jnp.concatenate([hc[h] * lax.rsqrt(jnp.mean(jnp.square(hc[h]), axis=-1, keepdims=True) + 1e-6)
                          for h in heads], axis=1)
    h_ref[0] = (hn * ng_ref[...] * jax.nn.sigmoid(o_ref[0].astype(F32))).astype(h_ref.dtype)


def _mlstm_core(proj, gates, norm_g):
    bsz, seq, _ = proj.shape
    lc = gates.shape[-1]
    nc = seq // lc
    hdk = M_HEADS * M_DK
    hdv = M_HEADS * M_DV
    assert 2 * hdk == hdv
    return pl.pallas_call(
        _mlstm_body,
        grid=(bsz, nc),
        in_specs=[pl.BlockSpec((1, lc, hdk), lambda b, c: (b, c, 0)),
                  pl.BlockSpec((1, lc, hdk), lambda b, c: (b, c, 1)),
                  pl.BlockSpec((1, lc, hdv), lambda b, c: (b, c, 1)),
                  pl.BlockSpec((1, lc, hdv), lambda b, c: (b, c, 2)),
                  pl.BlockSpec((1, 1, 2, 8, lc), lambda b, c: (b, c, 0, 0, 0)),
                  pl.BlockSpec((1, hdv), lambda b, c: (0, 0))],
        out_specs=pl.BlockSpec((1, lc, hdv), lambda b, c: (b, c, 0)),
        out_shape=jax.ShapeDtypeStruct((bsz, seq, hdv), BF16),
        scratch_shapes=[pltpu.VMEM((M_HEADS, M_DK, M_DV), F32),
                        pltpu.VMEM((M_HEADS, 8, M_DK), F32),
                        pltpu.VMEM((M_HEADS, 8, 128), F32)],
        compiler_params=_cparams(("parallel", "arbitrary")),
        name="mlstm_chunkwise",
    )(proj, proj, proj, proj, gates, norm_g.reshape(1, hdv))


def _softplus(z):
    return jnp.maximum(z, 0.0) + jnp.log1p(jnp.exp(-jnp.abs(z)))


def _mlstm_mixer(x, w_in, b_i, b_f, conv_w, conv_b, norm_g):
    bsz, seq, d = x.shape
    m = bsz * seq
    hdv = M_HEADS * M_DV
    n_main = 2 * M_HEADS * M_DK + 2 * hdv
    xf = x.reshape(m, d)
    w_gate = jnp.pad(w_in[:, n_main:], ((0, 0), (0, 128 - 2 * M_HEADS))).astype(BF16)
    proj, gate_pre = _proj_conv_silu(xf, w_in[:, :n_main].astype(BF16), w_gate, conv_w, conv_b,
                                     seq=seq, tm=512, col_chunk=256)
    gate_pre = gate_pre[:, :2 * M_HEADS]
    proj = proj.reshape(bsz, seq, n_main)
    gate_pre = gate_pre.reshape(bsz, seq, 2 * M_HEADS)
    log_i = M_GATE_CAP * jnp.tanh((gate_pre[..., :M_HEADS] + b_i) / M_GATE_CAP)
    log_f = -_softplus(-(M_GATE_CAP * jnp.tanh((gate_pre[..., M_HEADS:] + b_f) / M_GATE_CAP)))
    lc = min(M_CHUNK, seq)
    gates = jnp.stack([log_f, log_i], axis=0)
    gates = gates.reshape(2, bsz, seq // lc, lc, M_HEADS).transpose(1, 2, 0, 4, 3)
    gates = jnp.pad(gates, ((0, 0), (0, 0), (0, 0), (0, 8 - M_HEADS), (0, 0)))
    return _mlstm_core(proj, gates, norm_g)


def _split_bf16(x):
    hi = x.astype(BF16)
    return hi, (x - hi.astype(F32)).astype(BF16)


def _wkv_body(r_ref, k_ref, v_ref, tail_ref, w2_ref, a2_ref, g2_ref, w0_ref, a0_ref, kk_ref, ka_ref, rk_ref,
              gng_ref, gnb_ref, o_ref, s_ref, *, n_tanh):
    tb = r_ref.shape[1]
    lc = min(R_CHUNK, tb)
    chunks = range(tb // lc)
    width = r_ref.shape[2]
    gw = R_GROUP * R_N
    rows = R_GROUP * lc
    assert rows == gw and tb % lc == 0

    @pl.when(pl.program_id(2) == 0)
    def _():
        s_ref[...] = jnp.zeros_like(s_ref)

    pairs = range(width // gw)
    sls = [slice(p * gw, (p + 1) * gw) for p in pairs]
    brow = lax.broadcasted_iota(jnp.int32, (gw, gw), 0) // R_N
    bcol = lax.broadcasted_iota(jnp.int32, (gw, gw), 1) // R_N
    same_head = brow == bcol
    head_ones = jnp.where(same_head, 1.0, 0.0).astype(BF16)

    def seg_sum(x, parts=1):
        pieces = (x.astype(BF16),) if parts == 1 else _split_bf16(x)
        ones = jnp.concatenate([head_ones] * len(pieces), axis=0)
        return jnp.concatenate([_dot(jnp.concatenate([piece[:, sl] for piece in pieces], axis=1), ones)
                                for sl in sls], axis=1)

    r = r_ref[0].astype(F32)
    k = k_ref[0].astype(F32)
    v = v_ref[0].astype(F32)
    n_wa = w2_ref.shape[0]
    tail = tail_ref[0].astype(F32)
    t_wa = tail[:, :n_wa]
    is_tanh = lax.broadcasted_iota(jnp.int32, (1, n_wa), 1) < n_tanh
    f_wa = jnp.where(is_tanh, jnp.tanh(t_wa), t_wa).astype(BF16)
    gate = _dot(jax.nn.sigmoid(tail[:, n_wa:]).astype(BF16), g2_ref[...])
    ld = (-R_DECAY_SCALE) * jax.nn.sigmoid(w0_ref[...] + _dot(f_wa, w2_ref[...]))
    a_lr = jax.nn.sigmoid(a0_ref[...] + _dot(f_wa, a2_ref[...]))
    kk = k * kk_ref[...]
    kk = kk / jnp.maximum(jnp.sqrt(seg_sum(jnp.square(kk))), 1e-12)
    k = k * (1.0 + (a_lr - 1.0) * ka_ref[...])
    b_ = kk * a_lr

    ti = lax.broadcasted_iota(jnp.int32, (tb, tb), 0)
    tj = lax.broadcasted_iota(jnp.int32, (tb, tb), 1)
    in_chunk_tri = jnp.logical_and(tj <= ti, ti // lc == tj // lc)
    tri = jnp.where(in_chunk_tri, 1.0, 0.0).astype(BF16)
    cum = _dot(jnp.concatenate([tri, tri], axis=1), jnp.concatenate(_split_bf16(ld), axis=0))
    e_pos = jnp.exp(cum)
    e_neg = jnp.exp(-cum)
    w_end = [jnp.exp(cum[(c + 1) * lc - 1:(c + 1) * lc, :]) for c in chunks]
    w_end_rows = jnp.concatenate([jnp.broadcast_to(w_end[c], (lc, width)) for c in chunks], axis=0)
    a_til = -kk * jnp.exp(cum - ld)
    r_til = r * e_pos
    b_til = b_ * e_neg
    k_til = k * e_neg
    b_hat = b_til * w_end_rows
    k_hat = k_til * w_end_rows

    lane_head = lax.broadcasted_iota(jnp.int32, (lc, gw), 1) // R_N
    head_sel = [lane_head == h for h in range(R_GROUP)]

    def stack(x):
        zero = jnp.zeros_like(x)
        return jnp.concatenate([jnp.where(sel, x, zero) for sel in head_sel], axis=0)

    trow = lax.broadcasted_iota(jnp.int32, (lc, gw), 0)
    tcol = lax.broadcasted_iota(jnp.int32, (lc, gw), 1) % lc
    strict = tcol < trow
    causal = tcol <= trow
    eye = jnp.where(tcol == trow, 1.0, 0.0).astype(F32)
    level_masks = []
    s = 1
    while s < lc:
        lo, hi = s.bit_length() - 1, s.bit_length()
        level_masks.append(jnp.logical_and((trow >> hi) == (tcol >> hi), (trow >> lo) != (tcol >> lo)))
        s *= 2

    units = [(c, p) for c in chunks for p in pairs]
    tile = lambda x, c, p: x[c * lc:(c + 1) * lc, sls[p]]
    ar_b = {u: jnp.concatenate([tile(a_til, *u), tile(r_til, *u)], axis=0).astype(BF16) for u in units}
    bk_s = {u: jnp.concatenate([stack(tile(b_til, *u).astype(BF16)), stack(tile(k_til, *u).astype(BF16))], axis=0)
            for u in units}
    v_b = {u: tile(v, *u).astype(BF16) for u in units}
    v_s = {u: stack(v_b[u]) for u in units}
    bk_hat = {u: jnp.concatenate([tile(b_hat, *u), tile(k_hat, *u)], axis=0).astype(BF16) for u in units}

    pm = {u: _dot_nt(ar_b[u], bk_s[u]) for u in units}
    n_ab = {u: jnp.where(strict, pm[u][:lc, :rows], 0.0) for u in units}
    m_xk = {u: jnp.concatenate([jnp.where(strict, pm[u][:lc, rows:], 0.0),
                                jnp.where(causal, pm[u][lc:, rows:], 0.0)], axis=0).astype(BF16) for u in units}
    m_rb = {u: jnp.where(causal, pm[u][lc:, :rows], 0.0).astype(BF16) for u in units}

    n_b = {u: n_ab[u].astype(BF16) for u in units}
    zero_b = jnp.zeros((lc, gw), BF16)
    t_inv = {u: eye + jnp.where(level_masks[0], n_ab[u], 0.0) for u in units}
    for msk in level_masks[1:]:
        t_b = {u: t_inv[u].astype(BF16) for u in units}
        half = {u: _dot(t_b[u], stack(jnp.where(msk, n_b[u], zero_b))) for u in units}
        t_inv = {u: t_inv[u] + _dot(half[u].astype(BF16), stack(t_b[u])) for u in units}
    t_b = {u: t_inv[u].astype(BF16) for u in units}
    resid = {u: eye - t_b[u].astype(F32) + _dot(n_b[u], stack(t_b[u])) for u in units}
    t_fix = {u: _dot(t_b[u], stack(resid[u].astype(BF16))).astype(BF16) for u in units}

    state = [s_ref[p] for p in pairs]
    y_rows = []
    for c in chunks:
        us = [(c, p) for p in pairs]
        zy = [_dot_nt(ar_b[u], state[u[1]].astype(BF16)) + _dot(m_xk[u], v_s[u]) for u in us]
        ub = []
        for i, u in enumerate(us):
            z_s = stack(zy[i][:lc].astype(BF16))
            ub.append(_dot(jnp.concatenate([t_b[u], t_fix[u]], axis=1), jnp.concatenate([z_s, z_s], axis=0)).astype(BF16))
        y_rows.append(jnp.concatenate([zy[i][lc:] + _dot(m_rb[u], stack(ub[i])) for i, u in enumerate(us)], axis=1))
        for i, u in enumerate(us):
            upd = _dot_tn(jnp.concatenate([ub[i], v_b[u]], axis=0), bk_hat[u])
            state[u[1]] = state[u[1]] * w_end[c][:, sls[u[1]]] + jnp.where(same_head, upd, 0.0)
    for p in pairs:
        s_ref[p] = state[p]
    y = jnp.concatenate(y_rows, axis=0)

    mu = seg_sum(y, parts=2) * (1.0 / R_N)
    yc = y - mu
    var = seg_sum(jnp.square(yc)) * (1.0 / R_N)
    yn = yc * lax.rsqrt(var + R_GN_EPS) * gng_ref[...] + gnb_ref[...]
    bonus = seg_sum(r * k * rk_ref[...]) * v
    o_ref[0] = ((yn + bonus) * gate).astype(o_ref.dtype)


def _wkv_core(proj, w2, a2, g2, w0, a0, k_k, k_a, r_k, gn_g, gn_b):
    bsz, seq, n_proj = proj.shape
    d = w0.shape[-1]
    lw, la, lg = w2.shape[0], a2.shape[0], g2.shape[0]
    n_tail = lw + la + lg
    assert n_proj == 3 * d + n_tail and (3 * d) % n_tail == 0
    gw = R_GROUP * R_N
    width = R_PAIRS * gw
    nb = d // width
    tb = min(R_CHUNK * R_CHUNKS_PER_STEP, seq)
    nc = seq // tb
    w2p = jnp.concatenate([w2, jnp.zeros_like(a2)], axis=0).astype(BF16)
    a2p = jnp.concatenate([jnp.zeros_like(w2), a2], axis=0).astype(BF16)
    tile = lambda off: pl.BlockSpec((1, tb, width), lambda b, g, c: (b, c, off * nb + g))
    tail = pl.BlockSpec((1, tb, n_tail), lambda b, g, c: (b, c, 3 * d // n_tail))
    wa = pl.BlockSpec((lw + la, width), lambda b, g, c: (0, g))
    wg = pl.BlockSpec((lg, width), lambda b, g, c: (0, g))
    par = pl.BlockSpec((1, width), lambda b, g, c: (0, g))
    row = lambda p: p.reshape(1, d)
    return pl.pallas_call(
        functools.partial(_wkv_body, n_tanh=lw),
        grid=(bsz, nb, nc),
        in_specs=[tile(0), tile(1), tile(2), tail, wa, wa, wg] + [par] * 7,
        out_specs=pl.BlockSpec((1, tb, width), lambda b, g, c: (b, c, g)),
        out_shape=jax.ShapeDtypeStruct((bsz, seq, d), BF16),
        scratch_shapes=[pltpu.VMEM((R_PAIRS, gw, gw), F32)],
        compiler_params=_cparams(("parallel", "parallel", "arbitrary")),
        name="wkv7_chunked",
    )(proj, proj, proj, proj, w2p, a2p, g2.astype(BF16),
      row(w0), row(a0), row(k_k), row(k_a), row(r_k), row(gn_g), row(gn_b))


def _rwkv_mixer(x, w_in, mu, w0, w2, a0, a2, g2, k_k, k_a, r_k, gn_g, gn_b):
    bsz, seq, d = x.shape
    n_proj = w_in.shape[1]
    proj = _proj_shift_lerp(x.reshape(bsz * seq, d), w_in.astype(BF16), mu, seq=seq, tm=512, col_chunk=256)
    return _wkv_core(proj.reshape(bsz, seq, n_proj), w2, a2, g2, w0, a0, k_k, k_a, r_k, gn_g, gn_b)


def kernel(x, mlstm_w_in, mlstm_b_i, mlstm_b_f, mlstm_conv_w, mlstm_conv_b, mlstm_norm_g, mlstm_w_out, rwkv_w_in, rwkv_mu, rwkv_w0, rwkv_w2, rwkv_a0, rwkv_a2, rwkv_g2, rwkv_k_k, rwkv_k_a, rwkv_r_k, rwkv_gn_g, rwkv_gn_b, rwkv_w_out, ln_mix_g, ln_mix_b, mlp_w1, mlp_w2, ln_ffn_g, ln_ffn_b):
    bsz, seq, d = x.shape
    m = bsz * seq
    xf = x.reshape(m, d)
    for layer in range(DEPTH):
        j = layer // 2
        if layer % 2 == 0:
            mix = _mlstm_mixer(xf.reshape(bsz, seq, d), mlstm_w_in[j], mlstm_b_i[j], mlstm_b_f[j],
                               mlstm_conv_w[j], mlstm_conv_b[j], mlstm_norm_g[j])
            w_out = mlstm_w_out[j]
        else:
            mix = _rwkv_mixer(xf.reshape(bsz, seq, d), rwkv_w_in[j], rwkv_mu[j], rwkv_w0[j], rwkv_w2[j],
                              rwkv_a0[j], rwkv_a2[j], rwkv_g2[j], rwkv_k_k[j], rwkv_k_a[j],
                              rwkv_r_k[j].reshape(-1), rwkv_gn_g[j], rwkv_gn_b[j])
            w_out = rwkv_w_out[j]
        xf = _mix_ffn(mix.reshape(m, -1), w_out.astype(BF16), xf, ln_mix_g[layer], ln_mix_b[layer],
                      mlp_w1[layer].astype(BF16), mlp_w2[layer].astype(BF16), ln_ffn_g[layer], ln_ffn_b[layer],
                      tm=512, ff_chunk=1024, row_split=2)
    return xf.reshape(bsz, seq, d)
```

```python
import functools
import math

import jax
import jax.numpy as jnp
from jax import lax
from jax.experimental import pallas as pl
from jax.experimental.pallas import tpu as pltpu

F32 = jnp.float32
BF16 = jnp.bfloat16

DEPTH = 2
DN_ALPHA = (2.0 * DEPTH) ** 0.25
LN_EPS = 1e-5

M_HEADS = 4
M_DK = 128
M_DV = 256
M_GATE_CAP = 15.0
M_CHUNK = 256

R_N = 64
R_GN_EPS = 64e-5
R_DECAY_SCALE = math.exp(-0.5)
R_CHUNK = 64
R_GROUP = 2
R_STATE_HEADS = 2
R_GROUPS_PER_STEP = 8
R_CHUNKS_PER_STEP = 4

VMEM_LIMIT = 48 * 1024 * 1024


def _cparams(sem):
    return pltpu.CompilerParams(dimension_semantics=sem, vmem_limit_bytes=VMEM_LIMIT)


def _dot(a, b, precision=None):
    return jnp.dot(a, b, preferred_element_type=F32, precision=precision)


def _dot_nt(a, b, precision=None):
    return lax.dot_general(a, b, (((1,), (1,)), ((), ())), preferred_element_type=F32, precision=precision)


def _dot_tn(a, b, precision=None):
    return lax.dot_general(a, b, (((0,), (0,)), ((), ())), preferred_element_type=F32, precision=precision)


def _softplus(z):
    return jnp.maximum(z, 0.0) + jnp.log1p(jnp.exp(-jnp.abs(z)))


def _split_bf16(x):
    hi = x.astype(BF16)
    return hi, (x - hi.astype(F32)).astype(BF16)


def _split3_bf16(x):
    hi, rest = x.astype(BF16), x
    rest = rest - hi.astype(F32)
    mid = rest.astype(BF16)
    return hi, mid, (rest - mid.astype(F32)).astype(BF16)


def _proj_lerp_body(x_ref, w_ref, mu_ref, o_ref, carry_ref, *, tiles_per_seq, col_chunk):
    @pl.when(pl.program_id(0) % tiles_per_seq == 0)
    def _():
        carry_ref[...] = jnp.zeros_like(carry_ref)

    xb = x_ref[...].astype(BF16)
    tm = xb.shape[0]
    first = lax.broadcasted_iota(jnp.int32, (8, col_chunk), 0) == 0

    def epilogue(acc, cols):
        rolled = pltpu.roll(acc, 1, 0)
        top = jnp.where(first, carry_ref[0:1, cols], rolled[0:8])
        shifted = jnp.concatenate([top, rolled[8:]], axis=0)
        o_ref[:, cols] = (acc + mu_ref[:, cols] * (shifted - acc)).astype(o_ref.dtype)
        carry_ref[:, cols] = jnp.broadcast_to(acc[tm - 1:tm, :], (8, col_chunk))

    chunks = [slice(c0, c0 + col_chunk) for c0 in range(0, w_ref.shape[1], col_chunk)]
    pending = None
    for cols in chunks:
        acc = _dot(xb, w_ref[:, cols])
        if pending is not None:
            epilogue(*pending)
        pending = (acc, cols)
    epilogue(*pending)


def _proj_shift_lerp(x, w, mu, *, seq, tm, col_chunk):
    m, k = x.shape
    n = w.shape[1]
    tm = min(tm, seq)
    assert seq % tm == 0 and n % col_chunk == 0
    return pl.pallas_call(
        functools.partial(_proj_lerp_body, tiles_per_seq=seq // tm, col_chunk=col_chunk),
        grid=(m // tm,),
        in_specs=[pl.BlockSpec((tm, k), lambda i: (i, 0)),
                  pl.BlockSpec((k, n), lambda i: (0, 0), pipeline_mode=pl.Buffered(1)),
                  pl.BlockSpec((1, n), lambda i: (0, 0), pipeline_mode=pl.Buffered(1))],
        out_specs=pl.BlockSpec((tm, n), lambda i: (i, 0)),
        out_shape=jax.ShapeDtypeStruct((m, n), F32),
        scratch_shapes=[pltpu.VMEM((8, n), F32)],
        compiler_params=_cparams(("arbitrary",)),
        name="proj_token_shift",
    )(x, w, mu.reshape(1, n))


def _proj_conv_body(x_ref, w_ref, wg_ref, gb_ref, cw_ref, cb_ref, o_ref, g_ref, carry_ref, *,
                    tiles_per_seq, col_chunk):
    @pl.when(pl.program_id(0) % tiles_per_seq == 0)
    def _():
        carry_ref[...] = jnp.zeros_like(carry_ref)

    xb = x_ref[...].astype(BF16)
    tm = xb.shape[0]
    taps, n_conv = cw_ref.shape

    def epilogue(acc, cols):
        if cols.start < n_conv:
            ext = jnp.concatenate([carry_ref[:, cols], acc], axis=0)
            conv = cw_ref[0:1, cols] * ext
            for tap in range(1, taps):
                conv = pltpu.roll(conv, 1, 0) + cw_ref[tap:tap + 1, cols] * ext
            conv = conv[8:] + cb_ref[:, cols]
            o_ref[:, cols] = (conv * jax.nn.sigmoid(conv)).astype(o_ref.dtype)
            carry_ref[:, cols] = acc[tm - 8:tm]
        else:
            o_ref[:, cols] = acc.astype(o_ref.dtype)

    chunks = [slice(c0, c0 + col_chunk) for c0 in range(0, w_ref.shape[1], col_chunk)]
    pending = None
    for cols in chunks:
        acc = _dot(xb, w_ref[:, cols])
        if pending is not None:
            epilogue(*pending)
        pending = (acc, cols)
    capped = M_GATE_CAP * jnp.tanh((_dot(xb, wg_ref[...]) + gb_ref[...]) * (1.0 / M_GATE_CAP))
    is_input_gate = lax.broadcasted_iota(jnp.int32, capped.shape, 1) < M_HEADS
    log_gate = jnp.where(is_input_gate, capped, -_softplus(-capped))
    g_ref[0] = log_gate.T[0:2 * M_HEADS, :]
    epilogue(*pending)


def _proj_conv_silu(x, w, w_gate, b_gate, conv_w, conv_b, *, seq, tm, col_chunk):
    m, k = x.shape
    n = w.shape[1]
    n_gate = w_gate.shape[1]
    assert 2 * M_HEADS == 8 and n_gate % 128 == 0
    taps, n_conv = conv_w.shape
    tm = min(tm, seq)
    assert seq % tm == 0 and n % col_chunk == 0 and n_conv % col_chunk == 0 and taps <= 8
    resident = lambda shape: pl.BlockSpec(shape, lambda i: (0, 0), pipeline_mode=pl.Buffered(1))
    return pl.pallas_call(
        functools.partial(_proj_conv_body, tiles_per_seq=seq // tm, col_chunk=col_chunk),
        grid=(m // tm,),
        in_specs=[pl.BlockSpec((tm, k), lambda i: (i, 0)), resident((k, n)), resident((k, n_gate)),
                  resident((1, n_gate)),
                  resident((taps, n_conv)), resident((1, n_conv))],
        out_specs=[pl.BlockSpec((tm, n), lambda i: (i, 0)), pl.BlockSpec((1, 2 * M_HEADS, tm), lambda i: (i, 0, 0))],
        out_shape=[jax.ShapeDtypeStruct((m, n), BF16), jax.ShapeDtypeStruct((m // tm, 2 * M_HEADS, tm), F32)],
        scratch_shapes=[pltpu.VMEM((8, n_conv), F32)],
        compiler_params=_cparams(("arbitrary",)),
        name="proj_conv_silu",
    )(x, w, w_gate, b_gate.reshape(1, n_gate), conv_w, conv_b.reshape(1, n_conv))


def _layer_norm(y, g, b):
    mu = jnp.mean(y, axis=-1, keepdims=True)
    yc = y - mu
    var = jnp.mean(jnp.square(yc), axis=-1, keepdims=True)
    return yc * lax.rsqrt(var + LN_EPS) * g + b


def _mix_ffn_body(mix_ref, wo_ref, res_ref, g1_ref, b1_ref, w1_ref, w2_ref, g2_ref, b2_ref, o_ref, *,
                  ff_chunk, row_split):
    sub = mix_ref.shape[0] // row_split

    def stages(rows):
        st = {}

        def out_proj():
            st["y"] = DN_ALPHA * res_ref[rows, :] + _dot(mix_ref[rows, :], wo_ref[...])

        def norm1():
            st["x1"] = _layer_norm(st.pop("y"), g1_ref[...], b1_ref[...])
            st["x1b"] = st["x1"].astype(BF16)
            st["acc"] = DN_ALPHA * st.pop("x1")

        def up(f):
            st["hid"] = _dot(st["x1b"], w1_ref[:, f:f + ff_chunk])

        def act():
            st["hid"] = jnp.square(jnp.maximum(st["hid"], 0.0)).astype(BF16)

        def down(f):
            st["acc"] = st["acc"] + _dot(st.pop("hid"), w2_ref[f:f + ff_chunk, :])

        def norm2():
            o_ref[rows, :] = _layer_norm(st.pop("acc"), g2_ref[...], b2_ref[...])

        seq = [out_proj, norm1]
        for f in range(0, w1_ref.shape[1], ff_chunk):
            seq += [functools.partial(up, f), act, functools.partial(down, f)]
        return seq + [norm2]

    threads = [stages(slice(h * sub, (h + 1) * sub)) for h in range(row_split)]
    for t in range(len(threads[0]) + row_split - 1):
        for h, seq in enumerate(threads):
            if 0 <= t - h < len(seq):
                seq[t - h]()


def _mix_ffn(mix, w_out, res, g1, b1, w1, w2, g2, b2, *, tm, ff_chunk, row_split):
    m, k = mix.shape
    d = w_out.shape[1]
    dff = w1.shape[1]
    tm = min(tm, m)
    assert m % tm == 0 and dff % ff_chunk == 0 and tm % (8 * row_split) == 0
    resident = lambda shape: pl.BlockSpec(shape, lambda i: (0, 0), pipeline_mode=pl.Buffered(1))
    rows = lambda width: pl.BlockSpec((tm, width), lambda i: (i, 0))
    vec = lambda p: p.reshape(1, d)
    return pl.pallas_call(
        functools.partial(_mix_ffn_body, ff_chunk=ff_chunk, row_split=row_split),
        grid=(m // tm,),
        in_specs=[rows(k), resident((k, d)), rows(d), resident((1, d)), resident((1, d)),
                  resident((d, dff)), resident((dff, d)), resident((1, d)), resident((1, d))],
        out_specs=rows(d),
        out_shape=jax.ShapeDtypeStruct((m, d), F32),
        compiler_params=_cparams(("parallel",)),
        name="outproj_ln_ffn_ln",
    )(mix, w_out, res, vec(g1), vec(b1), w1, w2, vec(g2), vec(b2))


def _mlstm_body(q_ref, k_ref, v_ref, o_ref, gate_ref, ng_ref, h_ref, c_ref, n_ref, m_ref):
    lc = q_ref.shape[1]
    heads = range(M_HEADS)

    @pl.when(pl.program_id(1) == 0)
    def _():
        c_ref[...] = jnp.zeros_like(c_ref)
        n_ref[...] = jnp.zeros_like(n_ref)
        m_ref[...] = jnp.zeros_like(m_ref)

    ri = lax.broadcasted_iota(jnp.int32, (lc, lc), 0)
    ci = lax.broadcasted_iota(jnp.int32, (lc, lc), 1)
    causal = ci <= ri
    diag = ci == ri
    g_rows = gate_ref[0]
    tri_u = jnp.where(ri <= ci, 1.0, 0.0).astype(BF16)
    cum_rows = _dot(jnp.concatenate(_split3_bf16(g_rows), axis=1), jnp.concatenate([tri_u] * 3, axis=0))
    scale = M_DK ** -0.5
    q_all = q_ref[0]
    k_all = k_ref[0]
    v_all = v_ref[0]
    qb = [q_all[:, h * M_DK:(h + 1) * M_DK] for h in heads]
    kb = [k_all[:, h * M_DK:(h + 1) * M_DK] for h in heads]
    vb = [v_all[:, h * M_DV:(h + 1) * M_DV] for h in heads]
    q = [qb[h].astype(F32) for h in heads]
    k = [kb[h].astype(F32) for h in heads]
    i_row = [g_rows[h:h + 1, :] for h in heads]
    f_row = [g_rows[M_HEADS + h:M_HEADS + h + 1, :] for h in heads]
    bcum_row = [cum_rows[M_HEADS + h:M_HEADS + h + 1, :] for h in heads]
    bcum_col = [jnp.sum(jnp.where(causal, f_row[h], 0.0), axis=-1, keepdims=True) for h in heads]
    i_col = [jnp.sum(jnp.where(diag, i_row[h], 0.0), axis=-1, keepdims=True) for h in heads]
    m_prev = [m_ref[h, 0:1, 0:1] for h in heads]
    c_prev = [c_ref[h] for h in heads]
    n_prev = [n_ref[h, 0:1, :] for h in heads]

    log_d = [jnp.where(causal, bcum_col[h] - bcum_row[h] + i_row[h], -jnp.inf) for h in heads]
    log_inter = [bcum_col[h] + m_prev[h] for h in heads]
    m_row = [jnp.maximum(jnp.max(log_d[h], axis=-1, keepdims=True), log_inter[h]) for h in heads]
    inter = [jnp.exp(log_inter[h] - m_row[h]) for h in heads]
    s = [_dot_nt(qb[h], kb[h]) * (scale * jnp.exp(log_d[h] - m_row[h])) for h in heads]
    inter_s = [scale * inter[h] for h in heads]
    qc = [_dot(qb[h], c_prev[h].astype(BF16)) for h in heads]
    num = [_dot(s[h].astype(BF16), vb[h]) + inter_s[h] * qc[h] for h in heads]
    den = [jnp.sum(s[h], axis=-1, keepdims=True) + inter_s[h] * jnp.sum(q[h] * n_prev[h], axis=-1, keepdims=True)
           for h in heads]
    hc = [num[h] / jnp.maximum(jnp.abs(den[h]), jnp.exp(-m_row[h])) for h in heads]

    b_last = [bcum_col[h][lc - 1:lc, :] for h in heads]
    log_w = [b_last[h] - bcum_col[h] + i_col[h] for h in heads]
    m_new = [jnp.maximum(b_last[h] + m_prev[h], jnp.max(log_w[h], axis=0, keepdims=True)) for h in heads]
    decay = [jnp.exp(b_last[h] + m_prev[h] - m_new[h]) for h in heads]
    kw = [k[h] * jnp.exp(log_w[h] - m_new[h]) for h in heads]
    for h in heads:
        c_ref[h] = decay[h] * c_prev[h] + _dot_tn(kw[h].astype(BF16), vb[h])
        n_ref[h] = jnp.broadcast_to(decay[h] * n_prev[h] + jnp.sum(kw[h], axis=0, keepdims=True), n_ref.shape[1:])
        m_ref[h] = jnp.broadcast_to(m_new[h], m_ref.shape[1:])

    hn = jnp.concatenate([hc[h] * lax.rsqrt(jnp.mean(jnp.square(hc[h]), axis=-1, keepdims=True) + 1e-6)
                          for h in heads], axis=1)
    h_ref[0] = (hn * ng_ref[...] * jax.nn.sigmoid(o_ref[0].astype(F32))).astype(h_ref.dtype)


def _mlstm_core(proj, gates, norm_g):
    bsz, seq, _ = proj.shape
    tg = gates.shape[-1]
    lc = min(M_CHUNK, seq)
    nc = seq // lc
    assert tg % lc == 0 and seq % tg == 0
    per_tile = tg // lc
    hdk = M_HEADS * M_DK
    hdv = M_HEADS * M_DV
    assert 2 * hdk == hdv
    return pl.pallas_call(
        _mlstm_body,
        grid=(bsz, nc),
        in_specs=[pl.BlockSpec((1, lc, hdk), lambda b, c: (b, c, 0)),
                  pl.BlockSpec((1, lc, hdk), lambda b, c: (b, c, 1)),
                  pl.BlockSpec((1, lc, hdv), lambda b, c: (b, c, 1)),
                  pl.BlockSpec((1, lc, hdv), lambda b, c: (b, c, 2)),
                  pl.BlockSpec((1, 2 * M_HEADS, lc), lambda b, c: (b * (seq // tg) + c // per_tile, 0, c % per_tile)),
                  pl.BlockSpec((1, hdv), lambda b, c: (0, 0))],
        out_specs=pl.BlockSpec((1, lc, hdv), lambda b, c: (b, c, 0)),
        out_shape=jax.ShapeDtypeStruct((bsz, seq, hdv), BF16),
        scratch_shapes=[pltpu.VMEM((M_HEADS, M_DK, M_DV), F32),
                        pltpu.VMEM((M_HEADS, 8, M_DK), F32),
                        pltpu.VMEM((M_HEADS, 8, 128), F32)],
        compiler_params=_cparams(("parallel", "arbitrary")),
        name="mlstm_chunkwise",
    )(proj, proj, proj, proj, gates, norm_g.reshape(1, hdv))


def _mlstm_mixer(x, w_in, b_i, b_f, conv_w, conv_b, norm_g):
    bsz, seq, d = x.shape
    m = bsz * seq
    hdv = M_HEADS * M_DV
    n_main = 2 * M_HEADS * M_DK + 2 * hdv
    xf = x.reshape(m, d)
    pad = 128 - 2 * M_HEADS
    w_gate = jnp.pad(w_in[:, n_main:], ((0, 0), (0, pad))).astype(BF16)
    b_gate = jnp.pad(jnp.concatenate([b_i, b_f]), (0, pad))
    proj, gates = _proj_conv_silu(xf, w_in[:, :n_main].astype(BF16), w_gate, b_gate, conv_w, conv_b,
                                  seq=seq, tm=512, col_chunk=256)
    proj = proj.reshape(bsz, seq, n_main)
    return _mlstm_core(proj, gates, norm_g)


def _wkv_body(r_ref, k_ref, v_ref, tail_ref, w2_ref, a2_ref, g2_ref, w0_ref, a0_ref, kk_ref, ka_ref, rk_ref,
              gng_ref, gnb_ref, o_ref, s_ref, *, n_tanh):
    tb = r_ref.shape[1]
    lc = min(R_CHUNK, tb)
    chunks = range(tb // lc)
    width = r_ref.shape[2]
    gw = R_GROUP * R_N
    sw = R_STATE_HEADS * R_N
    rows = R_GROUP * lc
    assert tb % lc == 0 and gw % sw == 0 and width % gw == 0

    @pl.when(pl.program_id(2) == 0)
    def _():
        s_ref[...] = jnp.zeros_like(s_ref)

    groups = range(width // gw)
    tiles = range(width // sw)
    tls = [slice(j * sw, (j + 1) * sw) for j in tiles]
    brow = lax.broadcasted_iota(jnp.int32, (sw, sw), 0) // R_N
    bcol = lax.broadcasted_iota(jnp.int32, (sw, sw), 1) // R_N
    same_head = brow == bcol
    head_ones = jnp.where(same_head, 1.0, 0.0).astype(BF16)

    def seg_sum(x, parts=1):
        pieces = (x.astype(BF16),) if parts == 1 else _split_bf16(x)
        ones = jnp.concatenate([head_ones] * len(pieces), axis=0)
        return jnp.concatenate([_dot(jnp.concatenate([piece[:, sl] for piece in pieces], axis=1), ones)
                                for sl in tls], axis=1)

    r = r_ref[0].astype(F32)
    k = k_ref[0].astype(F32)
    v = v_ref[0].astype(F32)
    n_wa = w2_ref.shape[0]
    tail = tail_ref[0].astype(F32)
    t_wa = tail[:, :n_wa]
    is_tanh = lax.broadcasted_iota(jnp.int32, (1, n_wa), 1) < n_tanh
    f_wa = jnp.where(is_tanh, jnp.tanh(t_wa), t_wa).astype(BF16)
    gate = _dot(jax.nn.sigmoid(tail[:, n_wa:]).astype(BF16), g2_ref[...])
    ld = (-R_DECAY_SCALE) * jax.nn.sigmoid(w0_ref[...] + _dot(f_wa, w2_ref[...]))
    a_lr = jax.nn.sigmoid(a0_ref[...] + _dot(f_wa, a2_ref[...]))
    kk = k * kk_ref[...]
    kk = kk / jnp.maximum(jnp.sqrt(seg_sum(jnp.square(kk))), 1e-12)
    k = k * (1.0 + (a_lr - 1.0) * ka_ref[...])
    b_ = kk * a_lr

    ti = lax.broadcasted_iota(jnp.int32, (tb, tb), 0)
    tj = lax.broadcasted_iota(jnp.int32, (tb, tb), 1)
    in_chunk_tri = jnp.logical_and(tj <= ti, ti // lc == tj // lc)
    tri = jnp.where(in_chunk_tri, 1.0, 0.0).astype(BF16)
    cum = _dot(jnp.concatenate([tri, tri], axis=1), jnp.concatenate(_split_bf16(ld), axis=0))
    e_pos = jnp.exp(cum)
    e_neg = jnp.exp(-cum)
    w_end = [jnp.exp(cum[(c + 1) * lc - 1:(c + 1) * lc, :]) for c in chunks]
    w_end_rows = jnp.concatenate([jnp.broadcast_to(w_end[c], (lc, width)) for c in chunks], axis=0)
    a_til = -kk * jnp.exp(cum - ld)
    r_til = r * e_pos
    b_til = b_ * e_neg
    k_til = k * e_neg
    b_hat = b_til * w_end_rows
    k_hat = k_til * w_end_rows

    def stacker(n):
        lane_head = lax.broadcasted_iota(jnp.int32, (lc, n), 1) // (n // R_GROUP)
        sel = [lane_head == h for h in range(R_GROUP)]
        return lambda x: jnp.concatenate([jnp.where(s_, x, jnp.zeros_like(x)) for s_ in sel], axis=0)

    stack_c = stacker(gw)
    stack_t = stacker(rows)
    trow = lax.broadcasted_iota(jnp.int32, (lc, rows), 0)
    tcol = lax.broadcasted_iota(jnp.int32, (lc, rows), 1) % lc
    strict = tcol < trow
    causal = tcol <= trow
    eye = jnp.where(tcol == trow, 1.0, 0.0).astype(F32)
    level_masks = []
    s = 1
    while s < lc:
        lo, hi = s.bit_length() - 1, s.bit_length()
        level_masks.append(jnp.logical_and((trow >> hi) == (tcol >> hi), (trow >> lo) != (tcol >> lo)))
        s *= 2

    units = [(c, g) for c in chunks for g in groups]
    tile = lambda x, c, g: x[c * lc:(c + 1) * lc, g * gw:(g + 1) * gw]
    ar_b = {u: jnp.concatenate([tile(a_til, *u), tile(r_til, *u)], axis=0).astype(BF16) for u in units}
    bk_s = {u: jnp.concatenate([stack_c(tile(b_til, *u).astype(BF16)), stack_c(tile(k_til, *u).astype(BF16))],
                               axis=0) for u in units}
    v_b = {u: tile(v, *u).astype(BF16) for u in units}
    v_s = {u: stack_c(v_b[u]) for u in units}
    bk_hat = {u: jnp.concatenate([tile(b_hat, *u), tile(k_hat, *u)], axis=0).astype(BF16) for u in units}

    pm = {u: _dot_nt(ar_b[u], bk_s[u]) for u in units}
    n_ab = {u: jnp.where(strict, pm[u][:lc, :rows], 0.0) for u in units}
    m_xk = {u: jnp.concatenate([jnp.where(strict, pm[u][:lc, rows:], 0.0),
                                jnp.where(causal, pm[u][lc:, rows:], 0.0)], axis=0).astype(BF16) for u in units}
    m_rb = {u: jnp.where(causal, pm[u][lc:, :rows], 0.0).astype(BF16) for u in units}

    n_b = {u: n_ab[u].astype(BF16) for u in units}
    zero_b = jnp.zeros((lc, rows), BF16)
    t_inv = {u: eye + jnp.where(level_masks[0], n_ab[u], 0.0) for u in units}
    for msk in level_masks[1:]:
        t_b = {u: t_inv[u].astype(BF16) for u in units}
        half = {u: _dot(t_b[u], stack_t(jnp.where(msk, n_b[u], zero_b))) for u in units}
        t_inv = {u: t_inv[u] + _dot(half[u].astype(BF16), stack_t(t_b[u])) for u in units}
    t_b = {u: t_inv[u].astype(BF16) for u in units}
    resid = {u: eye - t_b[u].astype(F32) + _dot(n_b[u], stack_t(t_b[u])) for u in units}
    t_fix = {u: _dot(t_b[u], stack_t(resid[u].astype(BF16))).astype(BF16) for u in units}

    state = [s_ref[j] for j in tiles]
    tiles_of = lambda g: range(g * (gw // sw), (g + 1) * (gw // sw))
    sub = lambda x, g, j: x[:, (j - g * (gw // sw)) * sw:(j - g * (gw // sw) + 1) * sw]
    y_rows = []
    for c in chunks:
        us = [(c, g) for g in groups]
        zy = [jnp.concatenate([_dot_nt(sub(ar_b[u], u[1], j), state[j].astype(BF16)) for j in tiles_of(u[1])], axis=1)
              + _dot(m_xk[u], v_s[u]) for u in us]
        ub = []
        for i, u in enumerate(us):
            z_s = stack_c(zy[i][:lc].astype(BF16))
            ub.append(_dot(jnp.concatenate([t_b[u], t_fix[u]], axis=1), jnp.concatenate([z_s, z_s], axis=0)).astype(BF16))
        y_rows.append(jnp.concatenate([zy[i][lc:] + _dot(m_rb[u], stack_c(ub[i])) for i, u in enumerate(us)], axis=1))
        for i, u in enumerate(us):
            for j in tiles_of(u[1]):
                upd = _dot_tn(jnp.concatenate([sub(ub[i], u[1], j), sub(v_b[u], u[1], j)], axis=0),
                              sub(bk_hat[u], u[1], j))
                state[j] = state[j] * w_end[c][:, tls[j]] + jnp.where(same_head, upd, 0.0)
    for j in tiles:
        s_ref[j] = state[j]
    y = jnp.concatenate(y_rows, axis=0)

    mu = seg_sum(y, parts=2) * (1.0 / R_N)
    yc = y - mu
    var = seg_sum(jnp.square(yc)) * (1.0 / R_N)
    yn = yc * lax.rsqrt(var + R_GN_EPS) * gng_ref[...] + gnb_ref[...]
    bonus = seg_sum(r * k * rk_ref[...]) * v
    o_ref[0] = ((yn + bonus) * gate).astype(o_ref.dtype)


def _wkv_core(proj, w2, a2, g2, w0, a0, k_k, k_a, r_k, gn_g, gn_b):
    bsz, seq, n_proj = proj.shape
    d = w0.shape[-1]
    lw, la, lg = w2.shape[0], a2.shape[0], g2.shape[0]
    n_tail = lw + la + lg
    assert n_proj == 3 * d + n_tail and (3 * d) % n_tail == 0
    gw = R_GROUP * R_N
    sw = R_STATE_HEADS * R_N
    width = R_GROUPS_PER_STEP * gw
    nb = d // width
    tb = min(R_CHUNK * R_CHUNKS_PER_STEP, seq)
    nc = seq // tb
    w2p = jnp.concatenate([w2, jnp.zeros_like(a2)], axis=0).astype(BF16)
    a2p = jnp.concatenate([jnp.zeros_like(w2), a2], axis=0).astype(BF16)
    tile = lambda off: pl.BlockSpec((1, tb, width), lambda b, g, c: (b, c, off * nb + g))
    tail = pl.BlockSpec((1, tb, n_tail), lambda b, g, c: (b, c, 3 * d // n_tail))
    wa = pl.BlockSpec((lw + la, width), lambda b, g, c: (0, g))
    wg = pl.BlockSpec((lg, width), lambda b, g, c: (0, g))
    par = pl.BlockSpec((1, width), lambda b, g, c: (0, g))
    row = lambda p: p.reshape(1, d)
    return pl.pallas_call(
        functools.partial(_wkv_body, n_tanh=lw),
        grid=(bsz, nb, nc),
        in_specs=[tile(0), tile(1), tile(2), tail, wa, wa, wg] + [par] * 7,
        out_specs=pl.BlockSpec((1, tb, width), lambda b, g, c: (b, c, g)),
        out_shape=jax.ShapeDtypeStruct((bsz, seq, d), BF16),
        scratch_shapes=[pltpu.VMEM((width // sw, sw, sw), F32)],
        compiler_params=_cparams(("parallel", "parallel", "arbitrary")),
        name="wkv7_chunked",
    )(proj, proj, proj, proj, w2p, a2p, g2.astype(BF16),
      row(w0), row(a0), row(k_k), row(k_a), row(r_k), row(gn_g), row(gn_b))


def _rwkv_mixer(x, w_in, mu, w0, w2, a0, a2, g2, k_k, k_a, r_k, gn_g, gn_b):
    bsz, seq, d = x.shape
    n_proj = w_in.shape[1]
    proj = _proj_shift_lerp(x.reshape(bsz * seq, d), w_in.astype(BF16), mu, seq=seq, tm=512, col_chunk=256)
    return _wkv_core(proj.reshape(bsz, seq, n_proj), w2, a2, g2, w0, a0, k_k, k_a, r_k, gn_g, gn_b)


def kernel(x, mlstm_w_in, mlstm_b_i, mlstm_b_f, mlstm_conv_w, mlstm_conv_b, mlstm_norm_g, mlstm_w_out, rwkv_w_in, rwkv_mu, rwkv_w0, rwkv_w2, rwkv_a0, rwkv_a2, rwkv_g2, rwkv_k_k, rwkv_k_a, rwkv_r_k, rwkv_gn_g, rwkv_gn_b, rwkv_w_out, ln_mix_g, ln_mix_b, mlp_w1, mlp_w2, ln_ffn_g, ln_ffn_b):
    bsz, seq, d = x.shape
    m = bsz * seq
    xf = x.reshape(m, d)
    for layer in range(DEPTH):
        j = layer // 2
        if layer % 2 == 0:
            mix = _mlstm_mixer(xf.reshape(bsz, seq, d), mlstm_w_in[j], mlstm_b_i[j], mlstm_b_f[j],
                               mlstm_conv_w[j], mlstm_conv_b[j], mlstm_norm_g[j])
            w_out = mlstm_w_out[j]
        else:
            mix = _rwkv_mixer(xf.reshape(bsz, seq, d), rwkv_w_in[j], rwkv_mu[j], rwkv_w0[j], rwkv_w2[j],
                              rwkv_a0[j], rwkv_a2[j], rwkv_g2[j], rwkv_k_k[j], rwkv_k_a[j],
                              rwkv_r_k[j].reshape(-1), rwkv_gn_g[j], rwkv_gn_b[j])
            w_out = rwkv_w_out[j]
        xf = _mix_ffn(mix.reshape(m, -1), w_out.astype(BF16), xf, ln_mix_g[layer], ln_mix_b[layer],
                      mlp_w1[layer].astype(BF16), mlp_w2[layer].astype(BF16), ln_ffn_g[layer], ln_ffn_b[layer],
                      tm=512, ff_chunk=1024, row_split=2)
    return xf.reshape(bsz, seq, d)
```

```python
import functools
import math

import jax
import jax.numpy as jnp
from jax import lax
from jax.experimental import pallas as pl
from jax.experimental.pallas import tpu as pltpu

F32 = jnp.float32
BF16 = jnp.bfloat16

DEPTH = 2
DN_ALPHA = (2.0 * DEPTH) ** 0.25
LN_EPS = 1e-5

M_HEADS = 4
M_DK = 128
M_DV = 256
M_GATE_CAP = 15.0
M_CHUNK = 256

R_N = 64
R_GN_EPS = 64e-5
R_DECAY_SCALE = math.exp(-0.5)
R_CHUNK = 64
R_GROUP = 2
R_STATE_HEADS = 2
R_GROUPS_PER_STEP = 8
R_CHUNKS_PER_STEP = 4

VMEM_LIMIT = 48 * 1024 * 1024


def _cparams(sem):
    return pltpu.CompilerParams(dimension_semantics=sem, vmem_limit_bytes=VMEM_LIMIT)


def _dot(a, b, precision=None):
    return jnp.dot(a, b, preferred_element_type=F32, precision=precision)


def _dot_nt(a, b, precision=None):
    return lax.dot_general(a, b, (((1,), (1,)), ((), ())), preferred_element_type=F32, precision=precision)


def _dot_tn(a, b, precision=None):
    return lax.dot_general(a, b, (((0,), (0,)), ((), ())), preferred_element_type=F32, precision=precision)


def _softplus(z):
    return jnp.maximum(z, 0.0) + jnp.log1p(jnp.exp(-jnp.abs(z)))


def _sigmoid(z):
    return 0.5 + 0.5 * jnp.tanh(0.5 * z)


def _split_bf16(x):
    hi = x.astype(BF16)
    return hi, (x - hi.astype(F32)).astype(BF16)


def _split3_bf16(x):
    hi, rest = x.astype(BF16), x
    rest = rest - hi.astype(F32)
    mid = rest.astype(BF16)
    return hi, mid, (rest - mid.astype(F32)).astype(BF16)


def _proj_lerp_body(x_ref, w_ref, mu_ref, o_ref, carry_ref, *, tiles_per_seq, col_chunk):
    @pl.when(pl.program_id(0) % tiles_per_seq == 0)
    def _():
        carry_ref[...] = jnp.zeros_like(carry_ref)

    xb = x_ref[...].astype(BF16)
    tm = xb.shape[0]
    first = lax.broadcasted_iota(jnp.int32, (8, col_chunk), 0) == 0

    def epilogue(acc, cols):
        rolled = pltpu.roll(acc, 1, 0)
        top = jnp.where(first, carry_ref[0:1, cols], rolled[0:8])
        shifted = jnp.concatenate([top, rolled[8:]], axis=0)
        o_ref[:, cols] = (acc + mu_ref[:, cols] * (shifted - acc)).astype(o_ref.dtype)
        carry_ref[:, cols] = jnp.broadcast_to(acc[tm - 1:tm, :], (8, col_chunk))

    chunks = [slice(c0, c0 + col_chunk) for c0 in range(0, w_ref.shape[1], col_chunk)]
    pending = None
    for cols in chunks:
        acc = _dot(xb, w_ref[:, cols])
        if pending is not None:
            epilogue(*pending)
        pending = (acc, cols)
    epilogue(*pending)


def _proj_shift_lerp(x, w, mu, *, seq, tm, col_chunk):
    m, k = x.shape
    n = w.shape[1]
    tm = min(tm, seq)
    assert seq % tm == 0 and n % col_chunk == 0
    return pl.pallas_call(
        functools.partial(_proj_lerp_body, tiles_per_seq=seq // tm, col_chunk=col_chunk),
        grid=(m // tm,),
        in_specs=[pl.BlockSpec((tm, k), lambda i: (i, 0)),
                  pl.BlockSpec((k, n), lambda i: (0, 0), pipeline_mode=pl.Buffered(1)),
                  pl.BlockSpec((1, n), lambda i: (0, 0), pipeline_mode=pl.Buffered(1))],
        out_specs=pl.BlockSpec((tm, n), lambda i: (i, 0)),
        out_shape=jax.ShapeDtypeStruct((m, n), F32),
        scratch_shapes=[pltpu.VMEM((8, n), F32)],
        compiler_params=_cparams(("arbitrary",)),
        name="proj_token_shift",
    )(x, w, mu.reshape(1, n))


def _proj_conv_body(x_ref, w_ref, wg_ref, gb_ref, cw_ref, cb_ref, o_ref, g_ref, carry_ref, *,
                    tiles_per_seq, col_chunk):
    @pl.when(pl.program_id(0) % tiles_per_seq == 0)
    def _():
        carry_ref[...] = jnp.zeros_like(carry_ref)

    xb = x_ref[...].astype(BF16)
    tm = xb.shape[0]
    taps, n_conv = cw_ref.shape

    def epilogue(acc, cols):
        if cols.start < n_conv:
            ext = jnp.concatenate([carry_ref[:, cols], acc], axis=0)
            conv = cw_ref[0:1, cols] * ext
            for tap in range(1, taps):
                conv = pltpu.roll(conv, 1, 0) + cw_ref[tap:tap + 1, cols] * ext
            conv = conv[8:] + cb_ref[:, cols]
            o_ref[:, cols] = (conv * _sigmoid(conv)).astype(o_ref.dtype)
            carry_ref[:, cols] = acc[tm - 8:tm]
        else:
            o_ref[:, cols] = acc.astype(o_ref.dtype)

    chunks = [slice(c0, c0 + col_chunk) for c0 in range(0, w_ref.shape[1], col_chunk)]
    heavy = [c for c in chunks if c.start < n_conv]
    light = [c for c in chunks if c.start >= n_conv]
    per_heavy = len(light) // max(len(heavy), 1)
    order = []
    for i, c in enumerate(heavy):
        order += [c] + light[i * per_heavy:(i + 1) * per_heavy]
    order += light[len(heavy) * per_heavy:]
    pending = None
    for cols in order:
        acc = _dot(xb, w_ref[:, cols])
        if pending is not None:
            epilogue(*pending)
        pending = (acc, cols)
    capped = M_GATE_CAP * jnp.tanh((_dot(xb, wg_ref[...]) + gb_ref[...]) * (1.0 / M_GATE_CAP))
    is_input_gate = lax.broadcasted_iota(jnp.int32, capped.shape, 1) < M_HEADS
    log_gate = jnp.where(is_input_gate, capped, -_softplus(-capped))
    g_ref[0] = log_gate.T[0:2 * M_HEADS, :]
    epilogue(*pending)


def _proj_conv_silu(x, w, w_gate, b_gate, conv_w, conv_b, *, seq, tm, col_chunk):
    m, k = x.shape
    n = w.shape[1]
    n_gate = w_gate.shape[1]
    assert 2 * M_HEADS == 8 and n_gate % 128 == 0
    taps, n_conv = conv_w.shape
    tm = min(tm, seq)
    assert seq % tm == 0 and n % col_chunk == 0 and n_conv % col_chunk == 0 and taps <= 8
    resident = lambda shape: pl.BlockSpec(shape, lambda i: (0, 0), pipeline_mode=pl.Buffered(1))
    return pl.pallas_call(
        functools.partial(_proj_conv_body, tiles_per_seq=seq // tm, col_chunk=col_chunk),
        grid=(m // tm,),
        in_specs=[pl.BlockSpec((tm, k), lambda i: (i, 0)), resident((k, n)), resident((k, n_gate)),
                  resident((1, n_gate)),
                  resident((taps, n_conv)), resident((1, n_conv))],
        out_specs=[pl.BlockSpec((tm, n), lambda i: (i, 0)), pl.BlockSpec((1, 2 * M_HEADS, tm), lambda i: (i, 0, 0))],
        out_shape=[jax.ShapeDtypeStruct((m, n), BF16), jax.ShapeDtypeStruct((m // tm, 2 * M_HEADS, tm), F32)],
        scratch_shapes=[pltpu.VMEM((8, n_conv), F32)],
        compiler_params=_cparams(("arbitrary",)),
        name="proj_conv_silu",
    )(x, w, w_gate, b_gate.reshape(1, n_gate), conv_w, conv_b.reshape(1, n_conv))


def _layer_norm(y, g, b):
    mu = jnp.mean(y, axis=-1, keepdims=True)
    yc = y - mu
    var = jnp.mean(jnp.square(yc), axis=-1, keepdims=True)
    return yc * lax.rsqrt(var + LN_EPS) * g + b


def _mix_ffn_body(mix_ref, wo_ref, res_ref, g1_ref, b1_ref, w1_ref, w2_ref, g2_ref, b2_ref, o_ref, *,
                  ff_chunk, row_split):
    sub = mix_ref.shape[0] // row_split

    def stages(rows):
        st = {}

        def out_proj():
            st["y"] = DN_ALPHA * res_ref[rows, :] + _dot(mix_ref[rows, :], wo_ref[...])

        def norm1():
            st["x1"] = _layer_norm(st.pop("y"), g1_ref[...], b1_ref[...])
            st["x1b"] = st["x1"].astype(BF16)
            st["acc"] = DN_ALPHA * st.pop("x1")

        def up(f):
            st["hid"] = _dot(st["x1b"], w1_ref[:, f:f + ff_chunk])

        def act():
            st["hid"] = jnp.square(jnp.maximum(st["hid"], 0.0)).astype(BF16)

        def down(f):
            st["acc"] = st["acc"] + _dot(st.pop("hid"), w2_ref[f:f + ff_chunk, :])

        def norm2():
            o_ref[rows, :] = _layer_norm(st.pop("acc"), g2_ref[...], b2_ref[...])

        seq = [out_proj, norm1]
        for f in range(0, w1_ref.shape[1], ff_chunk):
            seq += [functools.partial(up, f), act, functools.partial(down, f)]
        return seq + [norm2]

    threads = [stages(slice(h * sub, (h + 1) * sub)) for h in range(row_split)]
    for t in range(len(threads[0]) + row_split - 1):
        for h, seq in enumerate(threads):
            if 0 <= t - h < len(seq):
                seq[t - h]()


def _mix_ffn(mix, w_out, res, g1, b1, w1, w2, g2, b2, *, tm, ff_chunk, row_split):
    m, k = mix.shape
    d = w_out.shape[1]
    dff = w1.shape[1]
    tm = min(tm, m)
    assert m % tm == 0 and dff % ff_chunk == 0 and tm % (8 * row_split) == 0
    resident = lambda shape: pl.BlockSpec(shape, lambda i: (0, 0), pipeline_mode=pl.Buffered(1))
    rows = lambda width: pl.BlockSpec((tm, width), lambda i: (i, 0))
    vec = lambda p: p.reshape(1, d)
    return pl.pallas_call(
        functools.partial(_mix_ffn_body, ff_chunk=ff_chunk, row_split=row_split),
        grid=(m // tm,),
        in_specs=[rows(k), resident((k, d)), rows(d), resident((1, d)), resident((1, d)),
                  resident((d, dff)), resident((dff, d)), resident((1, d)), resident((1, d))],
        out_specs=rows(d),
        out_shape=jax.ShapeDtypeStruct((m, d), F32),
        compiler_params=_cparams(("parallel",)),
        name="outproj_ln_ffn_ln",
    )(mix, w_out, res, vec(g1), vec(b1), w1, w2, vec(g2), vec(b2))


def _mlstm_body(q_ref, k_ref, v_ref, o_ref, gate_ref, ng_ref, h_ref, c_ref, n_ref, m_ref):
    lc = q_ref.shape[1]
    heads = range(M_HEADS)

    @pl.when(pl.program_id(1) == 0)
    def _():
        c_ref[...] = jnp.zeros_like(c_ref)
        n_ref[...] = jnp.zeros_like(n_ref)
        m_ref[...] = jnp.zeros_like(m_ref)

    ri = lax.broadcasted_iota(jnp.int32, (lc, lc), 0)
    ci = lax.broadcasted_iota(jnp.int32, (lc, lc), 1)
    causal = ci <= ri
    diag = ci == ri
    g_rows = gate_ref[0]
    tri_u = jnp.where(ri <= ci, 1.0, 0.0).astype(BF16)
    cum_rows = _dot(jnp.concatenate(_split3_bf16(g_rows), axis=1), jnp.concatenate([tri_u] * 3, axis=0))
    scale = M_DK ** -0.5
    q_all = q_ref[0]
    k_all = k_ref[0]
    v_all = v_ref[0]
    qb = [q_all[:, h * M_DK:(h + 1) * M_DK] for h in heads]
    kb = [k_all[:, h * M_DK:(h + 1) * M_DK] for h in heads]
    vb = [v_all[:, h * M_DV:(h + 1) * M_DV] for h in heads]
    q = [qb[h].astype(F32) for h in heads]
    k = [kb[h].astype(F32) for h in heads]
    i_row = [g_rows[h:h + 1, :] for h in heads]
    f_row = [g_rows[M_HEADS + h:M_HEADS + h + 1, :] for h in heads]
    bcum_row = [cum_rows[M_HEADS + h:M_HEADS + h + 1, :] for h in heads]
    bcum_col = [jnp.sum(jnp.where(causal, f_row[h], 0.0), axis=-1, keepdims=True) for h in heads]
    i_col = [jnp.sum(jnp.where(diag, i_row[h], 0.0), axis=-1, keepdims=True) for h in heads]
    m_prev = [m_ref[h, 0:1, 0:1] for h in heads]
    c_prev = [c_ref[h] for h in heads]
    n_prev = [n_ref[h, 0:1, :] for h in heads]

    log_d = [jnp.where(causal, bcum_col[h] - bcum_row[h] + i_row[h], -jnp.inf) for h in heads]
    log_inter = [bcum_col[h] + m_prev[h] for h in heads]
    m_row = [jnp.maximum(jnp.max(log_d[h], axis=-1, keepdims=True), log_inter[h]) for h in heads]
    inter = [jnp.exp(log_inter[h] - m_row[h]) for h in heads]
    s = [_dot_nt(qb[h], kb[h]) * (scale * jnp.exp(log_d[h] - m_row[h])) for h in heads]
    inter_s = [scale * inter[h] for h in heads]
    qc = [_dot(qb[h], c_prev[h].astype(BF16)) for h in heads]
    num = [_dot(s[h].astype(BF16), vb[h]) + inter_s[h] * qc[h] for h in heads]
    den = [jnp.sum(s[h], axis=-1, keepdims=True) + inter_s[h] * jnp.sum(q[h] * n_prev[h], axis=-1, keepdims=True)
           for h in heads]
    hc = [num[h] / jnp.maximum(jnp.abs(den[h]), jnp.exp(-m_row[h])) for h in heads]

    b_last = [bcum_col[h][lc - 1:lc, :] for h in heads]
    log_w = [b_last[h] - bcum_col[h] + i_col[h] for h in heads]
    m_new = [jnp.maximum(b_last[h] + m_prev[h], jnp.max(log_w[h], axis=0, keepdims=True)) for h in heads]
    decay = [jnp.exp(b_last[h] + m_prev[h] - m_new[h]) for h in heads]
    kw = [k[h] * jnp.exp(log_w[h] - m_new[h]) for h in heads]
    for h in heads:
        c_ref[h] = decay[h] * c_prev[h] + _dot_tn(kw[h].astype(BF16), vb[h])
        n_ref[h] = jnp.broadcast_to(decay[h] * n_prev[h] + jnp.sum(kw[h], axis=0, keepdims=True), n_ref.shape[1:])
        m_ref[h] = jnp.broadcast_to(m_new[h], m_ref.shape[1:])

    hn = jnp.concatenate([hc[h] * lax.rsqrt(jnp.mean(jnp.square(hc[h]), axis=-1, keepdims=True) + 1e-6)
                          for h in heads], axis=1)
    h_ref[0] = (hn * ng_ref[...] * _sigmoid(o_ref[0].astype(F32))).astype(h_ref.dtype)


def _mlstm_core(proj, gates, norm_g):
    bsz, seq, _ = proj.shape
    tg = gates.shape[-1]
    lc = min(M_CHUNK, seq)
    nc = seq // lc
    assert tg % lc == 0 and seq % tg == 0
    per_tile = tg // lc
    hdk = M_HEADS * M_DK
    hdv = M_HEADS * M_DV
    assert 2 * hdk == hdv
    return pl.pallas_call(
        _mlstm_body,
        grid=(bsz, nc),
        in_specs=[pl.BlockSpec((1, lc, hdk), lambda b, c: (b, c, 0)),
                  pl.BlockSpec((1, lc, hdk), lambda b, c: (b, c, 1)),
                  pl.BlockSpec((1, lc, hdv), lambda b, c: (b, c, 1)),
                  pl.BlockSpec((1, lc, hdv), lambda b, c: (b, c, 2)),
                  pl.BlockSpec((1, 2 * M_HEADS, lc), lambda b, c: (b * (seq // tg) + c // per_tile, 0, c % per_tile)),
                  pl.BlockSpec((1, hdv), lambda b, c: (0, 0))],
        out_specs=pl.BlockSpec((1, lc, hdv), lambda b, c: (b, c, 0)),
        out_shape=jax.ShapeDtypeStruct((bsz, seq, hdv), BF16),
        scratch_shapes=[pltpu.VMEM((M_HEADS, M_DK, M_DV), F32),
                        pltpu.VMEM((M_HEADS, 8, M_DK), F32),
                        pltpu.VMEM((M_HEADS, 8, 128), F32)],
        compiler_params=_cparams(("parallel", "arbitrary")),
        name="mlstm_chunkwise",
    )(proj, proj, proj, proj, gates, norm_g.reshape(1, hdv))


def _mlstm_mixer(x, w_in, b_i, b_f, conv_w, conv_b, norm_g):
    bsz, seq, d = x.shape
    m = bsz * seq
    hdv = M_HEADS * M_DV
    n_main = 2 * M_HEADS * M_DK + 2 * hdv
    xf = x.reshape(m, d)
    pad = 128 - 2 * M_HEADS
    w_gate = jnp.pad(w_in[:, n_main:], ((0, 0), (0, pad))).astype(BF16)
    b_gate = jnp.pad(jnp.concatenate([b_i, b_f]), (0, pad))
    proj, gates = _proj_conv_silu(xf, w_in[:, :n_main].astype(BF16), w_gate, b_gate, conv_w, conv_b,
                                  seq=seq, tm=512, col_chunk=256)
    proj = proj.reshape(bsz, seq, n_main)
    return _mlstm_core(proj, gates, norm_g)


def _wkv_body(r_ref, k_ref, v_ref, tail_ref, w2_ref, a2_ref, g2_ref, w0_ref, a0_ref, kk_ref, ka_ref, rk_ref,
              gng_ref, gnb_ref, o_ref, s_ref, *, n_tanh):
    tb = r_ref.shape[1]
    lc = min(R_CHUNK, tb)
    chunks = range(tb // lc)
    width = r_ref.shape[2]
    gw = R_GROUP * R_N
    sw = R_STATE_HEADS * R_N
    rows = R_GROUP * lc
    assert tb % lc == 0 and gw % sw == 0 and width % gw == 0

    @pl.when(pl.program_id(2) == 0)
    def _():
        s_ref[...] = jnp.zeros_like(s_ref)

    groups = range(width // gw)
    tiles = range(width // sw)
    tls = [slice(j * sw, (j + 1) * sw) for j in tiles]
    brow = lax.broadcasted_iota(jnp.int32, (sw, sw), 0) // R_N
    bcol = lax.broadcasted_iota(jnp.int32, (sw, sw), 1) // R_N
    same_head = brow == bcol
    head_ones = jnp.where(same_head, 1.0, 0.0).astype(BF16)

    def seg_sum(x, parts=1):
        pieces = (x.astype(BF16),) if parts == 1 else _split_bf16(x)
        ones = jnp.concatenate([head_ones] * len(pieces), axis=0)
        return jnp.concatenate([_dot(jnp.concatenate([piece[:, sl] for piece in pieces], axis=1), ones)
                                for sl in tls], axis=1)

    r = r_ref[0].astype(F32)
    k = k_ref[0].astype(F32)
    v = v_ref[0].astype(F32)
    n_wa = w2_ref.shape[0]
    tail = tail_ref[0].astype(F32)
    t_wa = tail[:, :n_wa]
    is_tanh = lax.broadcasted_iota(jnp.int32, (1, n_wa), 1) < n_tanh
    f_wa = jnp.where(is_tanh, jnp.tanh(t_wa), t_wa).astype(BF16)
    gate = _dot(_sigmoid(tail[:, n_wa:]).astype(BF16), g2_ref[...])
    ld = (-R_DECAY_SCALE) * _sigmoid(w0_ref[...] + _dot(f_wa, w2_ref[...]))
    a_lr = _sigmoid(a0_ref[...] + _dot(f_wa, a2_ref[...]))
    kk = k * kk_ref[...]
    kk = kk * lax.rsqrt(jnp.maximum(seg_sum(jnp.square(kk)), 1e-24))
    k = k * (a_lr * ka_ref[...] + (1.0 - ka_ref[...]))
    b_ = kk * a_lr

    ti = lax.broadcasted_iota(jnp.int32, (tb, tb), 0)
    tj = lax.broadcasted_iota(jnp.int32, (tb, tb), 1)
    in_chunk_tri = jnp.logical_and(tj <= ti, ti // lc == tj // lc)
    tri = jnp.where(in_chunk_tri, 1.0, 0.0).astype(BF16)
    cum = _dot(jnp.concatenate([tri, tri], axis=1), jnp.concatenate(_split_bf16(ld), axis=0))
    e_pos = jnp.exp(cum)
    e_neg = jnp.exp(-cum)
    w_end = [jnp.exp(cum[(c + 1) * lc - 1:(c + 1) * lc, :]) for c in chunks]
    a_til = -kk * jnp.exp(cum - ld)
    r_til = r * e_pos
    b_til = b_ * e_neg
    k_til = k * e_neg

    def stacker(n):
        lane_head = lax.broadcasted_iota(jnp.int32, (lc, n), 1) // (n // R_GROUP)
        sel = [lane_head == h for h in range(R_GROUP)]
        return lambda x: jnp.concatenate([jnp.where(s_, x, jnp.zeros_like(x)) for s_ in sel], axis=0)

    stack_c = stacker(gw)
    stack_t = stacker(rows)
    trow = lax.broadcasted_iota(jnp.int32, (lc, rows), 0)
    tcol = lax.broadcasted_iota(jnp.int32, (lc, rows), 1) % lc
    strict = tcol < trow
    causal = tcol <= trow
    eye = jnp.where(tcol == trow, 1.0, 0.0).astype(F32)
    level_masks = []
    s = 1
    while s < lc:
        lo, hi = s.bit_length() - 1, s.bit_length()
        level_masks.append(jnp.logical_and((trow >> hi) == (tcol >> hi), (trow >> lo) != (tcol >> lo)))
        s *= 2

    units = [(c, g) for c in chunks for g in groups]
    tile = lambda x, c, g: x[c * lc:(c + 1) * lc, g * gw:(g + 1) * gw]
    ar_b = {u: jnp.concatenate([tile(a_til, *u), tile(r_til, *u)], axis=0).astype(BF16) for u in units}
    bk_b = {u: (tile(b_til, *u).astype(BF16), tile(k_til, *u).astype(BF16)) for u in units}
    bk_s = {u: jnp.concatenate([stack_c(bk_b[u][0]), stack_c(bk_b[u][1])], axis=0) for u in units}
    v_b = {u: tile(v, *u).astype(BF16) for u in units}
    v_s = {u: stack_c(v_b[u]) for u in units}
    bk_cat = {u: jnp.concatenate(bk_b[u], axis=0) for u in units}

    pm = {u: _dot_nt(ar_b[u], bk_s[u]) for u in units}
    n_ab = {u: jnp.where(strict, pm[u][:lc, :rows], 0.0) for u in units}
    m_xk = {u: jnp.concatenate([jnp.where(strict, pm[u][:lc, rows:], 0.0),
                                jnp.where(causal, pm[u][lc:, rows:], 0.0)], axis=0).astype(BF16) for u in units}
    m_rb = {u: jnp.where(causal, pm[u][lc:, :rows], 0.0).astype(BF16) for u in units}

    n_b = {u: n_ab[u].astype(BF16) for u in units}
    zero_b = jnp.zeros((lc, rows), BF16)
    t_inv = {u: eye + jnp.where(level_masks[0], n_ab[u], 0.0) for u in units}
    for msk in level_masks[1:]:
        t_b = {u: t_inv[u].astype(BF16) for u in units}
        half = {u: _dot(t_b[u], stack_t(jnp.where(msk, n_b[u], zero_b))) for u in units}
        t_inv = {u: t_inv[u] + _dot(half[u].astype(BF16), stack_t(t_b[u])) for u in units}
    t_b = {u: t_inv[u].astype(BF16) for u in units}
    resid = {u: eye - t_b[u].astype(F32) + _dot(n_b[u], stack_t(t_b[u])) for u in units}
    t_fix = {u: _dot(t_b[u], stack_t(resid[u].astype(BF16))).astype(BF16) for u in units}

    state = [s_ref[j] for j in tiles]
    tiles_of = lambda g: range(g * (gw // sw), (g + 1) * (gw // sw))
    sub = lambda x, g, j: x[:, (j - g * (gw // sw)) * sw:(j - g * (gw // sw) + 1) * sw]
    y_rows = []
    for c in chunks:
        us = [(c, g) for g in groups]
        zy = [jnp.concatenate([_dot_nt(sub(ar_b[u], u[1], j), state[j].astype(BF16)) for j in tiles_of(u[1])], axis=1)
              + _dot(m_xk[u], v_s[u]) for u in us]
        ub = []
        for i, u in enumerate(us):
            z_s = stack_c(zy[i][:lc].astype(BF16))
            ub.append(_dot(jnp.concatenate([t_b[u], t_fix[u]], axis=1), jnp.concatenate([z_s, z_s], axis=0)).astype(BF16))
        y_rows.append(jnp.concatenate([zy[i][lc:] + _dot(m_rb[u], stack_c(ub[i])) for i, u in enumerate(us)], axis=1))
        for i, u in enumerate(us):
            for j in tiles_of(u[1]):
                upd = _dot_tn(jnp.concatenate([sub(ub[i], u[1], j), sub(v_b[u], u[1], j)], axis=0),
                              sub(bk_cat[u], u[1], j))
                state[j] = (state[j] + jnp.where(same_head, upd, 0.0)) * w_end[c][:, tls[j]]
    for j in tiles:
        s_ref[j] = state[j]
    y = jnp.concatenate(y_rows, axis=0)

    mu = seg_sum(y, parts=2) * (1.0 / R_N)
    yc = y - mu
    var = seg_sum(jnp.square(yc)) * (1.0 / R_N)
    yn = yc * lax.rsqrt(var + R_GN_EPS) * gng_ref[...] + gnb_ref[...]
    bonus = seg_sum(r * k * rk_ref[...]) * v
    o_ref[0] = ((yn + bonus) * gate).astype(o_ref.dtype)


def _wkv_core(proj, w2, a2, g2, w0, a0, k_k, k_a, r_k, gn_g, gn_b):
    bsz, seq, n_proj = proj.shape
    d = w0.shape[-1]
    lw, la, lg = w2.shape[0], a2.shape[0], g2.shape[0]
    n_tail = lw + la + lg
    assert n_proj == 3 * d + n_tail and (3 * d) % n_tail == 0
    gw = R_GROUP * R_N
    sw = R_STATE_HEADS * R_N
    width = R_GROUPS_PER_STEP * gw
    nb = d // width
    tb = min(R_CHUNK * R_CHUNKS_PER_STEP, seq)
    nc = seq // tb
    w2p = jnp.concatenate([w2, jnp.zeros_like(a2)], axis=0).astype(BF16)
    a2p = jnp.concatenate([jnp.zeros_like(w2), a2], axis=0).astype(BF16)
    tile = lambda off: pl.BlockSpec((1, tb, width), lambda b, g, c: (b, c, off * nb + g))
    tail = pl.BlockSpec((1, tb, n_tail), lambda b, g, c: (b, c, 3 * d // n_tail))
    wa = pl.BlockSpec((lw + la, width), lambda b, g, c: (0, g))
    wg = pl.BlockSpec((lg, width), lambda b, g, c: (0, g))
    par = pl.BlockSpec((1, width), lambda b, g, c: (0, g))
    row = lambda p: p.reshape(1, d)
    return pl.pallas_call(
        functools.partial(_wkv_body, n_tanh=lw),
        grid=(bsz, nb, nc),
        in_specs=[tile(0), tile(1), tile(2), tail, wa, wa, wg] + [par] * 7,
        out_specs=pl.BlockSpec((1, tb, width), lambda b, g, c: (b, c, g)),
        out_shape=jax.ShapeDtypeStruct((bsz, seq, d), BF16),
        scratch_shapes=[pltpu.VMEM((width // sw, sw, sw), F32)],
        compiler_params=_cparams(("parallel", "parallel", "arbitrary")),
        name="wkv7_chunked",
    )(proj, proj, proj, proj, w2p, a2p, g2.astype(BF16),
      row(w0), row(a0), row(k_k), row(k_a), row(r_k), row(gn_g), row(gn_b))


def _rwkv_mixer(x, w_in, mu, w0, w2, a0, a2, g2, k_k, k_a, r_k, gn_g, gn_b):
    bsz, seq, d = x.shape
    n_proj = w_in.shape[1]
    proj = _proj_shift_lerp(x.reshape(bsz * seq, d), w_in.astype(BF16), mu, seq=seq, tm=512, col_chunk=256)
    return _wkv_core(proj.reshape(bsz, seq, n_proj), w2, a2, g2, w0, a0, k_k, k_a, r_k, gn_g, gn_b)


def kernel(x, mlstm_w_in, mlstm_b_i, mlstm_b_f, mlstm_conv_w, mlstm_conv_b, mlstm_norm_g, mlstm_w_out, rwkv_w_in, rwkv_mu, rwkv_w0, rwkv_w2, rwkv_a0, rwkv_a2, rwkv_g2, rwkv_k_k, rwkv_k_a, rwkv_r_k, rwkv_gn_g, rwkv_gn_b, rwkv_w_out, ln_mix_g, ln_mix_b, mlp_w1, mlp_w2, ln_ffn_g, ln_ffn_b):
    bsz, seq, d = x.shape
    m = bsz * seq
    xf = x.reshape(m, d)
    for layer in range(DEPTH):
        j = layer // 2
        if layer % 2 == 0:
            mix = _mlstm_mixer(xf.reshape(bsz, seq, d), mlstm_w_in[j], mlstm_b_i[j], mlstm_b_f[j],
                               mlstm_conv_w[j], mlstm_conv_b[j], mlstm_norm_g[j])
            w_out = mlstm_w_out[j]
        else:
            mix = _rwkv_mixer(xf.reshape(bsz, seq, d), rwkv_w_in[j], rwkv_mu[j], rwkv_w0[j], rwkv_w2[j],
                              rwkv_a0[j], rwkv_a2[j], rwkv_g2[j], rwkv_k_k[j], rwkv_k_a[j],
                              rwkv_r_k[j].reshape(-1), rwkv_gn_g[j], rwkv_gn_b[j])
            w_out = rwkv_w_out[j]
        xf = _mix_ffn(mix.reshape(m, -1), w_out.astype(BF16), xf, ln_mix_g[layer], ln_mix_b[layer],
                      mlp_w1[layer].astype(BF16), mlp_w2[layer].astype(BF16), ln_ffn_g[layer], ln_ffn_b[layer],
                      tm=512, ff_chunk=1024, row_split=2)
    return xf.reshape(bsz, seq, d)
```

```python
import functools
import math

import jax
import jax.numpy as jnp
from jax import lax
from jax.experimental import pallas as pl
from jax.experimental.pallas import tpu as pltpu

F32 = jnp.float32
BF16 = jnp.bfloat16

DEPTH = 2
DN_ALPHA = (2.0 * DEPTH) ** 0.25
LN_EPS = 1e-5

M_HEADS = 4
M_DK = 128
M_DV = 256
M_GATE_CAP = 15.0
M_CHUNK = 256

R_N = 64
R_GN_EPS = 64e-5
R_DECAY_SCALE = math.exp(-0.5)
R_CHUNK = 64
R_GROUP = 2
R_STATE_HEADS = 2
R_GROUPS_PER_STEP = 8
R_CHUNKS_PER_STEP = 4

VMEM_LIMIT = 56 * 1024 * 1024


def _cparams(sem):
    return pltpu.CompilerParams(dimension_semantics=sem, vmem_limit_bytes=VMEM_LIMIT)


def _dot(a, b, precision=None):
    return jnp.dot(a, b, preferred_element_type=F32, precision=precision)


def _dot_nt(a, b, precision=None):
    return lax.dot_general(a, b, (((1,), (1,)), ((), ())), preferred_element_type=F32, precision=precision)


def _dot_tn(a, b, precision=None):
    return lax.dot_general(a, b, (((0,), (0,)), ((), ())), preferred_element_type=F32, precision=precision)


def _softplus(z):
    return jnp.maximum(z, 0.0) + jnp.log1p(jnp.exp(-jnp.abs(z)))


def _sigmoid(z):
    return 0.5 + 0.5 * jnp.tanh(0.5 * z)


def _split_bf16(x):
    hi = x.astype(BF16)
    return hi, (x - hi.astype(F32)).astype(BF16)


def _split3_bf16(x):
    hi, rest = x.astype(BF16), x
    rest = rest - hi.astype(F32)
    mid = rest.astype(BF16)
    return hi, mid, (rest - mid.astype(F32)).astype(BF16)


def _proj_lerp_body(x_ref, w_ref, mu_ref, o_ref, carry_ref, *, tiles_per_seq, col_chunk):
    @pl.when(pl.program_id(0) % tiles_per_seq == 0)
    def _():
        carry_ref[...] = jnp.zeros_like(carry_ref)

    xb = x_ref[...].astype(BF16)
    tm = xb.shape[0]
    first = lax.broadcasted_iota(jnp.int32, (8, col_chunk), 0) == 0

    def epilogue(acc, cols):
        rolled = pltpu.roll(acc, 1, 0)
        top = jnp.where(first, carry_ref[0:1, cols], rolled[0:8])
        shifted = jnp.concatenate([top, rolled[8:]], axis=0)
        o_ref[:, cols] = (acc + mu_ref[:, cols] * (shifted - acc)).astype(o_ref.dtype)
        carry_ref[:, cols] = jnp.broadcast_to(acc[tm - 1:tm, :], (8, col_chunk))

    chunks = [slice(c0, c0 + col_chunk) for c0 in range(0, w_ref.shape[1], col_chunk)]
    pending = None
    for cols in chunks:
        acc = _dot(xb, w_ref[:, cols])
        if pending is not None:
            epilogue(*pending)
        pending = (acc, cols)
    epilogue(*pending)


def _proj_shift_lerp(x, w, mu, *, seq, tm, col_chunk):
    m, k = x.shape
    n = w.shape[1]
    tm = min(tm, seq)
    assert seq % tm == 0 and n % col_chunk == 0
    return pl.pallas_call(
        functools.partial(_proj_lerp_body, tiles_per_seq=seq // tm, col_chunk=col_chunk),
        grid=(m // tm,),
        in_specs=[pl.BlockSpec((tm, k), lambda i: (i, 0)),
                  pl.BlockSpec((k, n), lambda i: (0, 0), pipeline_mode=pl.Buffered(1)),
                  pl.BlockSpec((1, n), lambda i: (0, 0), pipeline_mode=pl.Buffered(1))],
        out_specs=pl.BlockSpec((tm, n), lambda i: (i, 0)),
        out_shape=jax.ShapeDtypeStruct((m, n), F32),
        scratch_shapes=[pltpu.VMEM((8, n), F32)],
        compiler_params=_cparams(("arbitrary",)),
        name="proj_token_shift",
    )(x, w, mu.reshape(1, n))


def _proj_conv_body(x_ref, w_ref, wg_ref, gb_ref, cw_ref, cb_ref, o_ref, g_ref, carry_ref, *,
                    tiles_per_seq, col_chunk):
    @pl.when(pl.program_id(0) % tiles_per_seq == 0)
    def _():
        carry_ref[...] = jnp.zeros_like(carry_ref)

    xb = x_ref[...].astype(BF16)
    tm = xb.shape[0]
    taps, n_conv = cw_ref.shape

    def epilogue(acc, cols):
        if cols.start < n_conv:
            ext = jnp.concatenate([carry_ref[:, cols], acc], axis=0)
            conv = cw_ref[0:1, cols] * ext
            for tap in range(1, taps):
                conv = pltpu.roll(conv, 1, 0) + cw_ref[tap:tap + 1, cols] * ext
            conv = conv[8:] + cb_ref[:, cols]
            o_ref[:, cols] = (conv * _sigmoid(conv)).astype(o_ref.dtype)
            carry_ref[:, cols] = acc[tm - 8:tm]
        else:
            o_ref[:, cols] = acc.astype(o_ref.dtype)

    chunks = [slice(c0, c0 + col_chunk) for c0 in range(0, w_ref.shape[1], col_chunk)]
    heavy = [c for c in chunks if c.start < n_conv]
    light = [c for c in chunks if c.start >= n_conv]
    per_heavy = len(light) // max(len(heavy), 1)
    order = []
    for i, c in enumerate(heavy):
        order += [c] + light[i * per_heavy:(i + 1) * per_heavy]
    order += light[len(heavy) * per_heavy:]
    pending = None
    for cols in order:
        acc = _dot(xb, w_ref[:, cols])
        if pending is not None:
            epilogue(*pending)
        pending = (acc, cols)
    capped = M_GATE_CAP * jnp.tanh((_dot(xb, wg_ref[...]) + gb_ref[...]) * (1.0 / M_GATE_CAP))
    is_input_gate = lax.broadcasted_iota(jnp.int32, capped.shape, 1) < M_HEADS
    log_gate = jnp.where(is_input_gate, capped, -_softplus(-capped))
    g_ref[0] = log_gate.T[0:2 * M_HEADS, :]
    epilogue(*pending)


def _proj_conv_silu(x, w, w_gate, b_gate, conv_w, conv_b, *, seq, tm, col_chunk):
    m, k = x.shape
    n = w.shape[1]
    n_gate = w_gate.shape[1]
    assert 2 * M_HEADS == 8 and n_gate % 128 == 0
    taps, n_conv = conv_w.shape
    tm = min(tm, seq)
    assert seq % tm == 0 and n % col_chunk == 0 and n_conv % col_chunk == 0 and taps <= 8
    resident = lambda shape: pl.BlockSpec(shape, lambda i: (0, 0), pipeline_mode=pl.Buffered(1))
    return pl.pallas_call(
        functools.partial(_proj_conv_body, tiles_per_seq=seq // tm, col_chunk=col_chunk),
        grid=(m // tm,),
        in_specs=[pl.BlockSpec((tm, k), lambda i: (i, 0)), resident((k, n)), resident((k, n_gate)),
                  resident((1, n_gate)),
                  resident((taps, n_conv)), resident((1, n_conv))],
        out_specs=[pl.BlockSpec((tm, n), lambda i: (i, 0)), pl.BlockSpec((1, 2 * M_HEADS, tm), lambda i: (i, 0, 0))],
        out_shape=[jax.ShapeDtypeStruct((m, n), BF16), jax.ShapeDtypeStruct((m // tm, 2 * M_HEADS, tm), F32)],
        scratch_shapes=[pltpu.VMEM((8, n_conv), F32)],
        compiler_params=_cparams(("arbitrary",)),
        name="proj_conv_silu",
    )(x, w, w_gate, b_gate.reshape(1, n_gate), conv_w, conv_b.reshape(1, n_conv))


def _layer_norm(y, g, b):
    mu = jnp.mean(y, axis=-1, keepdims=True)
    yc = y - mu
    var = jnp.mean(jnp.square(yc), axis=-1, keepdims=True)
    return yc * lax.rsqrt(var + LN_EPS) * g + b


def _mix_ffn_body(mix_ref, wo_ref, res_ref, g1_ref, b1_ref, w1_ref, w2_ref, g2_ref, b2_ref, o_ref, *,
                  ff_chunk, row_split):
    sub = mix_ref.shape[0] // row_split

    def stages(rows):
        st = {}

        def out_proj():
            st["y"] = DN_ALPHA * res_ref[rows, :] + _dot(mix_ref[rows, :], wo_ref[...])

        def norm1():
            st["x1"] = _layer_norm(st.pop("y"), g1_ref[...], b1_ref[...])
            st["x1b"] = st["x1"].astype(BF16)
            st["acc"] = DN_ALPHA * st.pop("x1")

        def up(f):
            st["hid"] = _dot(st["x1b"], w1_ref[:, f:f + ff_chunk])

        def act():
            st["hid"] = jnp.square(jnp.maximum(st["hid"], 0.0)).astype(BF16)

        def down(f):
            st["acc"] = st["acc"] + _dot(st.pop("hid"), w2_ref[f:f + ff_chunk, :])

        def norm2():
            o_ref[rows, :] = _layer_norm(st.pop("acc"), g2_ref[...], b2_ref[...])

        seq = [out_proj, norm1]
        for f in range(0, w1_ref.shape[1], ff_chunk):
            seq += [functools.partial(up, f), act, functools.partial(down, f)]
        return seq + [norm2]

    threads = [stages(slice(h * sub, (h + 1) * sub)) for h in range(row_split)]
    for t in range(len(threads[0]) + row_split - 1):
        for h, seq in enumerate(threads):
            if 0 <= t - h < len(seq):
                seq[t - h]()


def _mix_ffn(mix, w_out, res, g1, b1, w1, w2, g2, b2, *, tm, ff_chunk, row_split):
    m, k = mix.shape
    d = w_out.shape[1]
    dff = w1.shape[1]
    tm = min(tm, m)
    assert m % tm == 0 and dff % ff_chunk == 0 and tm % (8 * row_split) == 0
    resident = lambda shape: pl.BlockSpec(shape, lambda i: (0, 0), pipeline_mode=pl.Buffered(1))
    rows = lambda width: pl.BlockSpec((tm, width), lambda i: (i, 0))
    vec = lambda p: p.reshape(1, d)
    return pl.pallas_call(
        functools.partial(_mix_ffn_body, ff_chunk=ff_chunk, row_split=row_split),
        grid=(m // tm,),
        in_specs=[rows(k), resident((k, d)), rows(d), resident((1, d)), resident((1, d)),
                  resident((d, dff)), resident((dff, d)), resident((1, d)), resident((1, d))],
        out_specs=rows(d),
        out_shape=jax.ShapeDtypeStruct((m, d), F32),
        compiler_params=_cparams(("parallel",)),
        name="outproj_ln_ffn_ln",
    )(mix, w_out, res, vec(g1), vec(b1), w1, w2, vec(g2), vec(b2))


def _mlstm_body(q_ref, k_ref, v_ref, o_ref, gate_ref, ng_ref, h_ref, c_ref, n_ref, m_ref):
    lc = q_ref.shape[1]
    heads = range(M_HEADS)

    @pl.when(pl.program_id(1) == 0)
    def _():
        c_ref[...] = jnp.zeros_like(c_ref)
        n_ref[...] = jnp.zeros_like(n_ref)
        m_ref[...] = jnp.zeros_like(m_ref)

    ri = lax.broadcasted_iota(jnp.int32, (lc, lc), 0)
    ci = lax.broadcasted_iota(jnp.int32, (lc, lc), 1)
    causal = ci <= ri
    diag = ci == ri
    g_rows = gate_ref[0]
    tri_u = jnp.where(ri <= ci, 1.0, 0.0).astype(BF16)
    cum_rows = _dot(jnp.concatenate(_split3_bf16(g_rows), axis=1), jnp.concatenate([tri_u] * 3, axis=0))
    scale = M_DK ** -0.5
    q_all = q_ref[0]
    k_all = k_ref[0]
    v_all = v_ref[0]
    qb = [q_all[:, h * M_DK:(h + 1) * M_DK] for h in heads]
    kb = [k_all[:, h * M_DK:(h + 1) * M_DK] for h in heads]
    vb = [v_all[:, h * M_DV:(h + 1) * M_DV] for h in heads]
    q = [qb[h].astype(F32) for h in heads]
    k = [kb[h].astype(F32) for h in heads]
    i_row = [g_rows[h:h + 1, :] for h in heads]
    f_row = [g_rows[M_HEADS + h:M_HEADS + h + 1, :] for h in heads]
    bcum_row = [cum_rows[M_HEADS + h:M_HEADS + h + 1, :] for h in heads]
    bcum_col = [jnp.sum(jnp.where(causal, f_row[h], 0.0), axis=-1, keepdims=True) for h in heads]
    i_col = [jnp.sum(jnp.where(diag, i_row[h], 0.0), axis=-1, keepdims=True) for h in heads]
    m_prev = [m_ref[h, 0:1, 0:1] for h in heads]
    c_prev = [c_ref[h] for h in heads]
    n_prev = [n_ref[h, 0:1, :] for h in heads]

    log_d = [jnp.where(causal, bcum_col[h] - bcum_row[h] + i_row[h], -jnp.inf) for h in heads]
    log_inter = [bcum_col[h] + m_prev[h] for h in heads]
    m_row = [jnp.maximum(jnp.max(log_d[h], axis=-1, keepdims=True), log_inter[h]) for h in heads]
    inter = [jnp.exp(log_inter[h] - m_row[h]) for h in heads]
    s = [_dot_nt(qb[h], kb[h]) * (scale * jnp.exp(log_d[h] - m_row[h])) for h in heads]
    inter_s = [scale * inter[h] for h in heads]
    qc = [_dot(qb[h], c_prev[h].astype(BF16)) for h in heads]
    num = [_dot(s[h].astype(BF16), vb[h]) + inter_s[h] * qc[h] for h in heads]
    den = [jnp.sum(s[h], axis=-1, keepdims=True) + inter_s[h] * jnp.sum(q[h] * n_prev[h], axis=-1, keepdims=True)
           for h in heads]
    hc = [num[h] / jnp.maximum(jnp.abs(den[h]), jnp.exp(-m_row[h])) for h in heads]

    b_last = [bcum_col[h][lc - 1:lc, :] for h in heads]
    log_w = [b_last[h] - bcum_col[h] + i_col[h] for h in heads]
    m_new = [jnp.maximum(b_last[h] + m_prev[h], jnp.max(log_w[h], axis=0, keepdims=True)) for h in heads]
    decay = [jnp.exp(b_last[h] + m_prev[h] - m_new[h]) for h in heads]
    kw = [k[h] * jnp.exp(log_w[h] - m_new[h]) for h in heads]
    for h in heads:
        c_ref[h] = decay[h] * c_prev[h] + _dot_tn(kw[h].astype(BF16), vb[h])
        n_ref[h] = jnp.broadcast_to(decay[h] * n_prev[h] + jnp.sum(kw[h], axis=0, keepdims=True), n_ref.shape[1:])
        m_ref[h] = jnp.broadcast_to(m_new[h], m_ref.shape[1:])

    hn = jnp.concatenate([hc[h] * lax.rsqrt(jnp.mean(jnp.square(hc[h]), axis=-1, keepdims=True) + 1e-6)
                          for h in heads], axis=1)
    h_ref[0] = (hn * ng_ref[...] * _sigmoid(o_ref[0].astype(F32))).astype(h_ref.dtype)


def _mlstm_core(proj, gates, norm_g):
    bsz, seq, _ = proj.shape
    tg = gates.shape[-1]
    lc = min(M_CHUNK, seq)
    nc = seq // lc
    assert tg % lc == 0 and seq % tg == 0
    per_tile = tg // lc
    hdk = M_HEADS * M_DK
    hdv = M_HEADS * M_DV
    assert 2 * hdk == hdv
    return pl.pallas_call(
        _mlstm_body,
        grid=(bsz, nc),
        in_specs=[pl.BlockSpec((1, lc, hdk), lambda b, c: (b, c, 0)),
                  pl.BlockSpec((1, lc, hdk), lambda b, c: (b, c, 1)),
                  pl.BlockSpec((1, lc, hdv), lambda b, c: (b, c, 1)),
                  pl.BlockSpec((1, lc, hdv), lambda b, c: (b, c, 2)),
                  pl.BlockSpec((1, 2 * M_HEADS, lc), lambda b, c: (b * (seq // tg) + c // per_tile, 0, c % per_tile)),
                  pl.BlockSpec((1, hdv), lambda b, c: (0, 0))],
        out_specs=pl.BlockSpec((1, lc, hdv), lambda b, c: (b, c, 0)),
        out_shape=jax.ShapeDtypeStruct((bsz, seq, hdv), BF16),
        scratch_shapes=[pltpu.VMEM((M_HEADS, M_DK, M_DV), F32),
                        pltpu.VMEM((M_HEADS, 8, M_DK), F32),
                        pltpu.VMEM((M_HEADS, 8, 128), F32)],
        compiler_params=_cparams(("parallel", "arbitrary")),
        name="mlstm_chunkwise",
    )(proj, proj, proj, proj, gates, norm_g.reshape(1, hdv))


def _mlstm_mixer(x, w_in, b_i, b_f, conv_w, conv_b, norm_g):
    bsz, seq, d = x.shape
    m = bsz * seq
    hdv = M_HEADS * M_DV
    n_main = 2 * M_HEADS * M_DK + 2 * hdv
    xf = x.reshape(m, d)
    pad = 128 - 2 * M_HEADS
    w_gate = jnp.pad(w_in[:, n_main:], ((0, 0), (0, pad))).astype(BF16)
    b_gate = jnp.pad(jnp.concatenate([b_i, b_f]), (0, pad))
    proj, gates = _proj_conv_silu(xf, w_in[:, :n_main].astype(BF16), w_gate, b_gate, conv_w, conv_b,
                                  seq=seq, tm=1024, col_chunk=256)
    proj = proj.reshape(bsz, seq, n_main)
    return _mlstm_core(proj, gates, norm_g)


def _wkv_body(r_ref, k_ref, v_ref, tail_ref, w2_ref, a2_ref, g2_ref, w0_ref, a0_ref, kk_ref, ka_ref, rk_ref,
              gng_ref, gnb_ref, o_ref, s_ref, *, n_tanh):
    tb = r_ref.shape[1]
    lc = min(R_CHUNK, tb)
    chunks = range(tb // lc)
    width = r_ref.shape[2]
    gw = R_GROUP * R_N
    sw = R_STATE_HEADS * R_N
    rows = R_GROUP * lc
    assert tb % lc == 0 and gw % sw == 0 and width % gw == 0

    @pl.when(pl.program_id(2) == 0)
    def _():
        s_ref[...] = jnp.zeros_like(s_ref)

    groups = range(width // gw)
    tiles = range(width // sw)
    tls = [slice(j * sw, (j + 1) * sw) for j in tiles]
    brow = lax.broadcasted_iota(jnp.int32, (sw, sw), 0) // R_N
    bcol = lax.broadcasted_iota(jnp.int32, (sw, sw), 1) // R_N
    same_head = brow == bcol
    head_ones = jnp.where(same_head, 1.0, 0.0).astype(BF16)

    def seg_sum(x, parts=1):
        pieces = (x.astype(BF16),) if parts == 1 else _split_bf16(x)
        ones = jnp.concatenate([head_ones] * len(pieces), axis=0)
        return jnp.concatenate([_dot(jnp.concatenate([piece[:, sl] for piece in pieces], axis=1), ones)
                                for sl in tls], axis=1)

    r = r_ref[0].astype(F32)
    k = k_ref[0].astype(F32)
    v = v_ref[0].astype(F32)
    n_wa = w2_ref.shape[0]
    tail = tail_ref[0].astype(F32)
    t_wa = tail[:, :n_wa]
    is_tanh = lax.broadcasted_iota(jnp.int32, (1, n_wa), 1) < n_tanh
    f_wa = jnp.where(is_tanh, jnp.tanh(t_wa), t_wa).astype(BF16)
    gate = _dot(_sigmoid(tail[:, n_wa:]).astype(BF16), g2_ref[...])
    ld = (-R_DECAY_SCALE) * _sigmoid(w0_ref[...] + _dot(f_wa, w2_ref[...]))
    a_lr = _sigmoid(a0_ref[...] + _dot(f_wa, a2_ref[...]))
    kk = k * kk_ref[...]
    kk = kk * lax.rsqrt(jnp.maximum(seg_sum(jnp.square(kk)), 1e-24))
    k = k * (a_lr * ka_ref[...] + (1.0 - ka_ref[...]))
    b_ = kk * a_lr

    ti = lax.broadcasted_iota(jnp.int32, (tb, tb), 0)
    tj = lax.broadcasted_iota(jnp.int32, (tb, tb), 1)
    in_chunk_tri = jnp.logical_and(tj <= ti, ti // lc == tj // lc)
    tri = jnp.where(in_chunk_tri, 1.0, 0.0).astype(BF16)
    cum = _dot(jnp.concatenate([tri, tri], axis=1), jnp.concatenate(_split_bf16(ld), axis=0))
    e_pos = jnp.exp(cum)
    e_neg = jnp.exp(-cum)
    w_end = [jnp.exp(cum[(c + 1) * lc - 1:(c + 1) * lc, :]) for c in chunks]
    a_til = -kk * jnp.exp(cum - ld)
    r_til = r * e_pos
    b_til = b_ * e_neg
    k_til = k * e_neg

    def stacker(n):
        lane_head = lax.broadcasted_iota(jnp.int32, (lc, n), 1) // (n // R_GROUP)
        sel = [lane_head == h for h in range(R_GROUP)]
        return lambda x: jnp.concatenate([jnp.where(s_, x, jnp.zeros_like(x)) for s_ in sel], axis=0)

    stack_c = stacker(gw)
    stack_t = stacker(rows)
    trow = lax.broadcasted_iota(jnp.int32, (lc, rows), 0)
    tcol = lax.broadcasted_iota(jnp.int32, (lc, rows), 1) % lc
    strict = tcol < trow
    causal = tcol <= trow
    eye = jnp.where(tcol == trow, 1.0, 0.0).astype(F32)
    level_masks = []
    s = 1
    while s < lc:
        lo, hi = s.bit_length() - 1, s.bit_length()
        level_masks.append(jnp.logical_and((trow >> hi) == (tcol >> hi), (trow >> lo) != (tcol >> lo)))
        s *= 2

    units = [(c, g) for c in chunks for g in groups]
    tile = lambda x, c, g: x[c * lc:(c + 1) * lc, g * gw:(g + 1) * gw]
    ar_b = {u: jnp.concatenate([tile(a_til, *u), tile(r_til, *u)], axis=0).astype(BF16) for u in units}
    bk_b = {u: (tile(b_til, *u).astype(BF16), tile(k_til, *u).astype(BF16)) for u in units}
    bk_s = {u: jnp.concatenate([stack_c(bk_b[u][0]), stack_c(bk_b[u][1])], axis=0) for u in units}
    v_b = {u: tile(v, *u).astype(BF16) for u in units}
    v_s = {u: stack_c(v_b[u]) for u in units}
    bk_cat = {u: jnp.concatenate(bk_b[u], axis=0) for u in units}

    pm = {u: _dot_nt(ar_b[u], bk_s[u]) for u in units}
    n_ab = {u: jnp.where(strict, pm[u][:lc, :rows], 0.0) for u in units}
    m_xk = {u: jnp.concatenate([jnp.where(strict, pm[u][:lc, rows:], 0.0),
                                jnp.where(causal, pm[u][lc:, rows:], 0.0)], axis=0).astype(BF16) for u in units}
    m_rb = {u: jnp.where(causal, pm[u][lc:, :rows], 0.0).astype(BF16) for u in units}

    n_b = {u: n_ab[u].astype(BF16) for u in units}
    zero_b = jnp.zeros((lc, rows), BF16)
    t_inv = {u: eye + jnp.where(level_masks[0], n_ab[u], 0.0) for u in units}
    for msk in level_masks[1:]:
        t_b = {u: t_inv[u].astype(BF16) for u in units}
        half = {u: _dot(t_b[u], stack_t(jnp.where(msk, n_b[u], zero_b))) for u in units}
        t_inv = {u: t_inv[u] + _dot(half[u].astype(BF16), stack_t(t_b[u])) for u in units}
    t_b = {u: t_inv[u].astype(BF16) for u in units}
    resid = {u: eye - t_b[u].astype(F32) + _dot(n_b[u], stack_t(t_b[u])) for u in units}
    t_fix = {u: _dot(t_b[u], stack_t(resid[u].astype(BF16))).astype(BF16) for u in units}

    state = [s_ref[j] for j in tiles]
    tiles_of = lambda g: range(g * (gw // sw), (g + 1) * (gw // sw))
    sub = lambda x, g, j: x[:, (j - g * (gw // sw)) * sw:(j - g * (gw // sw) + 1) * sw]
    y_rows = []
    for c in chunks:
        us = [(c, g) for g in groups]
        zy = [jnp.concatenate([_dot_nt(sub(ar_b[u], u[1], j), state[j].astype(BF16)) for j in tiles_of(u[1])], axis=1)
              + _dot(m_xk[u], v_s[u]) for u in us]
        ub = []
        for i, u in enumerate(us):
            z_s = stack_c(zy[i][:lc].astype(BF16))
            ub.append(_dot(jnp.concatenate([t_b[u], t_fix[u]], axis=1), jnp.concatenate([z_s, z_s], axis=0)).astype(BF16))
        y_rows.append(jnp.concatenate([zy[i][lc:] + _dot(m_rb[u], stack_c(ub[i])) for i, u in enumerate(us)], axis=1))
        for i, u in enumerate(us):
            for j in tiles_of(u[1]):
                upd = _dot_tn(jnp.concatenate([sub(ub[i], u[1], j), sub(v_b[u], u[1], j)], axis=0),
                              sub(bk_cat[u], u[1], j))
                state[j] = (state[j] + jnp.where(same_head, upd, 0.0)) * w_end[c][:, tls[j]]
    for j in tiles:
        s_ref[j] = state[j]
    y = jnp.concatenate(y_rows, axis=0)

    mu = seg_sum(y, parts=2) * (1.0 / R_N)
    yc = y - mu
    var = seg_sum(jnp.square(yc)) * (1.0 / R_N)
    yn = yc * lax.rsqrt(var + R_GN_EPS) * gng_ref[...] + gnb_ref[...]
    bonus = seg_sum(r * k * rk_ref[...]) * v
    o_ref[0] = ((yn + bonus) * gate).astype(o_ref.dtype)


def _wkv_core(proj, w2, a2, g2, w0, a0, k_k, k_a, r_k, gn_g, gn_b):
    bsz, seq, n_proj = proj.shape
    d = w0.shape[-1]
    lw, la, lg = w2.shape[0], a2.shape[0], g2.shape[0]
    n_tail = lw + la + lg
    assert n_proj == 3 * d + n_tail and (3 * d) % n_tail == 0
    gw = R_GROUP * R_N
    sw = R_STATE_HEADS * R_N
    width = R_GROUPS_PER_STEP * gw
    nb = d // width
    tb = min(R_CHUNK * R_CHUNKS_PER_STEP, seq)
    nc = seq // tb
    w2p = jnp.concatenate([w2, jnp.zeros_like(a2)], axis=0).astype(BF16)
    a2p = jnp.concatenate([jnp.zeros_like(w2), a2], axis=0).astype(BF16)
    tile = lambda off: pl.BlockSpec((1, tb, width), lambda b, g, c: (b, c, off * nb + g))
    tail = pl.BlockSpec((1, tb, n_tail), lambda b, g, c: (b, c, 3 * d // n_tail))
    wa = pl.BlockSpec((lw + la, width), lambda b, g, c: (0, g))
    wg = pl.BlockSpec((lg, width), lambda b, g, c: (0, g))
    par = pl.BlockSpec((1, width), lambda b, g, c: (0, g))
    row = lambda p: p.reshape(1, d)
    return pl.pallas_call(
        functools.partial(_wkv_body, n_tanh=lw),
        grid=(bsz, nb, nc),
        in_specs=[tile(0), tile(1), tile(2), tail, wa, wa, wg] + [par] * 7,
        out_specs=pl.BlockSpec((1, tb, width), lambda b, g, c: (b, c, g)),
        out_shape=jax.ShapeDtypeStruct((bsz, seq, d), BF16),
        scratch_shapes=[pltpu.VMEM((width // sw, sw, sw), F32)],
        compiler_params=_cparams(("parallel", "parallel", "arbitrary")),
        name="wkv7_chunked",
    )(proj, proj, proj, proj, w2p, a2p, g2.astype(BF16),
      row(w0), row(a0), row(k_k), row(k_a), row(r_k), row(gn_g), row(gn_b))


def _rwkv_mixer(x, w_in, mu, w0, w2, a0, a2, g2, k_k, k_a, r_k, gn_g, gn_b):
    bsz, seq, d = x.shape
    n_proj = w_in.shape[1]
    proj = _proj_shift_lerp(x.reshape(bsz * seq, d), w_in.astype(BF16), mu, seq=seq, tm=1024, col_chunk=256)
    return _wkv_core(proj.reshape(bsz, seq, n_proj), w2, a2, g2, w0, a0, k_k, k_a, r_k, gn_g, gn_b)


def kernel(x, mlstm_w_in, mlstm_b_i, mlstm_b_f, mlstm_conv_w, mlstm_conv_b, mlstm_norm_g, mlstm_w_out, rwkv_w_in, rwkv_mu, rwkv_w0, rwkv_w2, rwkv_a0, rwkv_a2, rwkv_g2, rwkv_k_k, rwkv_k_a, rwkv_r_k, rwkv_gn_g, rwkv_gn_b, rwkv_w_out, ln_mix_g, ln_mix_b, mlp_w1, mlp_w2, ln_ffn_g, ln_ffn_b):
    bsz, seq, d = x.shape
    m = bsz * seq
    xf = x.reshape(m, d)
    for layer in range(DEPTH):
        j = layer // 2
        if layer % 2 == 0:
            mix = _mlstm_mixer(xf.reshape(bsz, seq, d), mlstm_w_in[j], mlstm_b_i[j], mlstm_b_f[j],
                               mlstm_conv_w[j], mlstm_conv_b[j], mlstm_norm_g[j])
            w_out = mlstm_w_out[j]
        else:
            mix = _rwkv_mixer(xf.reshape(bsz, seq, d), rwkv_w_in[j], rwkv_mu[j], rwkv_w0[j], rwkv_w2[j],
                              rwkv_a0[j], rwkv_a2[j], rwkv_g2[j], rwkv_k_k[j], rwkv_k_a[j],
                              rwkv_r_k[j].reshape(-1), rwkv_gn_g[j], rwkv_gn_b[j])
            w_out = rwkv_w_out[j]
        xf = _mix_ffn(mix.reshape(m, -1), w_out.astype(BF16), xf, ln_mix_g[layer], ln_mix_b[layer],
                      mlp_w1[layer].astype(BF16), mlp_w2[layer].astype(BF16), ln_ffn_g[layer], ln_ffn_b[layer],
                      tm=1024, ff_chunk=1024, row_split=4)
    return xf.reshape(bsz, seq, d)
```

```python
import functools
import math

import jax
import jax.numpy as jnp
from jax import lax
from jax.experimental import pallas as pl
from jax.experimental.pallas import tpu as pltpu

F32 = jnp.float32
BF16 = jnp.bfloat16

DEPTH = 2
DN_ALPHA = (2.0 * DEPTH) ** 0.25
LN_EPS = 1e-5

M_HEADS = 4
M_DK = 128
M_DV = 256
M_GATE_CAP = 15.0
M_CHUNK = 256

R_N = 64
R_GN_EPS = 64e-5
R_DECAY_SCALE = math.exp(-0.5)
R_CHUNK = 64
R_GROUP = 2
R_STATE_HEADS = 2
R_GROUPS_PER_STEP = 8
R_CHUNKS_PER_STEP = 4

VMEM_LIMIT = 48 * 1024 * 1024


def _cparams(sem):
    return pltpu.CompilerParams(dimension_semantics=sem, vmem_limit_bytes=VMEM_LIMIT)


def _dot(a, b, precision=None):
    return jnp.dot(a, b, preferred_element_type=F32, precision=precision)


def _dot_nt(a, b, precision=None):
    return lax.dot_general(a, b, (((1,), (1,)), ((), ())), preferred_element_type=F32, precision=precision)


def _dot_tn(a, b, precision=None):
    return lax.dot_general(a, b, (((0,), (0,)), ((), ())), preferred_element_type=F32, precision=precision)


def _softplus(z):
    return jnp.maximum(z, 0.0) + jnp.log1p(jnp.exp(-jnp.abs(z)))


def _sigmoid(z):
    return 0.5 + 0.5 * jnp.tanh(0.5 * z)


def _split_bf16(x):
    hi = x.astype(BF16)
    return hi, (x - hi.astype(F32)).astype(BF16)


def _split3_bf16(x):
    hi, rest = x.astype(BF16), x
    rest = rest - hi.astype(F32)
    mid = rest.astype(BF16)
    return hi, mid, (rest - mid.astype(F32)).astype(BF16)


def _proj_lerp_body(x_ref, w_ref, mu_ref, o_ref, carry_ref, *, tiles_per_seq, col_chunk):
    @pl.when(pl.program_id(0) % tiles_per_seq == 0)
    def _():
        carry_ref[...] = jnp.zeros_like(carry_ref)

    xb = x_ref[...].astype(BF16)
    tm = xb.shape[0]
    first = lax.broadcasted_iota(jnp.int32, (8, col_chunk), 0) == 0

    def epilogue(acc, cols):
        rolled = pltpu.roll(acc, 1, 0)
        top = jnp.where(first, carry_ref[0:1, cols], rolled[0:8])
        shifted = jnp.concatenate([top, rolled[8:]], axis=0)
        o_ref[:, cols] = (acc + mu_ref[:, cols] * (shifted - acc)).astype(o_ref.dtype)
        carry_ref[:, cols] = jnp.broadcast_to(acc[tm - 1:tm, :], (8, col_chunk))

    chunks = [slice(c0, c0 + col_chunk) for c0 in range(0, w_ref.shape[1], col_chunk)]
    pending = None
    for cols in chunks:
        acc = _dot(xb, w_ref[:, cols])
        if pending is not None:
            epilogue(*pending)
        pending = (acc, cols)
    epilogue(*pending)


def _proj_shift_lerp(x, w, mu, *, seq, tm, col_chunk):
    m, k = x.shape
    n = w.shape[1]
    tm = min(tm, seq)
    assert seq % tm == 0 and n % col_chunk == 0
    return pl.pallas_call(
        functools.partial(_proj_lerp_body, tiles_per_seq=seq // tm, col_chunk=col_chunk),
        grid=(m // tm,),
        in_specs=[pl.BlockSpec((tm, k), lambda i: (i, 0)),
                  pl.BlockSpec((k, n), lambda i: (0, 0), pipeline_mode=pl.Buffered(1)),
                  pl.BlockSpec((1, n), lambda i: (0, 0), pipeline_mode=pl.Buffered(1))],
        out_specs=pl.BlockSpec((tm, n), lambda i: (i, 0)),
        out_shape=jax.ShapeDtypeStruct((m, n), F32),
        scratch_shapes=[pltpu.VMEM((8, n), F32)],
        compiler_params=_cparams(("arbitrary",)),
        name="proj_token_shift",
    )(x, w, mu.reshape(1, n))


def _proj_conv_body(x_ref, w_ref, wg_ref, gb_ref, cw_ref, cb_ref, o_ref, g_ref, carry_ref, *,
                    tiles_per_seq, col_chunk):
    @pl.when(pl.program_id(0) % tiles_per_seq == 0)
    def _():
        carry_ref[...] = jnp.zeros_like(carry_ref)

    xb = x_ref[...].astype(BF16)
    tm = xb.shape[0]
    taps, n_conv = cw_ref.shape

    def epilogue(acc, cols):
        if cols.start < n_conv:
            ext = jnp.concatenate([carry_ref[:, cols], acc], axis=0)
            conv = cw_ref[0:1, cols] * ext
            for tap in range(1, taps):
                conv = pltpu.roll(conv, 1, 0) + cw_ref[tap:tap + 1, cols] * ext
            conv = conv[8:] + cb_ref[:, cols]
            o_ref[:, cols] = (conv * _sigmoid(conv)).astype(o_ref.dtype)
            carry_ref[:, cols] = acc[tm - 8:tm]
        else:
            o_ref[:, cols] = acc.astype(o_ref.dtype)

    chunks = [slice(c0, c0 + col_chunk) for c0 in range(0, w_ref.shape[1], col_chunk)]
    heavy = [c for c in chunks if c.start < n_conv]
    light = [c for c in chunks if c.start >= n_conv]
    per_heavy = len(light) // max(len(heavy), 1)
    order = []
    for i, c in enumerate(heavy):
        order += [c] + light[i * per_heavy:(i + 1) * per_heavy]
    order += light[len(heavy) * per_heavy:]
    pending = None
    for cols in order:
        acc = _dot(xb, w_ref[:, cols])
        if pending is not None:
            epilogue(*pending)
        pending = (acc, cols)
    capped = M_GATE_CAP * jnp.tanh((_dot(xb, wg_ref[...]) + gb_ref[...]) * (1.0 / M_GATE_CAP))
    is_input_gate = lax.broadcasted_iota(jnp.int32, capped.shape, 1) < M_HEADS
    log_gate = jnp.where(is_input_gate, capped, -_softplus(-capped))
    g_ref[0] = log_gate.T[0:2 * M_HEADS, :]
    epilogue(*pending)


def _proj_conv_silu(x, w, w_gate, b_gate, conv_w, conv_b, *, seq, tm, col_chunk):
    m, k = x.shape
    n = w.shape[1]
    n_gate = w_gate.shape[1]
    assert 2 * M_HEADS == 8 and n_gate % 128 == 0
    taps, n_conv = conv_w.shape
    tm = min(tm, seq)
    assert seq % tm == 0 and n % col_chunk == 0 and n_conv % col_chunk == 0 and taps <= 8
    resident = lambda shape: pl.BlockSpec(shape, lambda i: (0, 0), pipeline_mode=pl.Buffered(1))
    return pl.pallas_call(
        functools.partial(_proj_conv_body, tiles_per_seq=seq // tm, col_chunk=col_chunk),
        grid=(m // tm,),
        in_specs=[pl.BlockSpec((tm, k), lambda i: (i, 0)), resident((k, n)), resident((k, n_gate)),
                  resident((1, n_gate)),
                  resident((taps, n_conv)), resident((1, n_conv))],
        out_specs=[pl.BlockSpec((tm, n), lambda i: (i, 0)), pl.BlockSpec((1, 2 * M_HEADS, tm), lambda i: (i, 0, 0))],
        out_shape=[jax.ShapeDtypeStruct((m, n), BF16), jax.ShapeDtypeStruct((m // tm, 2 * M_HEADS, tm), F32)],
        scratch_shapes=[pltpu.VMEM((8, n_conv), F32)],
        compiler_params=_cparams(("arbitrary",)),
        name="proj_conv_silu",
    )(x, w, w_gate, b_gate.reshape(1, n_gate), conv_w, conv_b.reshape(1, n_conv))


def _layer_norm(y, g, b):
    mu = jnp.mean(y, axis=-1, keepdims=True)
    yc = y - mu
    var = jnp.mean(jnp.square(yc), axis=-1, keepdims=True)
    return yc * lax.rsqrt(var + LN_EPS) * g + b


def _mix_ffn_body(mix_ref, wo_ref, res_ref, g1_ref, b1_ref, w1_ref, w2_ref, g2_ref, b2_ref, o_ref, *,
                  ff_chunk, row_split):
    sub = mix_ref.shape[0] // row_split

    def stages(rows):
        st = {}

        def out_proj():
            st["y"] = DN_ALPHA * res_ref[rows, :] + _dot(mix_ref[rows, :], wo_ref[...])

        def norm1():
            st["x1"] = _layer_norm(st.pop("y"), g1_ref[...], b1_ref[...])
            st["x1b"] = st["x1"].astype(BF16)
            st["acc"] = DN_ALPHA * st.pop("x1")

        def up(f):
            st["hid"] = _dot(st["x1b"], w1_ref[:, f:f + ff_chunk])

        def act():
            st["hid"] = jnp.square(jnp.maximum(st["hid"], 0.0)).astype(BF16)

        def down(f):
            st["acc"] = st["acc"] + _dot(st.pop("hid"), w2_ref[f:f + ff_chunk, :])

        def norm2():
            o_ref[rows, :] = _layer_norm(st.pop("acc"), g2_ref[...], b2_ref[...])

        seq = [out_proj, norm1]
        for f in range(0, w1_ref.shape[1], ff_chunk):
            seq += [functools.partial(up, f), act, functools.partial(down, f)]
        return seq + [norm2]

    threads = [stages(slice(h * sub, (h + 1) * sub)) for h in range(row_split)]
    for t in range(len(threads[0]) + row_split - 1):
        for h, seq in enumerate(threads):
            if 0 <= t - h < len(seq):
                seq[t - h]()


def _mix_ffn(mix, w_out, res, g1, b1, w1_all, w2_all, layer, g2, b2, *, tm, ff_chunk, row_split):
    m, k = mix.shape
    d = w_out.shape[1]
    dff = w1_all.shape[2]
    tm = min(tm, m)
    assert m % tm == 0 and dff % ff_chunk == 0 and tm % (8 * row_split) == 0
    resident = lambda shape: pl.BlockSpec(shape, lambda i: (0, 0), pipeline_mode=pl.Buffered(1))
    of_layer = lambda shape: pl.BlockSpec((None,) + shape, lambda i: (layer, 0, 0), pipeline_mode=pl.Buffered(1))
    rows = lambda width: pl.BlockSpec((tm, width), lambda i: (i, 0))
    vec = lambda p: p.reshape(1, d)
    return pl.pallas_call(
        functools.partial(_mix_ffn_body, ff_chunk=ff_chunk, row_split=row_split),
        grid=(m // tm,),
        in_specs=[rows(k), resident((k, d)), rows(d), resident((1, d)), resident((1, d)),
                  of_layer((d, dff)), of_layer((dff, d)), resident((1, d)), resident((1, d))],
        out_specs=rows(d),
        out_shape=jax.ShapeDtypeStruct((m, d), F32),
        compiler_params=_cparams(("parallel",)),
        name="outproj_ln_ffn_ln",
    )(mix, w_out, res, vec(g1), vec(b1), w1_all, w2_all, vec(g2), vec(b2))


def _mlstm_body(q_ref, k_ref, v_ref, o_ref, gate_ref, ng_ref, h_ref, c_ref, n_ref, m_ref):
    lc = q_ref.shape[1]
    heads = range(M_HEADS)

    @pl.when(pl.program_id(1) == 0)
    def _():
        c_ref[...] = jnp.zeros_like(c_ref)
        n_ref[...] = jnp.zeros_like(n_ref)
        m_ref[...] = jnp.zeros_like(m_ref)

    ri = lax.broadcasted_iota(jnp.int32, (lc, lc), 0)
    ci = lax.broadcasted_iota(jnp.int32, (lc, lc), 1)
    causal = ci <= ri
    diag = ci == ri
    g_rows = gate_ref[0]
    tri_u = jnp.where(ri <= ci, 1.0, 0.0).astype(BF16)
    cum_rows = _dot(jnp.concatenate(_split3_bf16(g_rows), axis=1), jnp.concatenate([tri_u] * 3, axis=0))
    scale = M_DK ** -0.5
    q_all = q_ref[0]
    k_all = k_ref[0]
    v_all = v_ref[0]
    qb = [q_all[:, h * M_DK:(h + 1) * M_DK] for h in heads]
    kb = [k_all[:, h * M_DK:(h + 1) * M_DK] for h in heads]
    vb = [v_all[:, h * M_DV:(h + 1) * M_DV] for h in heads]
    q = [qb[h].astype(F32) for h in heads]
    k = [kb[h].astype(F32) for h in heads]
    i_row = [g_rows[h:h + 1, :] for h in heads]
    f_row = [g_rows[M_HEADS + h:M_HEADS + h + 1, :] for h in heads]
    bcum_row = [cum_rows[M_HEADS + h:M_HEADS + h + 1, :] for h in heads]
    bcum_col = [jnp.sum(jnp.where(causal, f_row[h], 0.0), axis=-1, keepdims=True) for h in heads]
    i_col = [jnp.sum(jnp.where(diag, i_row[h], 0.0), axis=-1, keepdims=True) for h in heads]
    m_prev = [m_ref[h, 0:1, 0:1] for h in heads]
    c_prev = [c_ref[h] for h in heads]
    n_prev = [n_ref[h, 0:1, :] for h in heads]

    log_d = [jnp.where(causal, bcum_col[h] - bcum_row[h] + i_row[h], -jnp.inf) for h in heads]
    log_inter = [bcum_col[h] + m_prev[h] for h in heads]
    m_row = [jnp.maximum(jnp.max(log_d[h], axis=-1, keepdims=True), log_inter[h]) for h in heads]
    inter = [jnp.exp(log_inter[h] - m_row[h]) for h in heads]
    s = [_dot_nt(qb[h], kb[h]) * (scale * jnp.exp(log_d[h] - m_row[h])) for h in heads]
    inter_s = [scale * inter[h] for h in heads]
    qc = [_dot(qb[h], c_prev[h].astype(BF16)) for h in heads]
    num = [_dot(s[h].astype(BF16), vb[h]) + inter_s[h] * qc[h] for h in heads]
    den = [jnp.sum(s[h], axis=-1, keepdims=True) + inter_s[h] * jnp.sum(q[h] * n_prev[h], axis=-1, keepdims=True)
           for h in heads]
    hc = [num[h] / jnp.maximum(jnp.abs(den[h]), jnp.exp(-m_row[h])) for h in heads]

    b_last = [bcum_col[h][lc - 1:lc, :] for h in heads]
    log_w = [b_last[h] - bcum_col[h] + i_col[h] for h in heads]
    m_new = [jnp.maximum(b_last[h] + m_prev[h], jnp.max(log_w[h], axis=0, keepdims=True)) for h in heads]
    decay = [jnp.exp(b_last[h] + m_prev[h] - m_new[h]) for h in heads]
    kw = [k[h] * jnp.exp(log_w[h] - m_new[h]) for h in heads]
    for h in heads:
        c_ref[h] = decay[h] * c_prev[h] + _dot_tn(kw[h].astype(BF16), vb[h])
        n_ref[h] = jnp.broadcast_to(decay[h] * n_prev[h] + jnp.sum(kw[h], axis=0, keepdims=True), n_ref.shape[1:])
        m_ref[h] = jnp.broadcast_to(m_new[h], m_ref.shape[1:])

    hn = jnp.concatenate([hc[h] * lax.rsqrt(jnp.mean(jnp.square(hc[h]), axis=-1, keepdims=True) + 1e-6)
                          for h in heads], axis=1)
    h_ref[0] = (hn * ng_ref[...] * _sigmoid(o_ref[0].astype(F32))).astype(h_ref.dtype)


def _mlstm_core(proj, gates, norm_g):
    bsz, seq, _ = proj.shape
    tg = gates.shape[-1]
    lc = min(M_CHUNK, seq)
    nc = seq // lc
    assert tg % lc == 0 and seq % tg == 0
    per_tile = tg // lc
    hdk = M_HEADS * M_DK
    hdv = M_HEADS * M_DV
    assert 2 * hdk == hdv
    return pl.pallas_call(
        _mlstm_body,
        grid=(bsz, nc),
        in_specs=[pl.BlockSpec((1, lc, hdk), lambda b, c: (b, c, 0)),
                  pl.BlockSpec((1, lc, hdk), lambda b, c: (b, c, 1)),
                  pl.BlockSpec((1, lc, hdv), lambda b, c: (b, c, 1)),
                  pl.BlockSpec((1, lc, hdv), lambda b, c: (b, c, 2)),
                  pl.BlockSpec((1, 2 * M_HEADS, lc), lambda b, c: (b * (seq // tg) + c // per_tile, 0, c % per_tile)),
                  pl.BlockSpec((1, hdv), lambda b, c: (0, 0))],
        out_specs=pl.BlockSpec((1, lc, hdv), lambda b, c: (b, c, 0)),
        out_shape=jax.ShapeDtypeStruct((bsz, seq, hdv), BF16),
        scratch_shapes=[pltpu.VMEM((M_HEADS, M_DK, M_DV), F32),
                        pltpu.VMEM((M_HEADS, 8, M_DK), F32),
                        pltpu.VMEM((M_HEADS, 8, 128), F32)],
        compiler_params=_cparams(("parallel", "arbitrary")),
        name="mlstm_chunkwise",
    )(proj, proj, proj, proj, gates, norm_g.reshape(1, hdv))


def _mlstm_mixer(x, w_in, b_i, b_f, conv_w, conv_b, norm_g):
    bsz, seq, d = x.shape
    m = bsz * seq
    hdv = M_HEADS * M_DV
    n_main = 2 * M_HEADS * M_DK + 2 * hdv
    xf = x.reshape(m, d)
    pad = 128 - 2 * M_HEADS
    w_gate = jnp.pad(w_in[:, n_main:], ((0, 0), (0, pad))).astype(BF16)
    b_gate = jnp.pad(jnp.concatenate([b_i, b_f]), (0, pad))
    proj, gates = _proj_conv_silu(xf, w_in[:, :n_main].astype(BF16), w_gate, b_gate, conv_w, conv_b,
                                  seq=seq, tm=1024, col_chunk=256)
    proj = proj.reshape(bsz, seq, n_main)
    return _mlstm_core(proj, gates, norm_g)


def _wkv_body(r_ref, k_ref, v_ref, tail_ref, w2_ref, a2_ref, g2_ref, w0_ref, a0_ref, kk_ref, ka_ref, rk_ref,
              gng_ref, gnb_ref, o_ref, s_ref, *, n_tanh):
    tb = r_ref.shape[1]
    lc = min(R_CHUNK, tb)
    chunks = range(tb // lc)
    width = r_ref.shape[2]
    gw = R_GROUP * R_N
    sw = R_STATE_HEADS * R_N
    rows = R_GROUP * lc
    assert tb % lc == 0 and gw % sw == 0 and width % gw == 0

    @pl.when(pl.program_id(2) == 0)
    def _():
        s_ref[...] = jnp.zeros_like(s_ref)

    groups = range(width // gw)
    tiles = range(width // sw)
    tls = [slice(j * sw, (j + 1) * sw) for j in tiles]
    brow = lax.broadcasted_iota(jnp.int32, (sw, sw), 0) // R_N
    bcol = lax.broadcasted_iota(jnp.int32, (sw, sw), 1) // R_N
    same_head = brow == bcol
    head_ones = jnp.where(same_head, 1.0, 0.0).astype(BF16)

    def seg_sum(x, parts=1):
        pieces = (x.astype(BF16),) if parts == 1 else _split_bf16(x)
        ones = jnp.concatenate([head_ones] * len(pieces), axis=0)
        return jnp.concatenate([_dot(jnp.concatenate([piece[:, sl] for piece in pieces], axis=1), ones)
                                for sl in tls], axis=1)

    r = r_ref[0].astype(F32)
    k = k_ref[0].astype(F32)
    v = v_ref[0].astype(F32)
    n_wa = w2_ref.shape[0]
    tail = tail_ref[0].astype(F32)
    t_wa = tail[:, :n_wa]
    is_tanh = lax.broadcasted_iota(jnp.int32, (1, n_wa), 1) < n_tanh
    f_wa = jnp.where(is_tanh, jnp.tanh(t_wa), t_wa).astype(BF16)
    gate = _dot(_sigmoid(tail[:, n_wa:]).astype(BF16), g2_ref[...])
    ld = (-R_DECAY_SCALE) * _sigmoid(w0_ref[...] + _dot(f_wa, w2_ref[...]))
    a_lr = _sigmoid(a0_ref[...] + _dot(f_wa, a2_ref[...]))
    kk = k * kk_ref[...]
    kk = kk * lax.rsqrt(jnp.maximum(seg_sum(jnp.square(kk)), 1e-24))
    k = k * (a_lr * ka_ref[...] + (1.0 - ka_ref[...]))
    b_ = kk * a_lr

    ti = lax.broadcasted_iota(jnp.int32, (tb, tb), 0)
    tj = lax.broadcasted_iota(jnp.int32, (tb, tb), 1)
    in_chunk_tri = jnp.logical_and(tj <= ti, ti // lc == tj // lc)
    tri = jnp.where(in_chunk_tri, 1.0, 0.0).astype(BF16)
    cum = _dot(jnp.concatenate([tri, tri], axis=1), jnp.concatenate(_split_bf16(ld), axis=0))
    e_pos = jnp.exp(cum)
    e_neg = jnp.exp(-cum)
    w_end = [jnp.exp(cum[(c + 1) * lc - 1:(c + 1) * lc, :]) for c in chunks]
    a_til = -kk * jnp.exp(cum - ld)
    r_til = r * e_pos
    b_til = b_ * e_neg
    k_til = k * e_neg

    def stacker(n):
        lane_head = lax.broadcasted_iota(jnp.int32, (lc, n), 1) // (n // R_GROUP)
        sel = [lane_head == h for h in range(R_GROUP)]
        return lambda x: jnp.concatenate([jnp.where(s_, x, jnp.zeros_like(x)) for s_ in sel], axis=0)

    stack_c = stacker(gw)
    stack_t = stacker(rows)
    trow = lax.broadcasted_iota(jnp.int32, (lc, rows), 0)
    tcol = lax.broadcasted_iota(jnp.int32, (lc, rows), 1) % lc
    strict = tcol < trow
    causal = tcol <= trow
    eye = jnp.where(tcol == trow, 1.0, 0.0).astype(F32)
    level_masks = []
    s = 1
    while s < lc:
        lo, hi = s.bit_length() - 1, s.bit_length()
        level_masks.append(jnp.logical_and((trow >> hi) == (tcol >> hi), (trow >> lo) != (tcol >> lo)))
        s *= 2

    units = [(c, g) for c in chunks for g in groups]
    tile = lambda x, c, g: x[c * lc:(c + 1) * lc, g * gw:(g + 1) * gw]
    ar_b = {u: jnp.concatenate([tile(a_til, *u), tile(r_til, *u)], axis=0).astype(BF16) for u in units}
    bk_b = {u: (tile(b_til, *u).astype(BF16), tile(k_til, *u).astype(BF16)) for u in units}
    bk_s = {u: jnp.concatenate([stack_c(bk_b[u][0]), stack_c(bk_b[u][1])], axis=0) for u in units}
    v_b = {u: tile(v, *u).astype(BF16) for u in units}
    v_s = {u: stack_c(v_b[u]) for u in units}
    bk_cat = {u: jnp.concatenate(bk_b[u], axis=0) for u in units}

    pm = {u: _dot_nt(ar_b[u], bk_s[u]) for u in units}
    n_ab = {u: jnp.where(strict, pm[u][:lc, :rows], 0.0) for u in units}
    m_xk = {u: jnp.concatenate([jnp.where(strict, pm[u][:lc, rows:], 0.0),
                                jnp.where(causal, pm[u][lc:, rows:], 0.0)], axis=0).astype(BF16) for u in units}
    m_rb = {u: jnp.where(causal, pm[u][lc:, :rows], 0.0).astype(BF16) for u in units}

    n_b = {u: n_ab[u].astype(BF16) for u in units}
    zero_b = jnp.zeros((lc, rows), BF16)
    t_inv = {u: eye + jnp.where(level_masks[0], n_ab[u], 0.0) for u in units}
    for msk in level_masks[1:]:
        t_b = {u: t_inv[u].astype(BF16) for u in units}
        half = {u: _dot(t_b[u], stack_t(jnp.where(msk, n_b[u], zero_b))) for u in units}
        t_inv = {u: t_inv[u] + _dot(half[u].astype(BF16), stack_t(t_b[u])) for u in units}
    t_b = {u: t_inv[u].astype(BF16) for u in units}
    resid = {u: eye - t_b[u].astype(F32) + _dot(n_b[u], stack_t(t_b[u])) for u in units}
    t_fix = {u: _dot(t_b[u], stack_t(resid[u].astype(BF16))).astype(BF16) for u in units}

    state = [s_ref[j] for j in tiles]
    tiles_of = lambda g: range(g * (gw // sw), (g + 1) * (gw // sw))
    sub = lambda x, g, j: x[:, (j - g * (gw // sw)) * sw:(j - g * (gw // sw) + 1) * sw]
    y_rows = []
    for c in chunks:
        us = [(c, g) for g in groups]
        zy = [jnp.concatenate([_dot_nt(sub(ar_b[u], u[1], j), state[j].astype(BF16)) for j in tiles_of(u[1])], axis=1)
              + _dot(m_xk[u], v_s[u]) for u in us]
        ub = []
        for i, u in enumerate(us):
            z_s = stack_c(zy[i][:lc].astype(BF16))
            ub.append(_dot(jnp.concatenate([t_b[u], t_fix[u]], axis=1), jnp.concatenate([z_s, z_s], axis=0)).astype(BF16))
        y_rows.append(jnp.concatenate([zy[i][lc:] + _dot(m_rb[u], stack_c(ub[i])) for i, u in enumerate(us)], axis=1))
        for i, u in enumerate(us):
            for j in tiles_of(u[1]):
                upd = _dot_tn(jnp.concatenate([sub(ub[i], u[1], j), sub(v_b[u], u[1], j)], axis=0),
                              sub(bk_cat[u], u[1], j))
                state[j] = (state[j] + jnp.where(same_head, upd, 0.0)) * w_end[c][:, tls[j]]
    for j in tiles:
        s_ref[j] = state[j]
    y = jnp.concatenate(y_rows, axis=0)

    mu = seg_sum(y, parts=2) * (1.0 / R_N)
    yc = y - mu
    var = seg_sum(jnp.square(yc)) * (1.0 / R_N)
    yn = yc * lax.rsqrt(var + R_GN_EPS) * gng_ref[...] + gnb_ref[...]
    bonus = seg_sum(r * k * rk_ref[...]) * v
    o_ref[0] = ((yn + bonus) * gate).astype(o_ref.dtype)


def _wkv_core(proj, w2, a2, g2, w0, a0, k_k, k_a, r_k, gn_g, gn_b):
    bsz, seq, n_proj = proj.shape
    d = w0.shape[-1]
    lw, la, lg = w2.shape[0], a2.shape[0], g2.shape[0]
    n_tail = lw + la + lg
    assert n_proj == 3 * d + n_tail and (3 * d) % n_tail == 0
    gw = R_GROUP * R_N
    sw = R_STATE_HEADS * R_N
    width = R_GROUPS_PER_STEP * gw
    nb = d // width
    tb = min(R_CHUNK * R_CHUNKS_PER_STEP, seq)
    nc = seq // tb
    w2p = jnp.concatenate([w2, jnp.zeros_like(a2)], axis=0).astype(BF16)
    a2p = jnp.concatenate([jnp.zeros_like(w2), a2], axis=0).astype(BF16)
    tile = lambda off: pl.BlockSpec((1, tb, width), lambda b, g, c: (b, c, off * nb + g))
    tail = pl.BlockSpec((1, tb, n_tail), lambda b, g, c: (b, c, 3 * d // n_tail))
    wa = pl.BlockSpec((lw + la, width), lambda b, g, c: (0, g))
    wg = pl.BlockSpec((lg, width), lambda b, g, c: (0, g))
    par = pl.BlockSpec((1, width), lambda b, g, c: (0, g))
    row = lambda p: p.reshape(1, d)
    return pl.pallas_call(
        functools.partial(_wkv_body, n_tanh=lw),
        grid=(bsz, nb, nc),
        in_specs=[tile(0), tile(1), tile(2), tail, wa, wa, wg] + [par] * 7,
        out_specs=pl.BlockSpec((1, tb, width), lambda b, g, c: (b, c, g)),
        out_shape=jax.ShapeDtypeStruct((bsz, seq, d), BF16),
        scratch_shapes=[pltpu.VMEM((width // sw, sw, sw), F32)],
        compiler_params=_cparams(("parallel", "parallel", "arbitrary")),
        name="wkv7_chunked",
    )(proj, proj, proj, proj, w2p, a2p, g2.astype(BF16),
      row(w0), row(a0), row(k_k), row(k_a), row(r_k), row(gn_g), row(gn_b))


def _rwkv_mixer(x, w_in, mu, w0, w2, a0, a2, g2, k_k, k_a, r_k, gn_g, gn_b):
    bsz, seq, d = x.shape
    n_proj = w_in.shape[1]
    proj = _proj_shift_lerp(x.reshape(bsz * seq, d), w_in.astype(BF16), mu, seq=seq, tm=1024, col_chunk=256)
    return _wkv_core(proj.reshape(bsz, seq, n_proj), w2, a2, g2, w0, a0, k_k, k_a, r_k, gn_g, gn_b)


def kernel(x, mlstm_w_in, mlstm_b_i, mlstm_b_f, mlstm_conv_w, mlstm_conv_b, mlstm_norm_g, mlstm_w_out, rwkv_w_in, rwkv_mu, rwkv_w0, rwkv_w2, rwkv_a0, rwkv_a2, rwkv_g2, rwkv_k_k, rwkv_k_a, rwkv_r_k, rwkv_gn_g, rwkv_gn_b, rwkv_w_out, ln_mix_g, ln_mix_b, mlp_w1, mlp_w2, ln_ffn_g, ln_ffn_b):
    bsz, seq, d = x.shape
    m = bsz * seq
    xf = x.reshape(m, d)
    w1_all, w2_all = mlp_w1.astype(BF16), mlp_w2.astype(BF16)
    for layer in range(DEPTH):
        j = layer // 2
        if layer % 2 == 0:
            mix = _mlstm_mixer(xf.reshape(bsz, seq, d), mlstm_w_in[j], mlstm_b_i[j], mlstm_b_f[j],
                               mlstm_conv_w[j], mlstm_conv_b[j], mlstm_norm_g[j])
            w_out = mlstm_w_out[j]
        else:
            mix = _rwkv_mixer(xf.reshape(bsz, seq, d), rwkv_w_in[j], rwkv_mu[j], rwkv_w0[j], rwkv_w2[j],
                              rwkv_a0[j], rwkv_a2[j], rwkv_g2[j], rwkv_k_k[j], rwkv_k_a[j],
                              rwkv_r_k[j].reshape(-1), rwkv_gn_g[j], rwkv_gn_b[j])
            w_out = rwkv_w_out[j]
        xf = _mix_ffn(mix.reshape(m, -1), w_out.astype(BF16), xf, ln_mix_g[layer], ln_mix_b[layer],
                      w1_all, w2_all, layer, ln_ffn_g[layer], ln_ffn_b[layer], tm=512, ff_chunk=1024, row_split=2)
    return xf.reshape(bsz, seq, d)
```

```python
import functools
import math

import jax
import jax.numpy as jnp
from jax import lax
from jax.experimental import pallas as pl
from jax.experimental.pallas import tpu as pltpu

F32 = jnp.float32
BF16 = jnp.bfloat16

DEPTH = 2
DN_ALPHA = (2.0 * DEPTH) ** 0.25
LN_EPS = 1e-5

M_HEADS = 4
M_DK = 128
M_DV = 256
M_GATE_CAP = 15.0
M_CHUNK = 256

R_N = 64
R_GN_EPS = 64e-5
R_DECAY_SCALE = math.exp(-0.5)
R_CHUNK = 64
R_GROUP = 2
R_STATE_HEADS = 2
R_GROUPS_PER_STEP = 8
R_CHUNKS_PER_STEP = 4

VMEM_LIMIT = 48 * 1024 * 1024


def _cparams(sem):
    return pltpu.CompilerParams(dimension_semantics=sem, vmem_limit_bytes=VMEM_LIMIT)


def _dot(a, b, precision=None):
    return jnp.dot(a, b, preferred_element_type=F32, precision=precision)


def _dot_nt(a, b, precision=None):
    return lax.dot_general(a, b, (((1,), (1,)), ((), ())), preferred_element_type=F32, precision=precision)


def _dot_tn(a, b, precision=None):
    return lax.dot_general(a, b, (((0,), (0,)), ((), ())), preferred_element_type=F32, precision=precision)


def _softplus(z):
    return jnp.maximum(z, 0.0) + jnp.log1p(jnp.exp(-jnp.abs(z)))


def _sigmoid(z):
    return 0.5 + 0.5 * jnp.tanh(0.5 * z)


def _split_bf16(x):
    hi = x.astype(BF16)
    return hi, (x - hi.astype(F32)).astype(BF16)


def _split3_bf16(x):
    hi, rest = x.astype(BF16), x
    rest = rest - hi.astype(F32)
    mid = rest.astype(BF16)
    return hi, mid, (rest - mid.astype(F32)).astype(BF16)


def _proj_lerp_body(x_ref, w_ref, mu_ref, o_ref, carry_ref, *, tiles_per_seq, col_chunk):
    @pl.when(pl.program_id(0) % tiles_per_seq == 0)
    def _():
        carry_ref[...] = jnp.zeros_like(carry_ref)

    xb = x_ref[...].astype(BF16)
    tm = xb.shape[0]
    first = lax.broadcasted_iota(jnp.int32, (8, col_chunk), 0) == 0

    def epilogue(acc, cols):
        rolled = pltpu.roll(acc, 1, 0)
        top = jnp.where(first, carry_ref[0:1, cols], rolled[0:8])
        shifted = jnp.concatenate([top, rolled[8:]], axis=0)
        o_ref[:, cols] = (acc + mu_ref[:, cols] * (shifted - acc)).astype(o_ref.dtype)
        carry_ref[:, cols] = jnp.broadcast_to(acc[tm - 1:tm, :], (8, col_chunk))

    chunks = [slice(c0, c0 + col_chunk) for c0 in range(0, w_ref.shape[1], col_chunk)]
    pending = None
    for cols in chunks:
        acc = _dot(xb, w_ref[:, cols])
        if pending is not None:
            epilogue(*pending)
        pending = (acc, cols)
    epilogue(*pending)


def _proj_shift_lerp(x, w, mu, *, seq, tm, col_chunk):
    m, k = x.shape
    n = w.shape[1]
    tm = min(tm, seq)
    assert seq % tm == 0 and n % col_chunk == 0
    return pl.pallas_call(
        functools.partial(_proj_lerp_body, tiles_per_seq=seq // tm, col_chunk=col_chunk),
        grid=(m // tm,),
        in_specs=[pl.BlockSpec((tm, k), lambda i: (i, 0)),
                  pl.BlockSpec((k, n), lambda i: (0, 0), pipeline_mode=pl.Buffered(1)),
                  pl.BlockSpec((1, n), lambda i: (0, 0), pipeline_mode=pl.Buffered(1))],
        out_specs=pl.BlockSpec((tm, n), lambda i: (i, 0)),
        out_shape=jax.ShapeDtypeStruct((m, n), F32),
        scratch_shapes=[pltpu.VMEM((8, n), F32)],
        compiler_params=_cparams(("arbitrary",)),
        name="proj_token_shift",
    )(x, w, mu.reshape(1, n))


def _proj_conv_body(x_ref, w_ref, wg_ref, gb_ref, cw_ref, cb_ref, o_ref, g_ref, carry_ref, *,
                    tiles_per_seq, col_chunk):
    @pl.when(pl.program_id(0) % tiles_per_seq == 0)
    def _():
        carry_ref[...] = jnp.zeros_like(carry_ref)

    xb = x_ref[...].astype(BF16)
    tm = xb.shape[0]
    taps, n_conv = cw_ref.shape

    def epilogue(acc, cols):
        if cols.start < n_conv:
            ext = jnp.concatenate([carry_ref[:, cols], acc], axis=0)
            conv = cw_ref[0:1, cols] * ext
            for tap in range(1, taps):
                conv = pltpu.roll(conv, 1, 0) + cw_ref[tap:tap + 1, cols] * ext
            conv = conv[8:] + cb_ref[:, cols]
            o_ref[:, cols] = (conv * _sigmoid(conv)).astype(o_ref.dtype)
            carry_ref[:, cols] = acc[tm - 8:tm]
        else:
            o_ref[:, cols] = acc.astype(o_ref.dtype)

    chunks = [slice(c0, c0 + col_chunk) for c0 in range(0, o_ref.shape[1], col_chunk)]
    heavy = [c for c in chunks if c.start < n_conv]
    light = [c for c in chunks if c.start >= n_conv]
    per_heavy = len(light) // max(len(heavy), 1)
    order = []
    for i, c in enumerate(heavy):
        order += [c] + light[i * per_heavy:(i + 1) * per_heavy]
    order += light[len(heavy) * per_heavy:]
    pending = None
    for cols in order:
        acc = _dot(xb, w_ref[:, cols])
        if pending is not None:
            epilogue(*pending)
        pending = (acc, cols)
    capped = M_GATE_CAP * jnp.tanh((_dot(xb, wg_ref[...]) + gb_ref[...]) * (1.0 / M_GATE_CAP))
    is_input_gate = lax.broadcasted_iota(jnp.int32, capped.shape, 1) < M_HEADS
    log_gate = jnp.where(is_input_gate, capped, -_softplus(-capped))
    g_ref[0] = log_gate.T[0:2 * M_HEADS, :]
    epilogue(*pending)


def _proj_conv_silu(x, w, n, w_gate, b_gate, conv_w, conv_b, *, seq, tm, col_chunk):
    m, k = x.shape
    n_gate = w_gate.shape[1]
    assert 2 * M_HEADS == 8 and n_gate % 128 == 0
    taps, n_conv = conv_w.shape
    tm = min(tm, seq)
    assert seq % tm == 0 and n % col_chunk == 0 and n_conv % col_chunk == 0 and taps <= 8
    resident = lambda shape: pl.BlockSpec(shape, lambda i: (0, 0), pipeline_mode=pl.Buffered(1))
    return pl.pallas_call(
        functools.partial(_proj_conv_body, tiles_per_seq=seq // tm, col_chunk=col_chunk),
        grid=(m // tm,),
        in_specs=[pl.BlockSpec((tm, k), lambda i: (i, 0)), resident(w.shape), resident((k, n_gate)),
                  resident((1, n_gate)),
                  resident((taps, n_conv)), resident((1, n_conv))],
        out_specs=[pl.BlockSpec((tm, n), lambda i: (i, 0)), pl.BlockSpec((1, 2 * M_HEADS, tm), lambda i: (i, 0, 0))],
        out_shape=[jax.ShapeDtypeStruct((m, n), BF16), jax.ShapeDtypeStruct((m // tm, 2 * M_HEADS, tm), F32)],
        scratch_shapes=[pltpu.VMEM((8, n_conv), F32)],
        compiler_params=_cparams(("arbitrary",)),
        name="proj_conv_silu",
    )(x, w, w_gate, b_gate.reshape(1, n_gate), conv_w, conv_b.reshape(1, n_conv))


def _layer_norm(y, g, b):
    mu = jnp.mean(y, axis=-1, keepdims=True)
    yc = y - mu
    var = jnp.mean(jnp.square(yc), axis=-1, keepdims=True)
    return yc * lax.rsqrt(var + LN_EPS) * g + b


def _mix_ffn_body(mix_ref, wo_ref, res_ref, g1_ref, b1_ref, w1_ref, w2_ref, g2_ref, b2_ref, o_ref, *,
                  ff_chunk, row_split):
    sub = mix_ref.shape[0] // row_split

    def stages(rows):
        st = {}

        def out_proj():
            st["y"] = DN_ALPHA * res_ref[rows, :] + _dot(mix_ref[rows, :], wo_ref[...])

        def norm1():
            st["x1"] = _layer_norm(st.pop("y"), g1_ref[...], b1_ref[...])
            st["x1b"] = st["x1"].astype(BF16)
            st["acc"] = DN_ALPHA * st.pop("x1")

        def up(f):
            st["hid"] = _dot(st["x1b"], w1_ref[:, f:f + ff_chunk])

        def act():
            st["hid"] = jnp.square(jnp.maximum(st["hid"], 0.0)).astype(BF16)

        def down(f):
            st["acc"] = st["acc"] + _dot(st.pop("hid"), w2_ref[f:f + ff_chunk, :])

        def norm2():
            o_ref[rows, :] = _layer_norm(st.pop("acc"), g2_ref[...], b2_ref[...])

        seq = [out_proj, norm1]
        for f in range(0, w1_ref.shape[1], ff_chunk):
            seq += [functools.partial(up, f), act, functools.partial(down, f)]
        return seq + [norm2]

    threads = [stages(slice(h * sub, (h + 1) * sub)) for h in range(row_split)]
    for t in range(len(threads[0]) + row_split - 1):
        for h, seq in enumerate(threads):
            if 0 <= t - h < len(seq):
                seq[t - h]()


def _mix_ffn(mix, w_out, res, g1, b1, w1_all, w2_all, layer, g2, b2, *, tm, ff_chunk, row_split):
    m, k = mix.shape
    d = w_out.shape[1]
    dff = w1_all.shape[2]
    tm = min(tm, m)
    assert m % tm == 0 and dff % ff_chunk == 0 and tm % (8 * row_split) == 0
    resident = lambda shape: pl.BlockSpec(shape, lambda i: (0, 0), pipeline_mode=pl.Buffered(1))
    of_layer = lambda shape: pl.BlockSpec((None,) + shape, lambda i: (layer, 0, 0), pipeline_mode=pl.Buffered(1))
    rows = lambda width: pl.BlockSpec((tm, width), lambda i: (i, 0))
    vec = lambda p: p.reshape(1, d)
    return pl.pallas_call(
        functools.partial(_mix_ffn_body, ff_chunk=ff_chunk, row_split=row_split),
        grid=(m // tm,),
        in_specs=[rows(k), resident((k, d)), rows(d), resident((1, d)), resident((1, d)),
                  of_layer((d, dff)), of_layer((dff, d)), resident((1, d)), resident((1, d))],
        out_specs=rows(d),
        out_shape=jax.ShapeDtypeStruct((m, d), F32),
        compiler_params=_cparams(("parallel",)),
        name="outproj_ln_ffn_ln",
    )(mix, w_out, res, vec(g1), vec(b1), w1_all, w2_all, vec(g2), vec(b2))


def _mlstm_body(q_ref, k_ref, v_ref, o_ref, gate_ref, ng_ref, h_ref, c_ref, n_ref, m_ref):
    lc = q_ref.shape[1]
    heads = range(M_HEADS)

    @pl.when(pl.program_id(1) == 0)
    def _():
        c_ref[...] = jnp.zeros_like(c_ref)
        n_ref[...] = jnp.zeros_like(n_ref)
        m_ref[...] = jnp.zeros_like(m_ref)

    ri = lax.broadcasted_iota(jnp.int32, (lc, lc), 0)
    ci = lax.broadcasted_iota(jnp.int32, (lc, lc), 1)
    causal = ci <= ri
    diag = ci == ri
    g_rows = gate_ref[0]
    tri_u = jnp.where(ri <= ci, 1.0, 0.0).astype(BF16)
    cum_rows = _dot(jnp.concatenate(_split3_bf16(g_rows), axis=1), jnp.concatenate([tri_u] * 3, axis=0))
    scale = M_DK ** -0.5
    q_all = q_ref[0]
    k_all = k_ref[0]
    v_all = v_ref[0]
    qb = [q_all[:, h * M_DK:(h + 1) * M_DK] for h in heads]
    kb = [k_all[:, h * M_DK:(h + 1) * M_DK] for h in heads]
    vb = [v_all[:, h * M_DV:(h + 1) * M_DV] for h in heads]
    q = [qb[h].astype(F32) for h in heads]
    k = [kb[h].astype(F32) for h in heads]
    i_row = [g_rows[h:h + 1, :] for h in heads]
    f_row = [g_rows[M_HEADS + h:M_HEADS + h + 1, :] for h in heads]
    bcum_row = [cum_rows[M_HEADS + h:M_HEADS + h + 1, :] for h in heads]
    bcum_col = [jnp.sum(jnp.where(causal, f_row[h], 0.0), axis=-1, keepdims=True) for h in heads]
    i_col = [jnp.sum(jnp.where(diag, i_row[h], 0.0), axis=-1, keepdims=True) for h in heads]
    m_prev = [m_ref[h, 0:1, 0:1] for h in heads]
    c_prev = [c_ref[h] for h in heads]
    n_prev = [n_ref[h, 0:1, :] for h in heads]

    log_d = [jnp.where(causal, bcum_col[h] - bcum_row[h] + i_row[h], -jnp.inf) for h in heads]
    log_inter = [bcum_col[h] + m_prev[h] for h in heads]
    m_row = [jnp.maximum(jnp.max(log_d[h], axis=-1, keepdims=True), log_inter[h]) for h in heads]
    inter = [jnp.exp(log_inter[h] - m_row[h]) for h in heads]
    s = [_dot_nt(qb[h], kb[h]) * (scale * jnp.exp(log_d[h] - m_row[h])) for h in heads]
    inter_s = [scale * inter[h] for h in heads]
    qc = [_dot(qb[h], c_prev[h].astype(BF16)) for h in heads]
    num = [_dot(s[h].astype(BF16), vb[h]) + inter_s[h] * qc[h] for h in heads]
    den = [jnp.sum(s[h], axis=-1, keepdims=True) + inter_s[h] * jnp.sum(q[h] * n_prev[h], axis=-1, keepdims=True)
           for h in heads]
    hc = [num[h] / jnp.maximum(jnp.abs(den[h]), jnp.exp(-m_row[h])) for h in heads]

    b_last = [bcum_col[h][lc - 1:lc, :] for h in heads]
    log_w = [b_last[h] - bcum_col[h] + i_col[h] for h in heads]
    m_new = [jnp.maximum(b_last[h] + m_prev[h], jnp.max(log_w[h], axis=0, keepdims=True)) for h in heads]
    decay = [jnp.exp(b_last[h] + m_prev[h] - m_new[h]) for h in heads]
    kw = [k[h] * jnp.exp(log_w[h] - m_new[h]) for h in heads]
    for h in heads:
        c_ref[h] = decay[h] * c_prev[h] + _dot_tn(kw[h].astype(BF16), vb[h])
        n_ref[h] = jnp.broadcast_to(decay[h] * n_prev[h] + jnp.sum(kw[h], axis=0, keepdims=True), n_ref.shape[1:])
        m_ref[h] = jnp.broadcast_to(m_new[h], m_ref.shape[1:])

    hn = jnp.concatenate([hc[h] * lax.rsqrt(jnp.mean(jnp.square(hc[h]), axis=-1, keepdims=True) + 1e-6)
                          for h in heads], axis=1)
    h_ref[0] = (hn * ng_ref[...] * _sigmoid(o_ref[0].astype(F32))).astype(h_ref.dtype)


def _mlstm_core(proj, gates, norm_g):
    bsz, seq, _ = proj.shape
    tg = gates.shape[-1]
    lc = min(M_CHUNK, seq)
    nc = seq // lc
    assert tg % lc == 0 and seq % tg == 0
    per_tile = tg // lc
    hdk = M_HEADS * M_DK
    hdv = M_HEADS * M_DV
    assert 2 * hdk == hdv
    return pl.pallas_call(
        _mlstm_body,
        grid=(bsz, nc),
        in_specs=[pl.BlockSpec((1, lc, hdk), lambda b, c: (b, c, 0)),
                  pl.BlockSpec((1, lc, hdk), lambda b, c: (b, c, 1)),
                  pl.BlockSpec((1, lc, hdv), lambda b, c: (b, c, 1)),
                  pl.BlockSpec((1, lc, hdv), lambda b, c: (b, c, 2)),
                  pl.BlockSpec((1, 2 * M_HEADS, lc), lambda b, c: (b * (seq // tg) + c // per_tile, 0, c % per_tile)),
                  pl.BlockSpec((1, hdv), lambda b, c: (0, 0))],
        out_specs=pl.BlockSpec((1, lc, hdv), lambda b, c: (b, c, 0)),
        out_shape=jax.ShapeDtypeStruct((bsz, seq, hdv), BF16),
        scratch_shapes=[pltpu.VMEM((M_HEADS, M_DK, M_DV), F32),
                        pltpu.VMEM((M_HEADS, 8, M_DK), F32),
                        pltpu.VMEM((M_HEADS, 8, 128), F32)],
        compiler_params=_cparams(("parallel", "arbitrary")),
        name="mlstm_chunkwise",
    )(proj, proj, proj, proj, gates, norm_g.reshape(1, hdv))


def _mlstm_mixer(x, w_in, b_i, b_f, conv_w, conv_b, norm_g):
    bsz, seq, d = x.shape
    m = bsz * seq
    hdv = M_HEADS * M_DV
    n_main = 2 * M_HEADS * M_DK + 2 * hdv
    xf = x.reshape(m, d)
    pad = 128 - 2 * M_HEADS
    w_gate = jnp.pad(w_in[:, n_main:], ((0, 0), (0, pad))).astype(BF16)
    b_gate = jnp.pad(jnp.concatenate([b_i, b_f]), (0, pad))
    proj, gates = _proj_conv_silu(xf, w_in.astype(BF16), n_main, w_gate, b_gate, conv_w, conv_b,
                                  seq=seq, tm=1024, col_chunk=256)
    proj = proj.reshape(bsz, seq, n_main)
    return _mlstm_core(proj, gates, norm_g)


def _wkv_body(r_ref, k_ref, v_ref, tail_ref, w2_ref, a2_ref, g2_ref, w0_ref, a0_ref, kk_ref, ka_ref, rk_ref,
              gng_ref, gnb_ref, o_ref, s_ref, *, n_tanh):
    tb = r_ref.shape[1]
    lc = min(R_CHUNK, tb)
    chunks = range(tb // lc)
    width = r_ref.shape[2]
    gw = R_GROUP * R_N
    sw = R_STATE_HEADS * R_N
    rows = R_GROUP * lc
    assert tb % lc == 0 and gw % sw == 0 and width % gw == 0

    @pl.when(pl.program_id(2) == 0)
    def _():
        s_ref[...] = jnp.zeros_like(s_ref)

    groups = range(width // gw)
    tiles = range(width // sw)
    tls = [slice(j * sw, (j + 1) * sw) for j in tiles]
    brow = lax.broadcasted_iota(jnp.int32, (sw, sw), 0) // R_N
    bcol = lax.broadcasted_iota(jnp.int32, (sw, sw), 1) // R_N
    same_head = brow == bcol
    head_ones = jnp.where(same_head, 1.0, 0.0).astype(BF16)

    def seg_sum(x, parts=1):
        pieces = (x.astype(BF16),) if parts == 1 else _split_bf16(x)
        ones = jnp.concatenate([head_ones] * len(pieces), axis=0)
        return jnp.concatenate([_dot(jnp.concatenate([piece[:, sl] for piece in pieces], axis=1), ones)
                                for sl in tls], axis=1)

    r = r_ref[0].astype(F32)
    k = k_ref[0].astype(F32)
    v = v_ref[0].astype(F32)
    n_wa = w2_ref.shape[0]
    tail = tail_ref[0].astype(F32)
    t_wa = tail[:, :n_wa]
    is_tanh = lax.broadcasted_iota(jnp.int32, (1, n_wa), 1) < n_tanh
    f_wa = jnp.where(is_tanh, jnp.tanh(t_wa), t_wa).astype(BF16)
    gate = _dot(_sigmoid(tail[:, n_wa:]).astype(BF16), g2_ref[...])
    ld = (-R_DECAY_SCALE) * _sigmoid(w0_ref[...] + _dot(f_wa, w2_ref[...]))
    a_lr = _sigmoid(a0_ref[...] + _dot(f_wa, a2_ref[...]))
    kk = k * kk_ref[...]
    kk = kk * lax.rsqrt(jnp.maximum(seg_sum(jnp.square(kk)), 1e-24))
    k = k * (a_lr * ka_ref[...] + (1.0 - ka_ref[...]))
    b_ = kk * a_lr

    ti = lax.broadcasted_iota(jnp.int32, (tb, tb), 0)
    tj = lax.broadcasted_iota(jnp.int32, (tb, tb), 1)
    in_chunk_tri = jnp.logical_and(tj <= ti, ti // lc == tj // lc)
    tri = jnp.where(in_chunk_tri, 1.0, 0.0).astype(BF16)
    cum = _dot(jnp.concatenate([tri, tri], axis=1), jnp.concatenate(_split_bf16(ld), axis=0))
    e_pos = jnp.exp(cum)
    e_neg = jnp.exp(-cum)
    w_end = [jnp.exp(cum[(c + 1) * lc - 1:(c + 1) * lc, :]) for c in chunks]
    a_til = -kk * jnp.exp(cum - ld)
    r_til = r * e_pos
    b_til = b_ * e_neg
    k_til = k * e_neg

    def stacker(n):
        lane_head = lax.broadcasted_iota(jnp.int32, (lc, n), 1) // (n // R_GROUP)
        sel = [lane_head == h for h in range(R_GROUP)]
        return lambda x: jnp.concatenate([jnp.where(s_, x, jnp.zeros_like(x)) for s_ in sel], axis=0)

    stack_c = stacker(gw)
    stack_t = stacker(rows)
    trow = lax.broadcasted_iota(jnp.int32, (lc, rows), 0)
    tcol = lax.broadcasted_iota(jnp.int32, (lc, rows), 1) % lc
    strict = tcol < trow
    causal = tcol <= trow
    eye = jnp.where(tcol == trow, 1.0, 0.0).astype(F32)
    level_masks = []
    s = 1
    while s < lc:
        lo, hi = s.bit_length() - 1, s.bit_length()
        level_masks.append(jnp.logical_and((trow >> hi) == (tcol >> hi), (trow >> lo) != (tcol >> lo)))
        s *= 2

    units = [(c, g) for c in chunks for g in groups]
    tile = lambda x, c, g: x[c * lc:(c + 1) * lc, g * gw:(g + 1) * gw]
    ar_b = {u: jnp.concatenate([tile(a_til, *u), tile(r_til, *u)], axis=0).astype(BF16) for u in units}
    bk_b = {u: (tile(b_til, *u).astype(BF16), tile(k_til, *u).astype(BF16)) for u in units}
    bk_s = {u: jnp.concatenate([stack_c(bk_b[u][0]), stack_c(bk_b[u][1])], axis=0) for u in units}
    v_b = {u: tile(v, *u).astype(BF16) for u in units}
    v_s = {u: stack_c(v_b[u]) for u in units}
    bk_cat = {u: jnp.concatenate(bk_b[u], axis=0) for u in units}

    pm = {u: _dot_nt(ar_b[u], bk_s[u]) for u in units}
    n_ab = {u: jnp.where(strict, pm[u][:lc, :rows], 0.0) for u in units}
    m_xk = {u: jnp.concatenate([jnp.where(strict, pm[u][:lc, rows:], 0.0),
                                jnp.where(causal, pm[u][lc:, rows:], 0.0)], axis=0).astype(BF16) for u in units}
    m_rb = {u: jnp.where(causal, pm[u][lc:, :rows], 0.0).astype(BF16) for u in units}

    n_b = {u: n_ab[u].astype(BF16) for u in units}
    zero_b = jnp.zeros((lc, rows), BF16)
    t_inv = {u: eye + jnp.where(level_masks[0], n_ab[u], 0.0) for u in units}
    for msk in level_masks[1:]:
        t_b = {u: t_inv[u].astype(BF16) for u in units}
        half = {u: _dot(t_b[u], stack_t(jnp.where(msk, n_b[u], zero_b))) for u in units}
        t_inv = {u: t_inv[u] + _dot(half[u].astype(BF16), stack_t(t_b[u])) for u in units}
    t_b = {u: t_inv[u].astype(BF16) for u in units}
    resid = {u: eye - t_b[u].astype(F32) + _dot(n_b[u], stack_t(t_b[u])) for u in units}
    t_fix = {u: _dot(t_b[u], stack_t(resid[u].astype(BF16))).astype(BF16) for u in units}

    state = [s_ref[j] for j in tiles]
    tiles_of = lambda g: range(g * (gw // sw), (g + 1) * (gw // sw))
    sub = lambda x, g, j: x[:, (j - g * (gw // sw)) * sw:(j - g * (gw // sw) + 1) * sw]
    y_rows = []
    for c in chunks:
        us = [(c, g) for g in groups]
        zy = [jnp.concatenate([_dot_nt(sub(ar_b[u], u[1], j), state[j].astype(BF16)) for j in tiles_of(u[1])], axis=1)
              + _dot(m_xk[u], v_s[u]) for u in us]
        ub = []
        for i, u in enumerate(us):
            z_s = stack_c(zy[i][:lc].astype(BF16))
            ub.append(_dot(jnp.concatenate([t_b[u], t_fix[u]], axis=1), jnp.concatenate([z_s, z_s], axis=0)).astype(BF16))
        y_rows.append(jnp.concatenate([zy[i][lc:] + _dot(m_rb[u], stack_c(ub[i])) for i, u in enumerate(us)], axis=1))
        for i, u in enumerate(us):
            for j in tiles_of(u[1]):
                upd = _dot_tn(jnp.concatenate([sub(ub[i], u[1], j), sub(v_b[u], u[1], j)], axis=0),
                              sub(bk_cat[u], u[1], j))
                state[j] = (state[j] + jnp.where(same_head, upd, 0.0)) * w_end[c][:, tls[j]]
    for j in tiles:
        s_ref[j] = state[j]
    y = jnp.concatenate(y_rows, axis=0)

    mu = seg_sum(y, parts=2) * (1.0 / R_N)
    yc = y - mu
    var = seg_sum(jnp.square(yc)) * (1.0 / R_N)
    yn = yc * lax.rsqrt(var + R_GN_EPS) * gng_ref[...] + gnb_ref[...]
    bonus = seg_sum(r * k * rk_ref[...]) * v
    o_ref[0] = ((yn + bonus) * gate).astype(o_ref.dtype)


def _wkv_core(proj, w2, a2, g2, w0, a0, k_k, k_a, r_k, gn_g, gn_b):
    bsz, seq, n_proj = proj.shape
    d = w0.shape[-1]
    lw, la, lg = w2.shape[0], a2.shape[0], g2.shape[0]
    n_tail = lw + la + lg
    assert n_proj == 3 * d + n_tail and (3 * d) % n_tail == 0
    gw = R_GROUP * R_N
    sw = R_STATE_HEADS * R_N
    width = R_GROUPS_PER_STEP * gw
    nb = d // width
    tb = min(R_CHUNK * R_CHUNKS_PER_STEP, seq)
    nc = seq // tb
    w2p = jnp.concatenate([w2, jnp.zeros_like(a2)], axis=0).astype(BF16)
    a2p = jnp.concatenate([jnp.zeros_like(w2), a2], axis=0).astype(BF16)
    tile = lambda off: pl.BlockSpec((1, tb, width), lambda b, g, c: (b, c, off * nb + g))
    tail = pl.BlockSpec((1, tb, n_tail), lambda b, g, c: (b, c, 3 * d // n_tail))
    wa = pl.BlockSpec((lw + la, width), lambda b, g, c: (0, g))
    wg = pl.BlockSpec((lg, width), lambda b, g, c: (0, g))
    par = pl.BlockSpec((1, width), lambda b, g, c: (0, g))
    row = lambda p: p.reshape(1, d)
    return pl.pallas_call(
        functools.partial(_wkv_body, n_tanh=lw),
        grid=(bsz, nb, nc),
        in_specs=[tile(0), tile(1), tile(2), tail, wa, wa, wg] + [par] * 7,
        out_specs=pl.BlockSpec((1, tb, width), lambda b, g, c: (b, c, g)),
        out_shape=jax.ShapeDtypeStruct((bsz, seq, d), BF16),
        scratch_shapes=[pltpu.VMEM((width // sw, sw, sw), F32)],
        compiler_params=_cparams(("parallel", "parallel", "arbitrary")),
        name="wkv7_chunked",
    )(proj, proj, proj, proj, w2p, a2p, g2.astype(BF16),
      row(w0), row(a0), row(k_k), row(k_a), row(r_k), row(gn_g), row(gn_b))


def _rwkv_mixer(x, w_in, mu, w0, w2, a0, a2, g2, k_k, k_a, r_k, gn_g, gn_b):
    bsz, seq, d = x.shape
    n_proj = w_in.shape[1]
    proj = _proj_shift_lerp(x.reshape(bsz * seq, d), w_in.astype(BF16), mu, seq=seq, tm=1024, col_chunk=256)
    return _wkv_core(proj.reshape(bsz, seq, n_proj), w2, a2, g2, w0, a0, k_k, k_a, r_k, gn_g, gn_b)


def kernel(x, mlstm_w_in, mlstm_b_i, mlstm_b_f, mlstm_conv_w, mlstm_conv_b, mlstm_norm_g, mlstm_w_out, rwkv_w_in, rwkv_mu, rwkv_w0, rwkv_w2, rwkv_a0, rwkv_a2, rwkv_g2, rwkv_k_k, rwkv_k_a, rwkv_r_k, rwkv_gn_g, rwkv_gn_b, rwkv_w_out, ln_mix_g, ln_mix_b, mlp_w1, mlp_w2, ln_ffn_g, ln_ffn_b):
    bsz, seq, d = x.shape
    m = bsz * seq
    xf = x.reshape(m, d)
    w1_all, w2_all = mlp_w1.astype(BF16), mlp_w2.astype(BF16)
    for layer in range(DEPTH):
        j = layer // 2
        if layer % 2 == 0:
            mix = _mlstm_mixer(xf.reshape(bsz, seq, d), mlstm_w_in[j], mlstm_b_i[j], mlstm_b_f[j],
                               mlstm_conv_w[j], mlstm_conv_b[j], mlstm_norm_g[j])
            w_out = mlstm_w_out[j]
        else:
            mix = _rwkv_mixer(xf.reshape(bsz, seq, d), rwkv_w_in[j], rwkv_mu[j], rwkv_w0[j], rwkv_w2[j],
                              rwkv_a0[j], rwkv_a2[j], rwkv_g2[j], rwkv_k_k[j], rwkv_k_a[j],
                              rwkv_r_k[j].reshape(-1), rwkv_gn_g[j], rwkv_gn_b[j])
            w_out = rwkv_w_out[j]
        xf = _mix_ffn(mix.reshape(m, -1), w_out.astype(BF16), xf, ln_mix_g[layer], ln_mix_b[layer],
                      w1_all, w2_all, layer, ln_ffn_g[layer], ln_ffn_b[layer], tm=512, ff_chunk=1024, row_split=2)
    return xf.reshape(bsz, seq, d)
```

```python
import functools
import math

import jax
import jax.numpy as jnp
from jax import lax
from jax.experimental import pallas as pl
from jax.experimental.pallas import tpu as pltpu

F32 = jnp.float32
BF16 = jnp.bfloat16

DEPTH = 2
DN_ALPHA = (2.0 * DEPTH) ** 0.25
LN_EPS = 1e-5

M_HEADS = 4
M_DK = 128
M_DV = 256
M_GATE_CAP = 15.0
M_CHUNK = 256

R_N = 64
R_GN_EPS = 64e-5
R_DECAY_SCALE = math.exp(-0.5)
R_CHUNK = 64
R_GROUP = 2
R_STATE_HEADS = 2
R_GROUPS_PER_STEP = 8
R_CHUNKS_PER_STEP = 4

LANES = 128
SUBLANES = 8
VMEM_LIMIT = 52 * 1024 * 1024


def _cparams(sem):
    return pltpu.CompilerParams(dimension_semantics=sem, vmem_limit_bytes=VMEM_LIMIT)


def _dot(a, b):
    return jnp.dot(a, b, preferred_element_type=F32)


def _dot_nt(a, b):
    return lax.dot_general(a, b, (((1,), (1,)), ((), ())), preferred_element_type=F32)


def _dot_tn(a, b):
    return lax.dot_general(a, b, (((0,), (0,)), ((), ())), preferred_element_type=F32)


def _softplus(z):
    return jnp.maximum(z, 0.0) + jnp.log1p(jnp.exp(-jnp.abs(z)))


def _sigmoid(z):
    return 0.5 + 0.5 * jnp.tanh(0.5 * z)


def _split_bf16(x):
    hi = x.astype(BF16)
    return hi, (x - hi.astype(F32)).astype(BF16)


def _split3_bf16(x):
    hi, rest = x.astype(BF16), x
    rest = rest - hi.astype(F32)
    mid = rest.astype(BF16)
    return hi, mid, (rest - mid.astype(F32)).astype(BF16)


def _cast_rider_specs(stacked, layer, steps):
    _, rows, cols = stacked.shape
    assert rows % steps == 0 and (rows // steps) % (2 * SUBLANES) == 0
    slab = rows // steps
    return (pl.BlockSpec((None, slab, cols), lambda i: (layer, i, 0)),
            pl.BlockSpec((slab, cols), lambda i: (i, 0)),
            jax.ShapeDtypeStruct((rows, cols), BF16))


def _run_cast_riders(in_refs, out_refs):
    for src_ref, dst_ref in zip(in_refs, out_refs):
        dst_ref[...] = src_ref[...].astype(BF16)


def _proj_lerp_body(x_ref, w_ref, mu_ref, *refs, tiles_per_seq, col_chunk, n_riders):
    o_ref, carry_ref = refs[n_riders], refs[-1]
    _run_cast_riders(refs[:n_riders], refs[n_riders + 1:-1])

    @pl.when(pl.program_id(0) % tiles_per_seq == 0)
    def _():
        carry_ref[...] = jnp.zeros_like(carry_ref)

    xb = x_ref[...].astype(BF16)
    tm = xb.shape[0]
    first = lax.broadcasted_iota(jnp.int32, (SUBLANES, col_chunk), 0) == 0

    def epilogue(acc, cols):
        rolled = pltpu.roll(acc, 1, 0)
        top = jnp.where(first, carry_ref[0:1, cols], rolled[0:SUBLANES])
        shifted = jnp.concatenate([top, rolled[SUBLANES:]], axis=0)
        o_ref[:, cols] = (acc + mu_ref[:, cols] * (shifted - acc)).astype(o_ref.dtype)
        carry_ref[:, cols] = jnp.broadcast_to(acc[tm - 1:tm, :], (SUBLANES, col_chunk))

    chunks = [slice(c0, c0 + col_chunk) for c0 in range(0, w_ref.shape[1], col_chunk)]
    pending = None
    for cols in chunks:
        acc = _dot(xb, w_ref[:, cols])
        if pending is not None:
            epilogue(*pending)
        pending = (acc, cols)
    epilogue(*pending)


def _proj_shift_lerp(x, w, mu, riders, *, seq, tm, col_chunk):
    m, k = x.shape
    n = w.shape[1]
    tm = min(tm, seq)
    assert seq % tm == 0 and n % col_chunk == 0
    rider_specs = [_cast_rider_specs(a, layer, m // tm) for a, layer in riders]
    out = pl.pallas_call(
        functools.partial(_proj_lerp_body, tiles_per_seq=seq // tm, col_chunk=col_chunk, n_riders=len(riders)),
        grid=(m // tm,),
        in_specs=[pl.BlockSpec((tm, k), lambda i: (i, 0)),
                  pl.BlockSpec((k, n), lambda i: (0, 0), pipeline_mode=pl.Buffered(1)),
                  pl.BlockSpec((1, n), lambda i: (0, 0), pipeline_mode=pl.Buffered(1))] + [s[0] for s in rider_specs],
        out_specs=[pl.BlockSpec((tm, n), lambda i: (i, 0))] + [s[1] for s in rider_specs],
        out_shape=[jax.ShapeDtypeStruct((m, n), F32)] + [s[2] for s in rider_specs],
        scratch_shapes=[pltpu.VMEM((SUBLANES, n), F32)],
        compiler_params=_cparams(("arbitrary",)),
        name="proj_token_shift",
    )(x, w, mu.reshape(1, n), *[a for a, _ in riders])
    return out[0], out[1:]


def _proj_conv_body(x_ref, w_ref, wg_ref, gb_ref, cw_ref, cb_ref, *refs, tiles_per_seq, col_chunk, n_riders):
    o_ref, g_ref, carry_ref = refs[n_riders], refs[n_riders + 1], refs[-1]
    _run_cast_riders(refs[:n_riders], refs[n_riders + 2:-1])

    @pl.when(pl.program_id(0) % tiles_per_seq == 0)
    def _():
        carry_ref[...] = jnp.zeros_like(carry_ref)

    xb = x_ref[...].astype(BF16)
    tm = xb.shape[0]
    taps, n_conv = cw_ref.shape

    def epilogue(acc, cols):
        if cols.start < n_conv:
            ext = jnp.concatenate([carry_ref[:, cols], acc], axis=0)
            conv = cw_ref[0:1, cols] * ext
            for tap in range(1, taps):
                conv = pltpu.roll(conv, 1, 0) + cw_ref[tap:tap + 1, cols] * ext
            conv = conv[SUBLANES:] + cb_ref[:, cols]
            o_ref[:, cols] = (conv * _sigmoid(conv)).astype(o_ref.dtype)
            carry_ref[:, cols] = acc[tm - SUBLANES:tm]
        else:
            o_ref[:, cols] = acc.astype(o_ref.dtype)

    chunks = [slice(c0, c0 + col_chunk) for c0 in range(0, o_ref.shape[1], col_chunk)]
    heavy = [c for c in chunks if c.start < n_conv]
    light = [c for c in chunks if c.start >= n_conv]
    per_heavy = len(light) // max(len(heavy), 1)
    order = []
    for i, c in enumerate(heavy):
        order += [c] + light[i * per_heavy:(i + 1) * per_heavy]
    order += light[len(heavy) * per_heavy:]
    pending = None
    for cols in order:
        acc = _dot(xb, w_ref[:, cols])
        if pending is not None:
            epilogue(*pending)
        pending = (acc, cols)
    capped = M_GATE_CAP * jnp.tanh((_dot(xb, wg_ref[...]) + gb_ref[...]) * (1.0 / M_GATE_CAP))
    is_input_gate = lax.broadcasted_iota(jnp.int32, capped.shape, 1) < M_HEADS
    log_gate = jnp.where(is_input_gate, capped, -_softplus(-capped))
    g_ref[0] = log_gate.T[0:2 * M_HEADS, :]
    epilogue(*pending)


def _proj_conv_silu(x, w, n, w_gate, b_gate, conv_w, conv_b, riders, *, seq, tm, col_chunk):
    m, k = x.shape
    n_gate = w_gate.shape[1]
    assert 2 * M_HEADS == SUBLANES and n_gate % LANES == 0
    taps, n_conv = conv_w.shape
    tm = min(tm, seq)
    assert seq % tm == 0 and n % col_chunk == 0 and n_conv % col_chunk == 0 and taps <= SUBLANES
    resident = lambda shape: pl.BlockSpec(shape, lambda i: (0, 0), pipeline_mode=pl.Buffered(1))
    rider_specs = [_cast_rider_specs(a, layer, m // tm) for a, layer in riders]
    out = pl.pallas_call(
        functools.partial(_proj_conv_body, tiles_per_seq=seq // tm, col_chunk=col_chunk, n_riders=len(riders)),
        grid=(m // tm,),
        in_specs=[pl.BlockSpec((tm, k), lambda i: (i, 0)), resident(w.shape), resident((k, n_gate)),
                  resident((1, n_gate)),
                  resident((taps, n_conv)), resident((1, n_conv))] + [s[0] for s in rider_specs],
        out_specs=[pl.BlockSpec((tm, n), lambda i: (i, 0)),
                   pl.BlockSpec((1, 2 * M_HEADS, tm), lambda i: (i, 0, 0))] + [s[1] for s in rider_specs],
        out_shape=[jax.ShapeDtypeStruct((m, n), BF16),
                   jax.ShapeDtypeStruct((m // tm, 2 * M_HEADS, tm), F32)] + [s[2] for s in rider_specs],
        scratch_shapes=[pltpu.VMEM((SUBLANES, n_conv), F32)],
        compiler_params=_cparams(("arbitrary",)),
        name="proj_conv_silu",
    )(x, w, w_gate, b_gate.reshape(1, n_gate), conv_w, conv_b.reshape(1, n_conv), *[a for a, _ in riders])
    return out[0], out[1], out[2:]


def _layer_norm(y, g, b):
    mu = jnp.mean(y, axis=-1, keepdims=True)
    yc = y - mu
    var = jnp.mean(jnp.square(yc), axis=-1, keepdims=True)
    return yc * lax.rsqrt(var + LN_EPS) * g + b


def _mix_ffn_body(mix_ref, wo_ref, res_ref, g1_ref, b1_ref, w1_ref, w2_ref, g2_ref, b2_ref, o_ref, *,
                  ff_chunk, row_split):
    sub = mix_ref.shape[0] // row_split

    def stages(rows):
        st = {}

        def out_proj():
            st["y"] = DN_ALPHA * res_ref[rows, :] + _dot(mix_ref[rows, :], wo_ref[...])

        def norm1():
            st["x1"] = _layer_norm(st.pop("y"), g1_ref[...], b1_ref[...])
            st["x1b"] = st["x1"].astype(BF16)
            st["acc"] = DN_ALPHA * st.pop("x1")

        def up(f):
            st["hid"] = _dot(st["x1b"], w1_ref[:, f:f + ff_chunk])

        def act():
            st["hid"] = jnp.square(jnp.maximum(st["hid"], 0.0)).astype(BF16)

        def down(f):
            st["acc"] = st["acc"] + _dot(st.pop("hid"), w2_ref[f:f + ff_chunk, :])

        def norm2():
            o_ref[rows, :] = _layer_norm(st.pop("acc"), g2_ref[...], b2_ref[...])

        seq = [out_proj, norm1]
        for f in range(0, w1_ref.shape[1], ff_chunk):
            seq += [functools.partial(up, f), act, functools.partial(down, f)]
        return seq + [norm2]

    threads = [stages(slice(h * sub, (h + 1) * sub)) for h in range(row_split)]
    for t in range(len(threads[0]) + row_split - 1):
        for h, seq in enumerate(threads):
            if 0 <= t - h < len(seq):
                seq[t - h]()


def _mix_ffn(mix, w_out, res, g1, b1, w1, w2, g2, b2, *, tm, ff_chunk, row_split):
    m, k = mix.shape
    d = w_out.shape[1]
    dff = w1.shape[1]
    tm = min(tm, m)
    assert m % tm == 0 and dff % ff_chunk == 0 and tm % (SUBLANES * row_split) == 0
    resident = lambda shape: pl.BlockSpec(shape, lambda i: (0, 0), pipeline_mode=pl.Buffered(1))
    rows = lambda width: pl.BlockSpec((tm, width), lambda i: (i, 0))
    vec = lambda p: p.reshape(1, d)
    return pl.pallas_call(
        functools.partial(_mix_ffn_body, ff_chunk=ff_chunk, row_split=row_split),
        grid=(m // tm,),
        in_specs=[rows(k), resident((k, d)), rows(d), resident((1, d)), resident((1, d)),
                  resident((d, dff)), resident((dff, d)), resident((1, d)), resident((1, d))],
        out_specs=rows(d),
        out_shape=jax.ShapeDtypeStruct((m, d), F32),
        compiler_params=_cparams(("parallel",)),
        name="outproj_ln_ffn_ln",
    )(mix, w_out, res, vec(g1), vec(b1), w1, w2, vec(g2), vec(b2))


def _mlstm_body(q_ref, k_ref, v_ref, o_ref, gate_ref, ng_ref, h_ref, c_ref, n_ref, m_ref):
    lc = q_ref.shape[1]
    heads = range(M_HEADS)

    @pl.when(pl.program_id(1) == 0)
    def _():
        c_ref[...] = jnp.zeros_like(c_ref)
        n_ref[...] = jnp.zeros_like(n_ref)
        m_ref[...] = jnp.zeros_like(m_ref)

    ri = lax.broadcasted_iota(jnp.int32, (lc, lc), 0)
    ci = lax.broadcasted_iota(jnp.int32, (lc, lc), 1)
    causal = ci <= ri
    diag = ci == ri
    g_rows = gate_ref[0]
    tri_u = jnp.where(ri <= ci, 1.0, 0.0).astype(BF16)
    cum_rows = _dot(jnp.concatenate(_split3_bf16(g_rows), axis=1), jnp.concatenate([tri_u] * 3, axis=0))
    scale = M_DK ** -0.5
    q_all = q_ref[0]
    k_all = k_ref[0]
    v_all = v_ref[0]
    qb = [q_all[:, h * M_DK:(h + 1) * M_DK] for h in heads]
    kb = [k_all[:, h * M_DK:(h + 1) * M_DK] for h in heads]
    vb = [v_all[:, h * M_DV:(h + 1) * M_DV] for h in heads]
    q = [qb[h].astype(F32) for h in heads]
    k = [kb[h].astype(F32) for h in heads]
    i_row = [g_rows[h:h + 1, :] for h in heads]
    f_row = [g_rows[M_HEADS + h:M_HEADS + h + 1, :] for h in heads]
    bcum_row = [cum_rows[M_HEADS + h:M_HEADS + h + 1, :] for h in heads]
    bcum_col = [jnp.sum(jnp.where(causal, f_row[h], 0.0), axis=-1, keepdims=True) for h in heads]
    i_col = [jnp.sum(jnp.where(diag, i_row[h], 0.0), axis=-1, keepdims=True) for h in heads]
    m_prev = [m_ref[h, 0:1, 0:1] for h in heads]
    c_prev = [c_ref[h] for h in heads]
    n_prev = [n_ref[h, 0:1, :] for h in heads]

    log_d = [jnp.where(causal, bcum_col[h] - bcum_row[h] + i_row[h], -jnp.inf) for h in heads]
    log_inter = [bcum_col[h] + m_prev[h] for h in heads]
    m_row = [jnp.maximum(jnp.max(log_d[h], axis=-1, keepdims=True), log_inter[h]) for h in heads]
    inter = [jnp.exp(log_inter[h] - m_row[h]) for h in heads]
    s = [_dot_nt(qb[h], kb[h]) * (scale * jnp.exp(log_d[h] - m_row[h])) for h in heads]
    inter_s = [scale * inter[h] for h in heads]
    qc = [_dot(qb[h], c_prev[h].astype(BF16)) for h in heads]
    num = [_dot(s[h].astype(BF16), vb[h]) + inter_s[h] * qc[h] for h in heads]
    den = [jnp.sum(s[h], axis=-1, keepdims=True) + inter_s[h] * jnp.sum(q[h] * n_prev[h], axis=-1, keepdims=True)
           for h in heads]
    hc = [num[h] / jnp.maximum(jnp.abs(den[h]), jnp.exp(-m_row[h])) for h in heads]

    b_last = [bcum_col[h][lc - 1:lc, :] for h in heads]
    log_w = [b_last[h] - bcum_col[h] + i_col[h] for h in heads]
    m_new = [jnp.maximum(b_last[h] + m_prev[h], jnp.max(log_w[h], axis=0, keepdims=True)) for h in heads]
    decay = [jnp.exp(b_last[h] + m_prev[h] - m_new[h]) for h in heads]
    kw = [k[h] * jnp.exp(log_w[h] - m_new[h]) for h in heads]
    for h in heads:
        c_ref[h] = decay[h] * c_prev[h] + _dot_tn(kw[h].astype(BF16), vb[h])
        n_ref[h] = jnp.broadcast_to(decay[h] * n_prev[h] + jnp.sum(kw[h], axis=0, keepdims=True), n_ref.shape[1:])
        m_ref[h] = jnp.broadcast_to(m_new[h], m_ref.shape[1:])

    hn = jnp.concatenate([hc[h] * lax.rsqrt(jnp.mean(jnp.square(hc[h]), axis=-1, keepdims=True) + 1e-6)
                          for h in heads], axis=1)
    h_ref[0] = (hn * ng_ref[...] * _sigmoid(o_ref[0].astype(F32))).astype(h_ref.dtype)


def _mlstm_core(proj, gates, norm_g):
    bsz, seq, _ = proj.shape
    tg = gates.shape[-1]
    lc = min(M_CHUNK, seq)
    nc = seq // lc
    assert tg % lc == 0 and seq % tg == 0
    per_tile = tg // lc
    hdk = M_HEADS * M_DK
    hdv = M_HEADS * M_DV
    assert 2 * hdk == hdv
    return pl.pallas_call(
        _mlstm_body,
        grid=(bsz, nc),
        in_specs=[pl.BlockSpec((1, lc, hdk), lambda b, c: (b, c, 0)),
                  pl.BlockSpec((1, lc, hdk), lambda b, c: (b, c, 1)),
                  pl.BlockSpec((1, lc, hdv), lambda b, c: (b, c, 1)),
                  pl.BlockSpec((1, lc, hdv), lambda b, c: (b, c, 2)),
                  pl.BlockSpec((1, 2 * M_HEADS, lc), lambda b, c: (b * (seq // tg) + c // per_tile, 0, c % per_tile)),
                  pl.BlockSpec((1, hdv), lambda b, c: (0, 0))],
        out_specs=pl.BlockSpec((1, lc, hdv), lambda b, c: (b, c, 0)),
        out_shape=jax.ShapeDtypeStruct((bsz, seq, hdv), BF16),
        scratch_shapes=[pltpu.VMEM((M_HEADS, M_DK, M_DV), F32),
                        pltpu.VMEM((M_HEADS, SUBLANES, M_DK), F32),
                        pltpu.VMEM((M_HEADS, SUBLANES, LANES), F32)],
        compiler_params=_cparams(("parallel", "arbitrary")),
        name="mlstm_chunkwise",
    )(proj, proj, proj, proj, gates, norm_g.reshape(1, hdv))


def _mlstm_mixer(x, w_in, b_i, b_f, conv_w, conv_b, norm_g, riders):
    bsz, seq, d = x.shape
    m = bsz * seq
    hdv = M_HEADS * M_DV
    n_main = 2 * M_HEADS * M_DK + 2 * hdv
    xf = x.reshape(m, d)
    pad = LANES - 2 * M_HEADS
    w_gate = jnp.pad(w_in[:, n_main:], ((0, 0), (0, pad))).astype(BF16)
    b_gate = jnp.pad(jnp.concatenate([b_i, b_f]), (0, pad))
    proj, gates, casted = _proj_conv_silu(xf, w_in.astype(BF16), n_main, w_gate, b_gate, conv_w, conv_b, riders,
                                          seq=seq, tm=1024, col_chunk=256)
    proj = proj.reshape(bsz, seq, n_main)
    return _mlstm_core(proj, gates, norm_g), casted


def _wkv_body(r_ref, k_ref, v_ref, tail_ref, w2_ref, a2_ref, g2_ref, w0_ref, a0_ref, kk_ref, ka_ref, rk_ref,
              gng_ref, gnb_ref, o_ref, s_ref, *, n_tanh):
    tb = r_ref.shape[1]
    lc = min(R_CHUNK, tb)
    chunks = range(tb // lc)
    width = r_ref.shape[2]
    gw = R_GROUP * R_N
    sw = R_STATE_HEADS * R_N
    rows = R_GROUP * lc
    assert tb % lc == 0 and gw % sw == 0 and width % gw == 0

    @pl.when(pl.program_id(2) == 0)
    def _():
        s_ref[...] = jnp.zeros_like(s_ref)

    groups = range(width // gw)
    tiles = range(width // sw)
    tls = [slice(j * sw, (j + 1) * sw) for j in tiles]
    brow = lax.broadcasted_iota(jnp.int32, (sw, sw), 0) // R_N
    bcol = lax.broadcasted_iota(jnp.int32, (sw, sw), 1) // R_N
    same_head = brow == bcol
    head_ones = jnp.where(same_head, 1.0, 0.0).astype(BF16)

    def seg_sum(x, parts=1):
        pieces = (x.astype(BF16),) if parts == 1 else _split_bf16(x)
        ones = jnp.concatenate([head_ones] * len(pieces), axis=0)
        return jnp.concatenate([_dot(jnp.concatenate([piece[:, sl] for piece in pieces], axis=1), ones)
                                for sl in tls], axis=1)

    r = r_ref[0].astype(F32)
    k = k_ref[0].astype(F32)
    v = v_ref[0].astype(F32)
    n_wa = w2_ref.shape[0]
    tail = tail_ref[0].astype(F32)
    t_wa = tail[:, :n_wa]
    is_tanh = lax.broadcasted_iota(jnp.int32, (1, n_wa), 1) < n_tanh
    f_wa = jnp.where(is_tanh, jnp.tanh(t_wa), t_wa).astype(BF16)
    gate = _dot(_sigmoid(tail[:, n_wa:]).astype(BF16), g2_ref[...])
    ld = (-R_DECAY_SCALE) * _sigmoid(w0_ref[...] + _dot(f_wa, w2_ref[...]))
    a_lr = _sigmoid(a0_ref[...] + _dot(f_wa, a2_ref[...]))
    kk = k * kk_ref[...]
    kk = kk * lax.rsqrt(jnp.maximum(seg_sum(jnp.square(kk)), 1e-24))
    k = k * (a_lr * ka_ref[...] + (1.0 - ka_ref[...]))
    b_ = kk * a_lr

    ti = lax.broadcasted_iota(jnp.int32, (tb, tb), 0)
    tj = lax.broadcasted_iota(jnp.int32, (tb, tb), 1)
    in_chunk_tri = jnp.logical_and(tj <= ti, ti // lc == tj // lc)
    tri = jnp.where(in_chunk_tri, 1.0, 0.0).astype(BF16)
    cum = _dot(jnp.concatenate([tri, tri], axis=1), jnp.concatenate(_split_bf16(ld), axis=0))
    e_pos = jnp.exp(cum)
    e_neg = jnp.exp(-cum)
    w_end = [jnp.exp(cum[(c + 1) * lc - 1:(c + 1) * lc, :]) for c in chunks]
    a_til = -kk * jnp.exp(cum - ld)
    r_til = r * e_pos
    b_til = b_ * e_neg
    k_til = k * e_neg

    def stacker(n):
        lane_head = lax.broadcasted_iota(jnp.int32, (lc, n), 1) // (n // R_GROUP)
        sel = [lane_head == h for h in range(R_GROUP)]
        return lambda x: jnp.concatenate([jnp.where(s_, x, jnp.zeros_like(x)) for s_ in sel], axis=0)

    stack_c = stacker(gw)
    stack_t = stacker(rows)
    trow = lax.broadcasted_iota(jnp.int32, (lc, rows), 0)
    tcol = lax.broadcasted_iota(jnp.int32, (lc, rows), 1) % lc
    strict = tcol < trow
    causal = tcol <= trow
    eye = jnp.where(tcol == trow, 1.0, 0.0).astype(F32)
    level_masks = []
    s = 1
    while s < lc:
        lo, hi = s.bit_length() - 1, s.bit_length()
        level_masks.append(jnp.logical_and((trow >> hi) == (tcol >> hi), (trow >> lo) != (tcol >> lo)))
        s *= 2

    units = [(c, g) for c in chunks for g in groups]
    tile = lambda x, c, g: x[c * lc:(c + 1) * lc, g * gw:(g + 1) * gw]
    ar_b = {u: jnp.concatenate([tile(a_til, *u), tile(r_til, *u)], axis=0).astype(BF16) for u in units}
    bk_b = {u: (tile(b_til, *u).astype(BF16), tile(k_til, *u).astype(BF16)) for u in units}
    bk_s = {u: jnp.concatenate([stack_c(bk_b[u][0]), stack_c(bk_b[u][1])], axis=0) for u in units}
    v_b = {u: tile(v, *u).astype(BF16) for u in units}
    v_s = {u: stack_c(v_b[u]) for u in units}
    bk_cat = {u: jnp.concatenate(bk_b[u], axis=0) for u in units}

    pm = {u: _dot_nt(ar_b[u], bk_s[u]) for u in units}
    n_ab = {u: jnp.where(strict, pm[u][:lc, :rows], 0.0) for u in units}
    m_xk = {u: jnp.concatenate([jnp.where(strict, pm[u][:lc, rows:], 0.0),
                                jnp.where(causal, pm[u][lc:, rows:], 0.0)], axis=0).astype(BF16) for u in units}
    m_rb = {u: jnp.where(causal, pm[u][lc:, :rows], 0.0).astype(BF16) for u in units}

    n_b = {u: n_ab[u].astype(BF16) for u in units}
    zero_b = jnp.zeros((lc, rows), BF16)
    t_inv = {u: eye + jnp.where(level_masks[0], n_ab[u], 0.0) for u in units}
    for msk in level_masks[1:]:
        t_b = {u: t_inv[u].astype(BF16) for u in units}
        half = {u: _dot(t_b[u], stack_t(jnp.where(msk, n_b[u], zero_b))) for u in units}
        t_inv = {u: t_inv[u] + _dot(half[u].astype(BF16), stack_t(t_b[u])) for u in units}
    t_b = {u: t_inv[u].astype(BF16) for u in units}
    resid = {u: eye - t_b[u].astype(F32) + _dot(n_b[u], stack_t(t_b[u])) for u in units}
    t_fix = {u: _dot(t_b[u], stack_t(resid[u].astype(BF16))).astype(BF16) for u in units}

    state = [s_ref[j] for j in tiles]
    tiles_of = lambda g: range(g * (gw // sw), (g + 1) * (gw // sw))
    sub = lambda x, g, j: x[:, (j - g * (gw // sw)) * sw:(j - g * (gw // sw) + 1) * sw]
    y_rows = []
    for c in chunks:
        us = [(c, g) for g in groups]
        zy = [jnp.concatenate([_dot_nt(sub(ar_b[u], u[1], j), state[j].astype(BF16)) for j in tiles_of(u[1])], axis=1)
              + _dot(m_xk[u], v_s[u]) for u in us]
        ub = []
        for i, u in enumerate(us):
            z_s = stack_c(zy[i][:lc].astype(BF16))
            ub.append(_dot(jnp.concatenate([t_b[u], t_fix[u]], axis=1), jnp.concatenate([z_s, z_s], axis=0)).astype(BF16))
        y_rows.append(jnp.concatenate([zy[i][lc:] + _dot(m_rb[u], stack_c(ub[i])) for i, u in enumerate(us)], axis=1))
        for i, u in enumerate(us):
            for j in tiles_of(u[1]):
                upd = _dot_tn(jnp.concatenate([sub(ub[i], u[1], j), sub(v_b[u], u[1], j)], axis=0),
                              sub(bk_cat[u], u[1], j))
                state[j] = (state[j] + jnp.where(same_head, upd, 0.0)) * w_end[c][:, tls[j]]
    for j in tiles:
        s_ref[j] = state[j]
    y = jnp.concatenate(y_rows, axis=0)

    mu = seg_sum(y, parts=2) * (1.0 / R_N)
    yc = y - mu
    var = seg_sum(jnp.square(yc)) * (1.0 / R_N)
    yn = yc * lax.rsqrt(var + R_GN_EPS) * gng_ref[...] + gnb_ref[...]
    bonus = seg_sum(r * k * rk_ref[...]) * v
    o_ref[0] = ((yn + bonus) * gate).astype(o_ref.dtype)


def _wkv_core(proj, w2, a2, g2, w0, a0, k_k, k_a, r_k, gn_g, gn_b):
    bsz, seq, n_proj = proj.shape
    d = w0.shape[-1]
    lw, la, lg = w2.shape[0], a2.shape[0], g2.shape[0]
    n_tail = lw + la + lg
    assert n_proj == 3 * d + n_tail and (3 * d) % n_tail == 0
    gw = R_GROUP * R_N
    sw = R_STATE_HEADS * R_N
    width = R_GROUPS_PER_STEP * gw
    nb = d // width
    tb = min(R_CHUNK * R_CHUNKS_PER_STEP, seq)
    nc = seq // tb
    w2p = jnp.concatenate([w2, jnp.zeros_like(a2)], axis=0).astype(BF16)
    a2p = jnp.concatenate([jnp.zeros_like(w2), a2], axis=0).astype(BF16)
    tile = lambda off: pl.BlockSpec((1, tb, width), lambda b, g, c: (b, c, off * nb + g))
    tail = pl.BlockSpec((1, tb, n_tail), lambda b, g, c: (b, c, 3 * d // n_tail))
    wa = pl.BlockSpec((lw + la, width), lambda b, g, c: (0, g))
    wg = pl.BlockSpec((lg, width), lambda b, g, c: (0, g))
    par = pl.BlockSpec((1, width), lambda b, g, c: (0, g))
    row = lambda p: p.reshape(1, d)
    return pl.pallas_call(
        functools.partial(_wkv_body, n_tanh=lw),
        grid=(bsz, nb, nc),
        in_specs=[tile(0), tile(1), tile(2), tail, wa, wa, wg] + [par] * 7,
        out_specs=pl.BlockSpec((1, tb, width), lambda b, g, c: (b, c, g)),
        out_shape=jax.ShapeDtypeStruct((bsz, seq, d), BF16),
        scratch_shapes=[pltpu.VMEM((width // sw, sw, sw), F32)],
        compiler_params=_cparams(("parallel", "parallel", "arbitrary")),
        name="wkv7_chunked",
    )(proj, proj, proj, proj, w2p, a2p, g2.astype(BF16),
      row(w0), row(a0), row(k_k), row(k_a), row(r_k), row(gn_g), row(gn_b))


def _rwkv_mixer(x, w_in, mu, w0, w2, a0, a2, g2, k_k, k_a, r_k, gn_g, gn_b, riders):
    bsz, seq, d = x.shape
    n_proj = w_in.shape[1]
    proj, casted = _proj_shift_lerp(x.reshape(bsz * seq, d), w_in.astype(BF16), mu, riders,
                                    seq=seq, tm=1024, col_chunk=256)
    return _wkv_core(proj.reshape(bsz, seq, n_proj), w2, a2, g2, w0, a0, k_k, k_a, r_k, gn_g, gn_b), casted


def kernel(x, mlstm_w_in, mlstm_b_i, mlstm_b_f, mlstm_conv_w, mlstm_conv_b, mlstm_norm_g, mlstm_w_out, rwkv_w_in, rwkv_mu, rwkv_w0, rwkv_w2, rwkv_a0, rwkv_a2, rwkv_g2, rwkv_k_k, rwkv_k_a, rwkv_r_k, rwkv_gn_g, rwkv_gn_b, rwkv_w_out, ln_mix_g, ln_mix_b, mlp_w1, mlp_w2, ln_ffn_g, ln_ffn_b):
    bsz, seq, d = x.shape
    m = bsz * seq
    xf = x.reshape(m, d)
    for layer in range(DEPTH):
        j = layer // 2
        riders = [(mlp_w1, layer), (mlp_w2, layer)]
        if layer % 2 == 0:
            mix, (w1, w2) = _mlstm_mixer(xf.reshape(bsz, seq, d), mlstm_w_in[j], mlstm_b_i[j], mlstm_b_f[j],
                                         mlstm_conv_w[j], mlstm_conv_b[j], mlstm_norm_g[j], riders)
            w_out = mlstm_w_out[j]
        else:
            mix, (w1, w2) = _rwkv_mixer(xf.reshape(bsz, seq, d), rwkv_w_in[j], rwkv_mu[j], rwkv_w0[j], rwkv_w2[j],
                                        rwkv_a0[j], rwkv_a2[j], rwkv_g2[j], rwkv_k_k[j], rwkv_k_a[j],
                                        rwkv_r_k[j].reshape(-1), rwkv_gn_g[j], rwkv_gn_b[j], riders)
            w_out = rwkv_w_out[j]
        xf = _mix_ffn(mix.reshape(m, -1), w_out.astype(BF16), xf, ln_mix_g[layer], ln_mix_b[layer],
                      w1, w2, ln_ffn_g[layer], ln_ffn_b[layer], tm=512, ff_chunk=1024, row_split=2)
    return xf.reshape(bsz, seq, d)
```

```python
import functools
import math

import jax
import jax.numpy as jnp
from jax import lax
from jax.experimental import pallas as pl
from jax.experimental.pallas import tpu as pltpu

F32 = jnp.float32
BF16 = jnp.bfloat16

DEPTH = 2
DN_ALPHA = (2.0 * DEPTH) ** 0.25
LN_EPS = 1e-5

M_HEADS = 4
M_DK = 128
M_DV = 256
M_GATE_CAP = 15.0
M_CHUNK = 256

R_N = 64
R_GN_EPS = 64e-5
R_DECAY_SCALE = math.exp(-0.5)
R_CHUNK = 64
R_GROUP = 2
R_STATE_HEADS = 2
R_GROUPS_PER_STEP = 8
R_CHUNKS_PER_STEP = 4

LANES = 128
SUBLANES = 8
VMEM_LIMIT = 52 * 1024 * 1024


def _cparams(sem):
    return pltpu.CompilerParams(dimension_semantics=sem, vmem_limit_bytes=VMEM_LIMIT)


def _dot(a, b):
    return jnp.dot(a, b, preferred_element_type=F32)


def _dot_nt(a, b):
    return lax.dot_general(a, b, (((1,), (1,)), ((), ())), preferred_element_type=F32)


def _dot_tn(a, b):
    return lax.dot_general(a, b, (((0,), (0,)), ((), ())), preferred_element_type=F32)


def _softplus(z):
    return jnp.maximum(z, 0.0) + jnp.log1p(jnp.exp(-jnp.abs(z)))


def _sigmoid(z):
    return 0.5 + 0.5 * jnp.tanh(0.5 * z)


def _split_bf16(x):
    hi = x.astype(BF16)
    return hi, (x - hi.astype(F32)).astype(BF16)


def _split3_bf16(x):
    hi, rest = x.astype(BF16), x
    rest = rest - hi.astype(F32)
    mid = rest.astype(BF16)
    return hi, mid, (rest - mid.astype(F32)).astype(BF16)


def _cast_rider_specs(stacked, layer, steps):
    _, rows, cols = stacked.shape
    assert rows % steps == 0 and (rows // steps) % (2 * SUBLANES) == 0
    slab = rows // steps
    return (pl.BlockSpec((None, slab, cols), lambda i: (layer, i, 0)),
            pl.BlockSpec((slab, cols), lambda i: (i, 0)),
            jax.ShapeDtypeStruct((rows, cols), BF16))


def _run_cast_riders(in_refs, out_refs):
    for src_ref, dst_ref in zip(in_refs, out_refs):
        dst_ref[...] = src_ref[...].astype(BF16)


def _proj_lerp_body(x_ref, w_ref, mu_ref, *refs, tiles_per_seq, col_chunk, n_riders):
    o_ref, carry_ref = refs[n_riders], refs[-1]
    _run_cast_riders(refs[:n_riders], refs[n_riders + 1:-1])

    @pl.when(pl.program_id(0) % tiles_per_seq == 0)
    def _():
        carry_ref[...] = jnp.zeros_like(carry_ref)

    xb = x_ref[...].astype(BF16)
    tm = xb.shape[0]
    first = lax.broadcasted_iota(jnp.int32, (SUBLANES, col_chunk), 0) == 0

    def epilogue(acc, cols):
        rolled = pltpu.roll(acc, 1, 0)
        top = jnp.where(first, carry_ref[0:1, cols], rolled[0:SUBLANES])
        shifted = jnp.concatenate([top, rolled[SUBLANES:]], axis=0)
        o_ref[:, cols] = (acc + mu_ref[:, cols] * (shifted - acc)).astype(o_ref.dtype)
        carry_ref[:, cols] = jnp.broadcast_to(acc[tm - 1:tm, :], (SUBLANES, col_chunk))

    chunks = [slice(c0, c0 + col_chunk) for c0 in range(0, w_ref.shape[1], col_chunk)]
    pending = None
    for cols in chunks:
        acc = _dot(xb, w_ref[:, cols])
        if pending is not None:
            epilogue(*pending)
        pending = (acc, cols)
    epilogue(*pending)


def _proj_shift_lerp(x, w, mu, riders, *, seq, tm, col_chunk):
    m, k = x.shape
    n = w.shape[1]
    tm = min(tm, seq)
    assert seq % tm == 0 and n % col_chunk == 0
    rider_specs = [_cast_rider_specs(a, layer, m // tm) for a, layer in riders]
    out = pl.pallas_call(
        functools.partial(_proj_lerp_body, tiles_per_seq=seq // tm, col_chunk=col_chunk, n_riders=len(riders)),
        grid=(m // tm,),
        in_specs=[pl.BlockSpec((tm, k), lambda i: (i, 0)),
                  pl.BlockSpec((k, n), lambda i: (0, 0), pipeline_mode=pl.Buffered(1)),
                  pl.BlockSpec((1, n), lambda i: (0, 0), pipeline_mode=pl.Buffered(1))] + [s[0] for s in rider_specs],
        out_specs=[pl.BlockSpec((tm, n), lambda i: (i, 0))] + [s[1] for s in rider_specs],
        out_shape=[jax.ShapeDtypeStruct((m, n), F32)] + [s[2] for s in rider_specs],
        scratch_shapes=[pltpu.VMEM((SUBLANES, n), F32)],
        compiler_params=_cparams(("arbitrary",)),
        name="proj_token_shift",
    )(x, w, mu.reshape(1, n), *[a for a, _ in riders])
    return out[0], out[1:]


def _proj_conv_body(x_ref, w_ref, wg_ref, gb_ref, cw_ref, cb_ref, *refs, tiles_per_seq, col_chunk, n_riders):
    o_ref, g_ref, carry_ref = refs[n_riders], refs[n_riders + 1], refs[-1]
    _run_cast_riders(refs[:n_riders], refs[n_riders + 2:-1])

    @pl.when(pl.program_id(0) % tiles_per_seq == 0)
    def _():
        carry_ref[...] = jnp.zeros_like(carry_ref)

    xb = x_ref[...].astype(BF16)
    tm = xb.shape[0]
    taps, n_conv = cw_ref.shape

    def epilogue(acc, cols):
        if cols.start < n_conv:
            ext = jnp.concatenate([carry_ref[:, cols], acc], axis=0)
            conv = cw_ref[0:1, cols] * ext
            for tap in range(1, taps):
                conv = pltpu.roll(conv, 1, 0) + cw_ref[tap:tap + 1, cols] * ext
            conv = conv[SUBLANES:] + cb_ref[:, cols]
            o_ref[:, cols] = (conv * _sigmoid(conv)).astype(o_ref.dtype)
            carry_ref[:, cols] = acc[tm - SUBLANES:tm]
        else:
            o_ref[:, cols] = acc.astype(o_ref.dtype)

    chunks = [slice(c0, c0 + col_chunk) for c0 in range(0, o_ref.shape[1], col_chunk)]
    heavy = [c for c in chunks if c.start < n_conv]
    light = [c for c in chunks if c.start >= n_conv]
    per_heavy = len(light) // max(len(heavy), 1)
    order = []
    for i, c in enumerate(heavy):
        order += [c] + light[i * per_heavy:(i + 1) * per_heavy]
    order += light[len(heavy) * per_heavy:]
    pending = None
    for cols in order:
        acc = _dot(xb, w_ref[:, cols])
        if pending is not None:
            epilogue(*pending)
        pending = (acc, cols)
    capped = M_GATE_CAP * jnp.tanh((_dot(xb, wg_ref[...]) + gb_ref[...]) * (1.0 / M_GATE_CAP))
    is_input_gate = lax.broadcasted_iota(jnp.int32, capped.shape, 1) < M_HEADS
    log_gate = jnp.where(is_input_gate, capped, -_softplus(-capped))
    g_ref[0] = log_gate.T[0:2 * M_HEADS, :]
    epilogue(*pending)


def _proj_conv_silu(x, w, n, w_gate, b_gate, conv_w, conv_b, riders, *, seq, tm, col_chunk):
    m, k = x.shape
    n_gate = w_gate.shape[1]
    assert 2 * M_HEADS == SUBLANES and n_gate % LANES == 0
    taps, n_conv = conv_w.shape
    tm = min(tm, seq)
    assert seq % tm == 0 and n % col_chunk == 0 and n_conv % col_chunk == 0 and taps <= SUBLANES
    resident = lambda shape: pl.BlockSpec(shape, lambda i: (0, 0), pipeline_mode=pl.Buffered(1))
    rider_specs = [_cast_rider_specs(a, layer, m // tm) for a, layer in riders]
    out = pl.pallas_call(
        functools.partial(_proj_conv_body, tiles_per_seq=seq // tm, col_chunk=col_chunk, n_riders=len(riders)),
        grid=(m // tm,),
        in_specs=[pl.BlockSpec((tm, k), lambda i: (i, 0)), resident(w.shape), resident((k, n_gate)),
                  resident((1, n_gate)),
                  resident((taps, n_conv)), resident((1, n_conv))] + [s[0] for s in rider_specs],
        out_specs=[pl.BlockSpec((tm, n), lambda i: (i, 0)),
                   pl.BlockSpec((1, 2 * M_HEADS, tm), lambda i: (i, 0, 0))] + [s[1] for s in rider_specs],
        out_shape=[jax.ShapeDtypeStruct((m, n), BF16),
                   jax.ShapeDtypeStruct((m // tm, 2 * M_HEADS, tm), F32)] + [s[2] for s in rider_specs],
        scratch_shapes=[pltpu.VMEM((SUBLANES, n_conv), F32)],
        compiler_params=_cparams(("arbitrary",)),
        name="proj_conv_silu",
    )(x, w, w_gate, b_gate.reshape(1, n_gate), conv_w, conv_b.reshape(1, n_conv), *[a for a, _ in riders])
    return out[0], out[1], out[2:]


def _layer_norm(y, g, b):
    mu = jnp.mean(y, axis=-1, keepdims=True)
    yc = y - mu
    var = jnp.mean(jnp.square(yc), axis=-1, keepdims=True)
    return yc * lax.rsqrt(var + LN_EPS) * g + b


def _mix_ffn_body(mix_ref, wo_ref, res_ref, g1_ref, b1_ref, w1_ref, w2_ref, g2_ref, b2_ref, o_ref, *,
                  ff_chunk, row_split):
    sub = mix_ref.shape[0] // row_split

    def stages(rows):
        st = {}

        def out_proj():
            st["y"] = DN_ALPHA * res_ref[rows, :] + _dot(mix_ref[rows, :], wo_ref[...])

        def norm1():
            st["x1"] = _layer_norm(st.pop("y"), g1_ref[...], b1_ref[...])
            st["x1b"] = st["x1"].astype(BF16)
            st["acc"] = DN_ALPHA * st.pop("x1")

        def up(f):
            st["hid"] = _dot(st["x1b"], w1_ref[:, f:f + ff_chunk])

        def act():
            st["hid"] = jnp.square(jnp.maximum(st["hid"], 0.0)).astype(BF16)

        def down(f):
            st["acc"] = st["acc"] + _dot(st.pop("hid"), w2_ref[f:f + ff_chunk, :])

        def norm2():
            o_ref[rows, :] = _layer_norm(st.pop("acc"), g2_ref[...], b2_ref[...])

        seq = [out_proj, norm1]
        for f in range(0, w1_ref.shape[1], ff_chunk):
            seq += [functools.partial(up, f), act, functools.partial(down, f)]
        return seq + [norm2]

    threads = [stages(slice(h * sub, (h + 1) * sub)) for h in range(row_split)]
    for t in range(len(threads[0]) + row_split - 1):
        for h, seq in enumerate(threads):
            if 0 <= t - h < len(seq):
                seq[t - h]()


def _mix_ffn(mix, w_out, res, g1, b1, w1, w2, g2, b2, *, tm, ff_chunk, row_split):
    m, k = mix.shape
    d = w_out.shape[1]
    dff = w1.shape[1]
    tm = min(tm, m)
    assert m % tm == 0 and dff % ff_chunk == 0 and tm % (SUBLANES * row_split) == 0
    resident = lambda shape: pl.BlockSpec(shape, lambda i: (0, 0), pipeline_mode=pl.Buffered(1))
    rows = lambda width: pl.BlockSpec((tm, width), lambda i: (i, 0))
    vec = lambda p: p.reshape(1, d)
    return pl.pallas_call(
        functools.partial(_mix_ffn_body, ff_chunk=ff_chunk, row_split=row_split),
        grid=(m // tm,),
        in_specs=[rows(k), resident((k, d)), rows(d), resident((1, d)), resident((1, d)),
                  resident((d, dff)), resident((dff, d)), resident((1, d)), resident((1, d))],
        out_specs=rows(d),
        out_shape=jax.ShapeDtypeStruct((m, d), F32),
        compiler_params=_cparams(("parallel",)),
        name="outproj_ln_ffn_ln",
    )(mix, w_out, res, vec(g1), vec(b1), w1, w2, vec(g2), vec(b2))


def _mlstm_body(q_ref, k_ref, v_ref, o_ref, gate_ref, ng_ref, h_ref, c_ref, n_ref, m_ref):
    lc = q_ref.shape[1]
    heads = range(M_HEADS)

    @pl.when(pl.program_id(1) == 0)
    def _():
        c_ref[...] = jnp.zeros_like(c_ref)
        n_ref[...] = jnp.zeros_like(n_ref)
        m_ref[...] = jnp.zeros_like(m_ref)

    ri = lax.broadcasted_iota(jnp.int32, (lc, lc), 0)
    ci = lax.broadcasted_iota(jnp.int32, (lc, lc), 1)
    causal = ci <= ri
    diag = ci == ri
    g_rows = gate_ref[0]
    tri_u = jnp.where(ri <= ci, 1.0, 0.0).astype(BF16)
    cum_rows = _dot(jnp.concatenate(_split3_bf16(g_rows), axis=1), jnp.concatenate([tri_u] * 3, axis=0))
    scale = M_DK ** -0.5
    q_all = q_ref[0]
    k_all = k_ref[0]
    v_all = v_ref[0]
    qb = [q_all[:, h * M_DK:(h + 1) * M_DK] for h in heads]
    kb = [k_all[:, h * M_DK:(h + 1) * M_DK] for h in heads]
    vb = [v_all[:, h * M_DV:(h + 1) * M_DV] for h in heads]
    q = [qb[h].astype(F32) for h in heads]
    k = [kb[h].astype(F32) for h in heads]
    i_row = [g_rows[h:h + 1, :] for h in heads]
    f_row = [g_rows[M_HEADS + h:M_HEADS + h + 1, :] for h in heads]
    bcum_row = [cum_rows[M_HEADS + h:M_HEADS + h + 1, :] for h in heads]
    bcum_col = [jnp.sum(jnp.where(causal, f_row[h], 0.0), axis=-1, keepdims=True) for h in heads]
    i_col = [jnp.sum(jnp.where(diag, i_row[h], 0.0), axis=-1, keepdims=True) for h in heads]
    m_prev = [m_ref[h, 0:1, 0:1] for h in heads]
    c_prev = [c_ref[h] for h in heads]
    n_prev = [n_ref[h, 0:1, :] for h in heads]

    log_d = [jnp.where(causal, bcum_col[h] - bcum_row[h] + i_row[h], -jnp.inf) for h in heads]
    log_inter = [bcum_col[h] + m_prev[h] for h in heads]
    m_row = [jnp.maximum(jnp.max(log_d[h], axis=-1, keepdims=True), log_inter[h]) for h in heads]
    inter = [jnp.exp(log_inter[h] - m_row[h]) for h in heads]
    s = [_dot_nt(qb[h], kb[h]) * (scale * jnp.exp(log_d[h] - m_row[h])) for h in heads]
    inter_s = [scale * inter[h] for h in heads]
    qc = [_dot(qb[h], c_prev[h].astype(BF16)) for h in heads]
    num = [_dot(s[h].astype(BF16), vb[h]) + inter_s[h] * qc[h] for h in heads]
    den = [jnp.sum(s[h], axis=-1, keepdims=True) + inter_s[h] * jnp.sum(q[h] * n_prev[h], axis=-1, keepdims=True)
           for h in heads]
    inv_den = [1.0 / jnp.maximum(jnp.abs(den[h]), jnp.exp(-m_row[h])) for h in heads]

    b_last = [bcum_col[h][lc - 1:lc, :] for h in heads]
    log_w = [b_last[h] - bcum_col[h] + i_col[h] for h in heads]
    m_new = [jnp.maximum(b_last[h] + m_prev[h], jnp.max(log_w[h], axis=0, keepdims=True)) for h in heads]
    decay = [jnp.exp(b_last[h] + m_prev[h] - m_new[h]) for h in heads]
    kw = [k[h] * jnp.exp(log_w[h] - m_new[h]) for h in heads]
    for h in heads:
        c_ref[h] = decay[h] * c_prev[h] + _dot_tn(kw[h].astype(BF16), vb[h])
        n_ref[h] = jnp.broadcast_to(decay[h] * n_prev[h] + jnp.sum(kw[h], axis=0, keepdims=True), n_ref.shape[1:])
        m_ref[h] = jnp.broadcast_to(m_new[h], m_ref.shape[1:])

    msq = [jnp.mean(jnp.square(num[h]), axis=-1, keepdims=True) for h in heads]
    hn = jnp.concatenate([num[h] * (inv_den[h] * lax.rsqrt(jnp.square(inv_den[h]) * msq[h] + 1e-6))
                          for h in heads], axis=1)
    h_ref[0] = (hn * ng_ref[...] * _sigmoid(o_ref[0].astype(F32))).astype(h_ref.dtype)


def _mlstm_core(proj, gates, norm_g):
    bsz, seq, _ = proj.shape
    tg = gates.shape[-1]
    lc = min(M_CHUNK, seq)
    nc = seq // lc
    assert tg % lc == 0 and seq % tg == 0
    per_tile = tg // lc
    hdk = M_HEADS * M_DK
    hdv = M_HEADS * M_DV
    assert 2 * hdk == hdv
    return pl.pallas_call(
        _mlstm_body,
        grid=(bsz, nc),
        in_specs=[pl.BlockSpec((1, lc, hdk), lambda b, c: (b, c, 0)),
                  pl.BlockSpec((1, lc, hdk), lambda b, c: (b, c, 1)),
                  pl.BlockSpec((1, lc, hdv), lambda b, c: (b, c, 1)),
                  pl.BlockSpec((1, lc, hdv), lambda b, c: (b, c, 2)),
                  pl.BlockSpec((1, 2 * M_HEADS, lc), lambda b, c: (b * (seq // tg) + c // per_tile, 0, c % per_tile)),
                  pl.BlockSpec((1, hdv), lambda b, c: (0, 0))],
        out_specs=pl.BlockSpec((1, lc, hdv), lambda b, c: (b, c, 0)),
        out_shape=jax.ShapeDtypeStruct((bsz, seq, hdv), BF16),
        scratch_shapes=[pltpu.VMEM((M_HEADS, M_DK, M_DV), F32),
                        pltpu.VMEM((M_HEADS, SUBLANES, M_DK), F32),
                        pltpu.VMEM((M_HEADS, SUBLANES, LANES), F32)],
        compiler_params=_cparams(("parallel", "arbitrary")),
        name="mlstm_chunkwise",
    )(proj, proj, proj, proj, gates, norm_g.reshape(1, hdv))


def _mlstm_mixer(x, w_in, b_i, b_f, conv_w, conv_b, norm_g, riders):
    bsz, seq, d = x.shape
    m = bsz * seq
    hdv = M_HEADS * M_DV
    n_main = 2 * M_HEADS * M_DK + 2 * hdv
    xf = x.reshape(m, d)
    pad = LANES - 2 * M_HEADS
    w_gate = jnp.pad(w_in[:, n_main:], ((0, 0), (0, pad))).astype(BF16)
    b_gate = jnp.pad(jnp.concatenate([b_i, b_f]), (0, pad))
    proj, gates, casted = _proj_conv_silu(xf, w_in.astype(BF16), n_main, w_gate, b_gate, conv_w, conv_b, riders,
                                          seq=seq, tm=1024, col_chunk=256)
    proj = proj.reshape(bsz, seq, n_main)
    return _mlstm_core(proj, gates, norm_g), casted


def _wkv_body(r_ref, k_ref, v_ref, tail_ref, w2_ref, a2_ref, g2_ref, w0_ref, a0_ref, kk_ref, ka_ref, rk_ref,
              gng_ref, gnb_ref, o_ref, s_ref, *, n_tanh):
    tb = r_ref.shape[1]
    lc = min(R_CHUNK, tb)
    chunks = range(tb // lc)
    width = r_ref.shape[2]
    gw = R_GROUP * R_N
    sw = R_STATE_HEADS * R_N
    rows = R_GROUP * lc
    assert tb % lc == 0 and gw % sw == 0 and width % gw == 0

    @pl.when(pl.program_id(2) == 0)
    def _():
        s_ref[...] = jnp.zeros_like(s_ref)

    groups = range(width // gw)
    tiles = range(width // sw)
    tls = [slice(j * sw, (j + 1) * sw) for j in tiles]
    brow = lax.broadcasted_iota(jnp.int32, (sw, sw), 0) // R_N
    bcol = lax.broadcasted_iota(jnp.int32, (sw, sw), 1) // R_N
    same_head = brow == bcol
    head_ones = jnp.where(same_head, 1.0, 0.0).astype(BF16)

    def seg_sum(x, parts=1):
        pieces = (x.astype(BF16),) if parts == 1 else _split_bf16(x)
        ones = jnp.concatenate([head_ones] * len(pieces), axis=0)
        return jnp.concatenate([_dot(jnp.concatenate([piece[:, sl] for piece in pieces], axis=1), ones)
                                for sl in tls], axis=1)

    r = r_ref[0].astype(F32)
    k = k_ref[0].astype(F32)
    v = v_ref[0].astype(F32)
    n_wa = w2_ref.shape[0]
    tail = tail_ref[0].astype(F32)
    t_wa = tail[:, :n_wa]
    is_tanh = lax.broadcasted_iota(jnp.int32, (1, n_wa), 1) < n_tanh
    f_wa = jnp.where(is_tanh, jnp.tanh(t_wa), t_wa).astype(BF16)
    gate = _dot(_sigmoid(tail[:, n_wa:]).astype(BF16), g2_ref[...])
    ld = (-R_DECAY_SCALE) * _sigmoid(w0_ref[...] + _dot(f_wa, w2_ref[...]))
    a_lr = _sigmoid(a0_ref[...] + _dot(f_wa, a2_ref[...]))
    kk = k * kk_ref[...]
    kk = kk * lax.rsqrt(jnp.maximum(seg_sum(jnp.square(kk)), 1e-24))
    k = k * (a_lr * ka_ref[...] + (1.0 - ka_ref[...]))
    b_ = kk * a_lr

    ti = lax.broadcasted_iota(jnp.int32, (tb, tb), 0)
    tj = lax.broadcasted_iota(jnp.int32, (tb, tb), 1)
    in_chunk_tri = jnp.logical_and(tj <= ti, ti // lc == tj // lc)
    tri = jnp.where(in_chunk_tri, 1.0, 0.0).astype(BF16)
    cum = _dot(jnp.concatenate([tri, tri], axis=1), jnp.concatenate(_split_bf16(ld), axis=0))
    e_pos = jnp.exp(cum)
    e_neg = jnp.exp(-cum)
    w_end = [jnp.exp(cum[(c + 1) * lc - 1:(c + 1) * lc, :]) for c in chunks]
    a_til = -kk * jnp.exp(cum - ld)
    r_til = r * e_pos
    b_til = b_ * e_neg
    k_til = k * e_neg

    def stacker(n):
        lane_head = lax.broadcasted_iota(jnp.int32, (lc, n), 1) // (n // R_GROUP)
        sel = [lane_head == h for h in range(R_GROUP)]
        return lambda x: jnp.concatenate([jnp.where(s_, x, jnp.zeros_like(x)) for s_ in sel], axis=0)

    stack_c = stacker(gw)
    stack_t = stacker(rows)
    trow = lax.broadcasted_iota(jnp.int32, (lc, rows), 0)
    tcol = lax.broadcasted_iota(jnp.int32, (lc, rows), 1) % lc
    strict = tcol < trow
    causal = tcol <= trow
    eye = jnp.where(tcol == trow, 1.0, 0.0).astype(F32)
    level_masks = []
    s = 1
    while s < lc:
        lo, hi = s.bit_length() - 1, s.bit_length()
        level_masks.append(jnp.logical_and((trow >> hi) == (tcol >> hi), (trow >> lo) != (tcol >> lo)))
        s *= 2

    units = [(c, g) for c in chunks for g in groups]
    tile = lambda x, c, g: x[c * lc:(c + 1) * lc, g * gw:(g + 1) * gw]
    ar_b = {u: jnp.concatenate([tile(a_til, *u), tile(r_til, *u)], axis=0).astype(BF16) for u in units}
    bk_b = {u: (tile(b_til, *u).astype(BF16), tile(k_til, *u).astype(BF16)) for u in units}
    bk_s = {u: jnp.concatenate([stack_c(bk_b[u][0]), stack_c(bk_b[u][1])], axis=0) for u in units}
    v_b = {u: tile(v, *u).astype(BF16) for u in units}
    v_s = {u: stack_c(v_b[u]) for u in units}
    bk_cat = {u: jnp.concatenate(bk_b[u], axis=0) for u in units}

    pm = {u: _dot_nt(ar_b[u], bk_s[u]) for u in units}
    n_ab = {u: jnp.where(strict, pm[u][:lc, :rows], 0.0) for u in units}
    m_xk = {u: jnp.concatenate([jnp.where(strict, pm[u][:lc, rows:], 0.0),
                                jnp.where(causal, pm[u][lc:, rows:], 0.0)], axis=0).astype(BF16) for u in units}
    m_rb = {u: jnp.where(causal, pm[u][lc:, :rows], 0.0).astype(BF16) for u in units}

    n_b = {u: n_ab[u].astype(BF16) for u in units}
    zero_b = jnp.zeros((lc, rows), BF16)
    t_inv = {u: eye + jnp.where(level_masks[0], n_ab[u], 0.0) for u in units}
    for msk in level_masks[1:]:
        t_b = {u: t_inv[u].astype(BF16) for u in units}
        half = {u: _dot(t_b[u], stack_t(jnp.where(msk, n_b[u], zero_b))) for u in units}
        t_inv = {u: t_inv[u] + _dot(half[u].astype(BF16), stack_t(t_b[u])) for u in units}
    t_b = {u: t_inv[u].astype(BF16) for u in units}
    resid = {u: eye - t_b[u].astype(F32) + _dot(n_b[u], stack_t(t_b[u])) for u in units}
    t_fix = {u: _dot(t_b[u], stack_t(resid[u].astype(BF16))).astype(BF16) for u in units}

    state = [s_ref[j] for j in tiles]
    tiles_of = lambda g: range(g * (gw // sw), (g + 1) * (gw // sw))
    sub = lambda x, g, j: x[:, (j - g * (gw // sw)) * sw:(j - g * (gw // sw) + 1) * sw]
    y_rows = []
    for c in chunks:
        us = [(c, g) for g in groups]
        zy = [jnp.concatenate([_dot_nt(sub(ar_b[u], u[1], j), state[j].astype(BF16)) for j in tiles_of(u[1])], axis=1)
              + _dot(m_xk[u], v_s[u]) for u in us]
        ub = []
        for i, u in enumerate(us):
            z_s = stack_c(zy[i][:lc].astype(BF16))
            ub.append(_dot(jnp.concatenate([t_b[u], t_fix[u]], axis=1), jnp.concatenate([z_s, z_s], axis=0)).astype(BF16))
        y_rows.append(jnp.concatenate([zy[i][lc:] + _dot(m_rb[u], stack_c(ub[i])) for i, u in enumerate(us)], axis=1))
        for i, u in enumerate(us):
            for j in tiles_of(u[1]):
                upd = _dot_tn(jnp.concatenate([sub(ub[i], u[1], j), sub(v_b[u], u[1], j)], axis=0),
                              sub(bk_cat[u], u[1], j))
                state[j] = (state[j] + jnp.where(same_head, upd, 0.0)) * w_end[c][:, tls[j]]
    for j in tiles:
        s_ref[j] = state[j]
    y = jnp.concatenate(y_rows, axis=0)

    mu = seg_sum(y, parts=2) * (1.0 / R_N)
    yc = y - mu
    var = seg_sum(jnp.square(yc)) * (1.0 / R_N)
    yn = yc * lax.rsqrt(var + R_GN_EPS) * gng_ref[...] + gnb_ref[...]
    bonus = seg_sum(r * k * rk_ref[...]) * v
    o_ref[0] = ((yn + bonus) * gate).astype(o_ref.dtype)


def _wkv_core(proj, w2, a2, g2, w0, a0, k_k, k_a, r_k, gn_g, gn_b):
    bsz, seq, n_proj = proj.shape
    d = w0.shape[-1]
    lw, la, lg = w2.shape[0], a2.shape[0], g2.shape[0]
    n_tail = lw + la + lg
    assert n_proj == 3 * d + n_tail and (3 * d) % n_tail == 0
    gw = R_GROUP * R_N
    sw = R_STATE_HEADS * R_N
    width = R_GROUPS_PER_STEP * gw
    nb = d // width
    tb = min(R_CHUNK * R_CHUNKS_PER_STEP, seq)
    nc = seq // tb
    w2p = jnp.concatenate([w2, jnp.zeros_like(a2)], axis=0).astype(BF16)
    a2p = jnp.concatenate([jnp.zeros_like(w2), a2], axis=0).astype(BF16)
    tile = lambda off: pl.BlockSpec((1, tb, width), lambda b, g, c: (b, c, off * nb + g))
    tail = pl.BlockSpec((1, tb, n_tail), lambda b, g, c: (b, c, 3 * d // n_tail))
    wa = pl.BlockSpec((lw + la, width), lambda b, g, c: (0, g))
    wg = pl.BlockSpec((lg, width), lambda b, g, c: (0, g))
    par = pl.BlockSpec((1, width), lambda b, g, c: (0, g))
    row = lambda p: p.reshape(1, d)
    return pl.pallas_call(
        functools.partial(_wkv_body, n_tanh=lw),
        grid=(bsz, nb, nc),
        in_specs=[tile(0), tile(1), tile(2), tail, wa, wa, wg] + [par] * 7,
        out_specs=pl.BlockSpec((1, tb, width), lambda b, g, c: (b, c, g)),
        out_shape=jax.ShapeDtypeStruct((bsz, seq, d), BF16),
        scratch_shapes=[pltpu.VMEM((width // sw, sw, sw), F32)],
        compiler_params=_cparams(("parallel", "parallel", "arbitrary")),
        name="wkv7_chunked",
    )(proj, proj, proj, proj, w2p, a2p, g2.astype(BF16),
      row(w0), row(a0), row(k_k), row(k_a), row(r_k), row(gn_g), row(gn_b))


def _rwkv_mixer(x, w_in, mu, w0, w2, a0, a2, g2, k_k, k_a, r_k, gn_g, gn_b, riders):
    bsz, seq, d = x.shape
    n_proj = w_in.shape[1]
    proj, casted = _proj_shift_lerp(x.reshape(bsz * seq, d), w_in.astype(BF16), mu, riders,
                                    seq=seq, tm=1024, col_chunk=256)
    return _wkv_core(proj.reshape(bsz, seq, n_proj), w2, a2, g2, w0, a0, k_k, k_a, r_k, gn_g, gn_b), casted


def kernel(x, mlstm_w_in, mlstm_b_i, mlstm_b_f, mlstm_conv_w, mlstm_conv_b, mlstm_norm_g, mlstm_w_out, rwkv_w_in, rwkv_mu, rwkv_w0, rwkv_w2, rwkv_a0, rwkv_a2, rwkv_g2, rwkv_k_k, rwkv_k_a, rwkv_r_k, rwkv_gn_g, rwkv_gn_b, rwkv_w_out, ln_mix_g, ln_mix_b, mlp_w1, mlp_w2, ln_ffn_g, ln_ffn_b):
    bsz, seq, d = x.shape
    m = bsz * seq
    xf = x.reshape(m, d)
    for layer in range(DEPTH):
        j = layer // 2
        riders = [(mlp_w1, layer), (mlp_w2, layer)]
        if layer % 2 == 0:
            if layer + 1 < DEPTH:
                riders.append((rwkv_w_in, (layer + 1) // 2))
            mix, casted = _mlstm_mixer(xf.reshape(bsz, seq, d), mlstm_w_in[j], mlstm_b_i[j], mlstm_b_f[j],
                                       mlstm_conv_w[j], mlstm_conv_b[j], mlstm_norm_g[j], riders)
            w1, w2 = casted[:2]
            w_out = mlstm_w_out[j]
        else:
            w_in = casted[2] if len(casted) > 2 else rwkv_w_in[j]
            mix, (w1, w2) = _rwkv_mixer(xf.reshape(bsz, seq, d), w_in, rwkv_mu[j], rwkv_w0[j], rwkv_w2[j],
                                        rwkv_a0[j], rwkv_a2[j], rwkv_g2[j], rwkv_k_k[j], rwkv_k_a[j],
                                        rwkv_r_k[j].reshape(-1), rwkv_gn_g[j], rwkv_gn_b[j], riders)
            w_out = rwkv_w_out[j]
        xf = _mix_ffn(mix.reshape(m, -1), w_out.astype(BF16), xf, ln_mix_g[layer], ln_mix_b[layer],
                      w1, w2, ln_ffn_g[layer], ln_ffn_b[layer], tm=512, ff_chunk=1024, row_split=2)
    return xf.reshape(bsz, seq, d)
```

```python
import functools
import math

import jax
import jax.numpy as jnp
from jax import lax
from jax.experimental import pallas as pl
from jax.experimental.pallas import tpu as pltpu

F32 = jnp.float32
BF16 = jnp.bfloat16

DEPTH = 2
DN_ALPHA = (2.0 * DEPTH) ** 0.25
LN_EPS = 1e-5

M_HEADS = 4
M_DK = 128
M_DV = 256
M_GATE_CAP = 15.0
M_CHUNK = 256
M_CHUNKS_PER_STEP = 2
M_NORM_EPS = 1e-6

R_N = 64
R_GN_EPS = 64e-5
R_DECAY_SCALE = math.exp(-0.5)
R_CHUNK = 64
R_GROUP = 2
R_STATE_HEADS = 2
R_GROUPS_PER_STEP = 8
R_CHUNKS_PER_STEP = 4

LANES = 128
SUBLANES = 8
VMEM_LIMIT = 52 * 1024 * 1024


def _cparams(sem):
    return pltpu.CompilerParams(dimension_semantics=sem, vmem_limit_bytes=VMEM_LIMIT)


def _dot(a, b):
    return jnp.dot(a, b, preferred_element_type=F32)


def _dot_nt(a, b):
    return lax.dot_general(a, b, (((1,), (1,)), ((), ())), preferred_element_type=F32)


def _dot_tn(a, b):
    return lax.dot_general(a, b, (((0,), (0,)), ((), ())), preferred_element_type=F32)


def _softplus(z):
    return jnp.maximum(z, 0.0) + jnp.log1p(jnp.exp(-jnp.abs(z)))


def _sigmoid(z):
    return 0.5 + 0.5 * jnp.tanh(0.5 * z)


def _split_bf16(x):
    hi = x.astype(BF16)
    return hi, (x - hi.astype(F32)).astype(BF16)


def _split3_bf16(x):
    hi, rest = x.astype(BF16), x
    rest = rest - hi.astype(F32)
    mid = rest.astype(BF16)
    return hi, mid, (rest - mid.astype(F32)).astype(BF16)


def _cast_rider_specs(stacked, layer, steps):
    _, rows, cols = stacked.shape
    assert rows % steps == 0 and (rows // steps) % (2 * SUBLANES) == 0
    slab = rows // steps
    return (pl.BlockSpec((None, slab, cols), lambda i: (layer, i, 0)),
            pl.BlockSpec((slab, cols), lambda i: (i, 0)),
            jax.ShapeDtypeStruct((rows, cols), BF16))


def _run_cast_riders(in_refs, out_refs):
    for src_ref, dst_ref in zip(in_refs, out_refs):
        dst_ref[...] = src_ref[...].astype(BF16)


def _proj_lerp_body(x_ref, w_ref, mu_ref, *refs, tiles_per_seq, col_chunk, n_riders):
    o_ref, carry_ref = refs[n_riders], refs[-1]
    _run_cast_riders(refs[:n_riders], refs[n_riders + 1:-1])

    @pl.when(pl.program_id(0) % tiles_per_seq == 0)
    def _():
        carry_ref[...] = jnp.zeros_like(carry_ref)

    xb = x_ref[...].astype(BF16)
    tm = xb.shape[0]
    first = lax.broadcasted_iota(jnp.int32, (SUBLANES, col_chunk), 0) == 0

    def epilogue(acc, cols):
        rolled = pltpu.roll(acc, 1, 0)
        top = jnp.where(first, carry_ref[0:1, cols], rolled[0:SUBLANES])
        shifted = jnp.concatenate([top, rolled[SUBLANES:]], axis=0)
        o_ref[:, cols] = (acc + mu_ref[:, cols] * (shifted - acc)).astype(o_ref.dtype)
        carry_ref[:, cols] = jnp.broadcast_to(acc[tm - 1:tm, :], (SUBLANES, col_chunk))

    chunks = [slice(c0, c0 + col_chunk) for c0 in range(0, w_ref.shape[1], col_chunk)]
    pending = None
    for cols in chunks:
        acc = _dot(xb, w_ref[:, cols])
        if pending is not None:
            epilogue(*pending)
        pending = (acc, cols)
    epilogue(*pending)


def _proj_shift_lerp(x, w, mu, riders, *, seq, tm, col_chunk):
    m, k = x.shape
    n = w.shape[1]
    tm = min(tm, seq)
    assert seq % tm == 0 and n % col_chunk == 0
    rider_specs = [_cast_rider_specs(a, layer, m // tm) for a, layer in riders]
    out = pl.pallas_call(
        functools.partial(_proj_lerp_body, tiles_per_seq=seq // tm, col_chunk=col_chunk, n_riders=len(riders)),
        grid=(m // tm,),
        in_specs=[pl.BlockSpec((tm, k), lambda i: (i, 0)),
                  pl.BlockSpec((k, n), lambda i: (0, 0), pipeline_mode=pl.Buffered(1)),
                  pl.BlockSpec((1, n), lambda i: (0, 0), pipeline_mode=pl.Buffered(1))] + [s[0] for s in rider_specs],
        out_specs=[pl.BlockSpec((tm, n), lambda i: (i, 0))] + [s[1] for s in rider_specs],
        out_shape=[jax.ShapeDtypeStruct((m, n), F32)] + [s[2] for s in rider_specs],
        scratch_shapes=[pltpu.VMEM((SUBLANES, n), F32)],
        compiler_params=_cparams(("arbitrary",)),
        name="proj_token_shift",
    )(x, w, mu.reshape(1, n), *[a for a, _ in riders])
    return out[0], out[1:]


def _proj_conv_body(x_ref, w_ref, wg_ref, gb_ref, cw_ref, cb_ref, *refs, tiles_per_seq, col_chunk, n_riders):
    o_ref, g_ref, carry_ref = refs[n_riders], refs[n_riders + 1], refs[-1]
    _run_cast_riders(refs[:n_riders], refs[n_riders + 2:-1])

    @pl.when(pl.program_id(0) % tiles_per_seq == 0)
    def _():
        carry_ref[...] = jnp.zeros_like(carry_ref)

    xb = x_ref[...].astype(BF16)
    tm = xb.shape[0]
    taps, n_conv = cw_ref.shape

    def epilogue(acc, cols):
        if cols.start < n_conv:
            ext = jnp.concatenate([carry_ref[:, cols], acc], axis=0)
            conv = cw_ref[0:1, cols] * ext
            for tap in range(1, taps):
                conv = pltpu.roll(conv, 1, 0) + cw_ref[tap:tap + 1, cols] * ext
            conv = conv[SUBLANES:] + cb_ref[:, cols]
            o_ref[:, cols] = (conv * _sigmoid(conv)).astype(o_ref.dtype)
            carry_ref[:, cols] = acc[tm - SUBLANES:tm]
        else:
            o_ref[:, cols] = acc.astype(o_ref.dtype)

    chunks = [slice(c0, c0 + col_chunk) for c0 in range(0, o_ref.shape[1], col_chunk)]
    heavy = [c for c in chunks if c.start < n_conv]
    light = [c for c in chunks if c.start >= n_conv]
    per_heavy = len(light) // max(len(heavy), 1)
    order = []
    for i, c in enumerate(heavy):
        order += [c] + light[i * per_heavy:(i + 1) * per_heavy]
    order += light[len(heavy) * per_heavy:]
    pending = None
    for cols in order:
        acc = _dot(xb, w_ref[:, cols])
        if pending is not None:
            epilogue(*pending)
        pending = (acc, cols)
    capped = M_GATE_CAP * jnp.tanh((_dot(xb, wg_ref[...]) + gb_ref[...]) * (1.0 / M_GATE_CAP))
    is_input_gate = lax.broadcasted_iota(jnp.int32, capped.shape, 1) < M_HEADS
    log_gate = jnp.where(is_input_gate, capped, -_softplus(-capped))
    g_ref[0] = log_gate.T[0:2 * M_HEADS, :]
    epilogue(*pending)


def _proj_conv_silu(x, w, n, w_gate, b_gate, conv_w, conv_b, riders, *, seq, tm, col_chunk):
    m, k = x.shape
    n_gate = w_gate.shape[1]
    assert 2 * M_HEADS == SUBLANES and n_gate % LANES == 0
    taps, n_conv = conv_w.shape
    tm = min(tm, seq)
    assert seq % tm == 0 and n % col_chunk == 0 and n_conv % col_chunk == 0 and taps <= SUBLANES
    resident = lambda shape: pl.BlockSpec(shape, lambda i: (0, 0), pipeline_mode=pl.Buffered(1))
    rider_specs = [_cast_rider_specs(a, layer, m // tm) for a, layer in riders]
    out = pl.pallas_call(
        functools.partial(_proj_conv_body, tiles_per_seq=seq // tm, col_chunk=col_chunk, n_riders=len(riders)),
        grid=(m // tm,),
        in_specs=[pl.BlockSpec((tm, k), lambda i: (i, 0)), resident(w.shape), resident((k, n_gate)),
                  resident((1, n_gate)),
                  resident((taps, n_conv)), resident((1, n_conv))] + [s[0] for s in rider_specs],
        out_specs=[pl.BlockSpec((tm, n), lambda i: (i, 0)),
                   pl.BlockSpec((1, 2 * M_HEADS, tm), lambda i: (i, 0, 0))] + [s[1] for s in rider_specs],
        out_shape=[jax.ShapeDtypeStruct((m, n), BF16),
                   jax.ShapeDtypeStruct((m // tm, 2 * M_HEADS, tm), F32)] + [s[2] for s in rider_specs],
        scratch_shapes=[pltpu.VMEM((SUBLANES, n_conv), F32)],
        compiler_params=_cparams(("arbitrary",)),
        name="proj_conv_silu",
    )(x, w, w_gate, b_gate.reshape(1, n_gate), conv_w, conv_b.reshape(1, n_conv), *[a for a, _ in riders])
    return out[0], out[1], out[2:]


def _layer_norm(y, g, b):
    mu = jnp.mean(y, axis=-1, keepdims=True)
    yc = y - mu
    var = jnp.mean(jnp.square(yc), axis=-1, keepdims=True)
    return yc * lax.rsqrt(var + LN_EPS) * g + b


def _mix_ffn_body(mix_ref, wo_ref, res_ref, g1_ref, b1_ref, w1_ref, w2_ref, g2_ref, b2_ref, o_ref, *,
                  ff_chunk, row_split):
    sub = mix_ref.shape[0] // row_split

    def stages(rows):
        st = {}

        def out_proj():
            st["y"] = DN_ALPHA * res_ref[rows, :] + _dot(mix_ref[rows, :], wo_ref[...])

        def norm1():
            st["x1"] = _layer_norm(st.pop("y"), g1_ref[...], b1_ref[...])
            st["x1b"] = st["x1"].astype(BF16)
            st["acc"] = DN_ALPHA * st.pop("x1")

        def up(f):
            st["hid"] = _dot(st["x1b"], w1_ref[:, f:f + ff_chunk])

        def act():
            st["hid"] = jnp.square(jnp.maximum(st["hid"], 0.0)).astype(BF16)

        def down(f):
            st["acc"] = st["acc"] + _dot(st.pop("hid"), w2_ref[f:f + ff_chunk, :])

        def norm2():
            o_ref[rows, :] = _layer_norm(st.pop("acc"), g2_ref[...], b2_ref[...])

        seq = [out_proj, norm1]
        for f in range(0, w1_ref.shape[1], ff_chunk):
            seq += [functools.partial(up, f), act, functools.partial(down, f)]
        return seq + [norm2]

    threads = [stages(slice(h * sub, (h + 1) * sub)) for h in range(row_split)]
    for t in range(len(threads[0]) + row_split - 1):
        for h, seq in enumerate(threads):
            if 0 <= t - h < len(seq):
                seq[t - h]()


def _mix_ffn(mix, w_out, res, g1, b1, w1, w2, g2, b2, *, tm, ff_chunk, row_split):
    m, k = mix.shape
    d = w_out.shape[1]
    dff = w1.shape[1]
    tm = min(tm, m)
    assert m % tm == 0 and dff % ff_chunk == 0 and tm % (SUBLANES * row_split) == 0
    resident = lambda shape: pl.BlockSpec(shape, lambda i: (0, 0), pipeline_mode=pl.Buffered(1))
    rows = lambda width: pl.BlockSpec((tm, width), lambda i: (i, 0))
    vec = lambda p: p.reshape(1, d)
    return pl.pallas_call(
        functools.partial(_mix_ffn_body, ff_chunk=ff_chunk, row_split=row_split),
        grid=(m // tm,),
        in_specs=[rows(k), resident((k, d)), rows(d), resident((1, d)), resident((1, d)),
                  resident((d, dff)), resident((dff, d)), resident((1, d)), resident((1, d))],
        out_specs=rows(d),
        out_shape=jax.ShapeDtypeStruct((m, d), F32),
        compiler_params=_cparams(("parallel",)),
        name="outproj_ln_ffn_ln",
    )(mix, w_out, res, vec(g1), vec(b1), w1, w2, vec(g2), vec(b2))


def _mlstm_body(q_ref, k_ref, v_ref, o_ref, gate_ref, ng_ref, h_ref, c_ref, n_ref, m_ref, *, lc):
    heads = range(M_HEADS)

    @pl.when(pl.program_id(1) == 0)
    def _():
        c_ref[...] = jnp.zeros_like(c_ref)
        n_ref[...] = jnp.zeros_like(n_ref)
        m_ref[...] = jnp.zeros_like(m_ref)

    ri = lax.broadcasted_iota(jnp.int32, (lc, lc), 0)
    ci = lax.broadcasted_iota(jnp.int32, (lc, lc), 1)
    causal = ci <= ri
    diag = ci == ri
    tri_u3 = jnp.concatenate([jnp.where(ri <= ci, 1.0, 0.0).astype(BF16)] * 3, axis=0)
    scale = M_DK ** -0.5
    m_prev = [m_ref[h, 0:1, 0:1] for h in heads]
    c_prev = [c_ref[h] for h in heads]
    n_prev = [n_ref[h, 0:1, :] for h in heads]

    for c0 in range(0, q_ref.shape[1], lc):
        rows = slice(c0, c0 + lc)
        g_rows = gate_ref[0, :, rows]
        cum_rows = _dot(jnp.concatenate(_split3_bf16(g_rows), axis=1), tri_u3)
        q_all = q_ref[0, rows, :]
        k_all = k_ref[0, rows, :]
        v_all = v_ref[0, rows, :]
        qb = [q_all[:, h * M_DK:(h + 1) * M_DK] for h in heads]
        kb = [k_all[:, h * M_DK:(h + 1) * M_DK] for h in heads]
        vb = [v_all[:, h * M_DV:(h + 1) * M_DV] for h in heads]
        q = [qb[h].astype(F32) for h in heads]
        k = [kb[h].astype(F32) for h in heads]
        i_row = [g_rows[h:h + 1, :] for h in heads]
        f_row = [g_rows[M_HEADS + h:M_HEADS + h + 1, :] for h in heads]
        bcum_row = [cum_rows[M_HEADS + h:M_HEADS + h + 1, :] for h in heads]
        bcum_col = [jnp.sum(jnp.where(causal, f_row[h], 0.0), axis=-1, keepdims=True) for h in heads]
        i_col = [jnp.sum(jnp.where(diag, i_row[h], 0.0), axis=-1, keepdims=True) for h in heads]

        log_d = [jnp.where(causal, bcum_col[h] - bcum_row[h] + i_row[h], -jnp.inf) for h in heads]
        log_inter = [bcum_col[h] + m_prev[h] for h in heads]
        m_row = [jnp.maximum(jnp.max(log_d[h], axis=-1, keepdims=True), log_inter[h]) for h in heads]
        inter = [jnp.exp(log_inter[h] - m_row[h]) for h in heads]
        s = [_dot_nt(qb[h], kb[h]) * (scale * jnp.exp(log_d[h] - m_row[h])) for h in heads]
        inter_s = [scale * inter[h] for h in heads]
        qc = [_dot(qb[h], c_prev[h].astype(BF16)) for h in heads]
        num = [_dot(s[h].astype(BF16), vb[h]) + inter_s[h] * qc[h] for h in heads]
        den = [jnp.sum(s[h], axis=-1, keepdims=True)
               + inter_s[h] * jnp.sum(q[h] * n_prev[h], axis=-1, keepdims=True) for h in heads]
        inv_den = [1.0 / jnp.maximum(jnp.abs(den[h]), jnp.exp(-m_row[h])) for h in heads]

        b_last = [bcum_col[h][lc - 1:lc, :] for h in heads]
        log_w = [b_last[h] - bcum_col[h] + i_col[h] for h in heads]
        m_new = [jnp.maximum(b_last[h] + m_prev[h], jnp.max(log_w[h], axis=0, keepdims=True)) for h in heads]
        decay = [jnp.exp(b_last[h] + m_prev[h] - m_new[h]) for h in heads]
        kw = [k[h] * jnp.exp(log_w[h] - m_new[h]) for h in heads]
        c_prev = [decay[h] * c_prev[h] + _dot_tn(kw[h].astype(BF16), vb[h]) for h in heads]
        n_prev = [decay[h] * n_prev[h] + jnp.sum(kw[h], axis=0, keepdims=True) for h in heads]
        m_prev = m_new

        msq = [jnp.mean(jnp.square(num[h]), axis=-1, keepdims=True) for h in heads]
        hn = jnp.concatenate([num[h] * (inv_den[h] * lax.rsqrt(jnp.square(inv_den[h]) * msq[h] + M_NORM_EPS))
                              for h in heads], axis=1)
        h_ref[0, rows, :] = (hn * ng_ref[...] * _sigmoid(o_ref[0, rows, :].astype(F32))).astype(h_ref.dtype)

    for h in heads:
        c_ref[h] = c_prev[h]
        n_ref[h] = jnp.broadcast_to(n_prev[h], n_ref.shape[1:])
        m_ref[h] = jnp.broadcast_to(m_prev[h], m_ref.shape[1:])


def _mlstm_core(proj, gates, norm_g):
    bsz, seq, _ = proj.shape
    tg = gates.shape[-1]
    lc = min(M_CHUNK, seq)
    tb = min(lc * M_CHUNKS_PER_STEP, seq)
    nc = seq // tb
    assert tg % tb == 0 and seq % tg == 0 and tb % lc == 0
    per_tile = tg // tb
    hdk = M_HEADS * M_DK
    hdv = M_HEADS * M_DV
    assert 2 * hdk == hdv
    return pl.pallas_call(
        functools.partial(_mlstm_body, lc=lc),
        grid=(bsz, nc),
        in_specs=[pl.BlockSpec((1, tb, hdk), lambda b, c: (b, c, 0)),
                  pl.BlockSpec((1, tb, hdk), lambda b, c: (b, c, 1)),
                  pl.BlockSpec((1, tb, hdv), lambda b, c: (b, c, 1)),
                  pl.BlockSpec((1, tb, hdv), lambda b, c: (b, c, 2)),
                  pl.BlockSpec((1, 2 * M_HEADS, tb), lambda b, c: (b * (seq // tg) + c // per_tile, 0, c % per_tile)),
                  pl.BlockSpec((1, hdv), lambda b, c: (0, 0))],
        out_specs=pl.BlockSpec((1, tb, hdv), lambda b, c: (b, c, 0)),
        out_shape=jax.ShapeDtypeStruct((bsz, seq, hdv), BF16),
        scratch_shapes=[pltpu.VMEM((M_HEADS, M_DK, M_DV), F32),
                        pltpu.VMEM((M_HEADS, SUBLANES, M_DK), F32),
                        pltpu.VMEM((M_HEADS, SUBLANES, LANES), F32)],
        compiler_params=_cparams(("parallel", "arbitrary")),
        name="mlstm_chunkwise",
    )(proj, proj, proj, proj, gates, norm_g.reshape(1, hdv))


def _mlstm_mixer(x, w_in, b_i, b_f, conv_w, conv_b, norm_g, riders):
    bsz, seq, d = x.shape
    m = bsz * seq
    hdv = M_HEADS * M_DV
    n_main = 2 * M_HEADS * M_DK + 2 * hdv
    xf = x.reshape(m, d)
    pad = LANES - 2 * M_HEADS
    w_gate = jnp.pad(w_in[:, n_main:], ((0, 0), (0, pad))).astype(BF16)
    b_gate = jnp.pad(jnp.concatenate([b_i, b_f]), (0, pad))
    proj, gates, casted = _proj_conv_silu(xf, w_in.astype(BF16), n_main, w_gate, b_gate, conv_w, conv_b, riders,
                                          seq=seq, tm=1024, col_chunk=256)
    proj = proj.reshape(bsz, seq, n_main)
    return _mlstm_core(proj, gates, norm_g), casted


def _wkv_body(r_ref, k_ref, v_ref, tail_ref, w2_ref, a2_ref, g2_ref, w0_ref, a0_ref, kk_ref, ka_ref, rk_ref,
              gng_ref, gnb_ref, o_ref, s_ref, *, n_tanh):
    tb = r_ref.shape[1]
    lc = min(R_CHUNK, tb)
    chunks = range(tb // lc)
    width = r_ref.shape[2]
    gw = R_GROUP * R_N
    sw = R_STATE_HEADS * R_N
    rows = R_GROUP * lc
    assert tb % lc == 0 and gw % sw == 0 and width % gw == 0

    @pl.when(pl.program_id(2) == 0)
    def _():
        s_ref[...] = jnp.zeros_like(s_ref)

    groups = range(width // gw)
    tiles = range(width // sw)
    tls = [slice(j * sw, (j + 1) * sw) for j in tiles]
    brow = lax.broadcasted_iota(jnp.int32, (sw, sw), 0) // R_N
    bcol = lax.broadcasted_iota(jnp.int32, (sw, sw), 1) // R_N
    same_head = brow == bcol
    head_ones = jnp.where(same_head, 1.0, 0.0).astype(BF16)

    def seg_sum(x, parts=1):
        pieces = (x.astype(BF16),) if parts == 1 else _split_bf16(x)
        ones = jnp.concatenate([head_ones] * len(pieces), axis=0)
        return jnp.concatenate([_dot(jnp.concatenate([piece[:, sl] for piece in pieces], axis=1), ones)
                                for sl in tls], axis=1)

    r = r_ref[0].astype(F32)
    k = k_ref[0].astype(F32)
    v = v_ref[0].astype(F32)
    n_wa = w2_ref.shape[0]
    tail = tail_ref[0].astype(F32)
    t_wa = tail[:, :n_wa]
    is_tanh = lax.broadcasted_iota(jnp.int32, (1, n_wa), 1) < n_tanh
    f_wa = jnp.where(is_tanh, jnp.tanh(t_wa), t_wa).astype(BF16)
    gate = _dot(_sigmoid(tail[:, n_wa:]).astype(BF16), g2_ref[...])
    ld = (-R_DECAY_SCALE) * _sigmoid(w0_ref[...] + _dot(f_wa, w2_ref[...]))
    a_lr = _sigmoid(a0_ref[...] + _dot(f_wa, a2_ref[...]))
    kk = k * kk_ref[...]
    kk = kk * lax.rsqrt(jnp.maximum(seg_sum(jnp.square(kk)), 1e-24))
    k = k * (a_lr * ka_ref[...] + (1.0 - ka_ref[...]))
    b_ = kk * a_lr

    ti = lax.broadcasted_iota(jnp.int32, (tb, tb), 0)
    tj = lax.broadcasted_iota(jnp.int32, (tb, tb), 1)
    in_chunk_tri = jnp.logical_and(tj <= ti, ti // lc == tj // lc)
    tri = jnp.where(in_chunk_tri, 1.0, 0.0).astype(BF16)
    cum = _dot(jnp.concatenate([tri, tri], axis=1), jnp.concatenate(_split_bf16(ld), axis=0))
    e_pos = jnp.exp(cum)
    e_neg = jnp.exp(-cum)
    w_end = [jnp.exp(cum[(c + 1) * lc - 1:(c + 1) * lc, :]) for c in chunks]
    a_til = -kk * jnp.exp(cum - ld)
    r_til = r * e_pos
    b_til = b_ * e_neg
    k_til = k * e_neg

    def stacker(n):
        lane_head = lax.broadcasted_iota(jnp.int32, (lc, n), 1) // (n // R_GROUP)
        sel = [lane_head == h for h in range(R_GROUP)]
        return lambda x: jnp.concatenate([jnp.where(s_, x, jnp.zeros_like(x)) for s_ in sel], axis=0)

    stack_c = stacker(gw)
    stack_t = stacker(rows)
    trow = lax.broadcasted_iota(jnp.int32, (lc, rows), 0)
    tcol = lax.broadcasted_iota(jnp.int32, (lc, rows), 1) % lc
    strict = tcol < trow
    causal = tcol <= trow
    eye = jnp.where(tcol == trow, 1.0, 0.0).astype(F32)
    level_masks = []
    s = 1
    while s < lc:
        lo, hi = s.bit_length() - 1, s.bit_length()
        level_masks.append(jnp.logical_and((trow >> hi) == (tcol >> hi), (trow >> lo) != (tcol >> lo)))
        s *= 2

    units = [(c, g) for c in chunks for g in groups]
    tile = lambda x, c, g: x[c * lc:(c + 1) * lc, g * gw:(g + 1) * gw]
    ar_b = {u: jnp.concatenate([tile(a_til, *u), tile(r_til, *u)], axis=0).astype(BF16) for u in units}
    bk_b = {u: (tile(b_til, *u).astype(BF16), tile(k_til, *u).astype(BF16)) for u in units}
    bk_s = {u: jnp.concatenate([stack_c(bk_b[u][0]), stack_c(bk_b[u][1])], axis=0) for u in units}
    v_b = {u: tile(v, *u).astype(BF16) for u in units}
    v_s = {u: stack_c(v_b[u]) for u in units}
    bk_cat = {u: jnp.concatenate(bk_b[u], axis=0) for u in units}

    pm = {u: _dot_nt(ar_b[u], bk_s[u]) for u in units}
    n_ab = {u: jnp.where(strict, pm[u][:lc, :rows], 0.0) for u in units}
    m_xk = {u: jnp.concatenate([jnp.where(strict, pm[u][:lc, rows:], 0.0),
                                jnp.where(causal, pm[u][lc:, rows:], 0.0)], axis=0).astype(BF16) for u in units}
    m_rb = {u: jnp.where(causal, pm[u][lc:, :rows], 0.0).astype(BF16) for u in units}

    n_b = {u: n_ab[u].astype(BF16) for u in units}
    zero_b = jnp.zeros((lc, rows), BF16)
    t_inv = {u: eye + jnp.where(level_masks[0], n_ab[u], 0.0) for u in units}
    for msk in level_masks[1:]:
        t_b = {u: t_inv[u].astype(BF16) for u in units}
        half = {u: _dot(t_b[u], stack_t(jnp.where(msk, n_b[u], zero_b))) for u in units}
        t_inv = {u: t_inv[u] + _dot(half[u].astype(BF16), stack_t(t_b[u])) for u in units}
    t_b = {u: t_inv[u].astype(BF16) for u in units}
    resid = {u: eye - t_b[u].astype(F32) + _dot(n_b[u], stack_t(t_b[u])) for u in units}
    t_fix = {u: _dot(t_b[u], stack_t(resid[u].astype(BF16))).astype(BF16) for u in units}

    state = [s_ref[j] for j in tiles]
    tiles_of = lambda g: range(g * (gw // sw), (g + 1) * (gw // sw))
    sub = lambda x, g, j: x[:, (j - g * (gw // sw)) * sw:(j - g * (gw // sw) + 1) * sw]
    y_rows = []
    for c in chunks:
        us = [(c, g) for g in groups]
        zy = [jnp.concatenate([_dot_nt(sub(ar_b[u], u[1], j), state[j].astype(BF16)) for j in tiles_of(u[1])], axis=1)
              + _dot(m_xk[u], v_s[u]) for u in us]
        ub = []
        for i, u in enumerate(us):
            z_s = stack_c(zy[i][:lc].astype(BF16))
            ub.append(_dot(jnp.concatenate([t_b[u], t_fix[u]], axis=1), jnp.concatenate([z_s, z_s], axis=0)).astype(BF16))
        y_rows.append(jnp.concatenate([zy[i][lc:] + _dot(m_rb[u], stack_c(ub[i])) for i, u in enumerate(us)], axis=1))
        for i, u in enumerate(us):
            for j in tiles_of(u[1]):
                upd = _dot_tn(jnp.concatenate([sub(ub[i], u[1], j), sub(v_b[u], u[1], j)], axis=0),
                              sub(bk_cat[u], u[1], j))
                state[j] = (state[j] + jnp.where(same_head, upd, 0.0)) * w_end[c][:, tls[j]]
    for j in tiles:
        s_ref[j] = state[j]
    y = jnp.concatenate(y_rows, axis=0)

    mu = seg_sum(y, parts=2) * (1.0 / R_N)
    yc = y - mu
    var = seg_sum(jnp.square(yc)) * (1.0 / R_N)
    yn = yc * lax.rsqrt(var + R_GN_EPS) * gng_ref[...] + gnb_ref[...]
    bonus = seg_sum(r * k * rk_ref[...]) * v
    o_ref[0] = ((yn + bonus) * gate).astype(o_ref.dtype)


def _wkv_core(proj, w2, a2, g2, w0, a0, k_k, k_a, r_k, gn_g, gn_b):
    bsz, seq, n_proj = proj.shape
    d = w0.shape[-1]
    lw, la, lg = w2.shape[0], a2.shape[0], g2.shape[0]
    n_tail = lw + la + lg
    assert n_proj == 3 * d + n_tail and (3 * d) % n_tail == 0
    gw = R_GROUP * R_N
    sw = R_STATE_HEADS * R_N
    width = R_GROUPS_PER_STEP * gw
    nb = d // width
    tb = min(R_CHUNK * R_CHUNKS_PER_STEP, seq)
    nc = seq // tb
    w2p = jnp.concatenate([w2, jnp.zeros_like(a2)], axis=0).astype(BF16)
    a2p = jnp.concatenate([jnp.zeros_like(w2), a2], axis=0).astype(BF16)
    tile = lambda off: pl.BlockSpec((1, tb, width), lambda b, g, c: (b, c, off * nb + g))
    tail = pl.BlockSpec((1, tb, n_tail), lambda b, g, c: (b, c, 3 * d // n_tail))
    wa = pl.BlockSpec((lw + la, width), lambda b, g, c: (0, g))
    wg = pl.BlockSpec((lg, width), lambda b, g, c: (0, g))
    par = pl.BlockSpec((1, width), lambda b, g, c: (0, g))
    row = lambda p: p.reshape(1, d)
    return pl.pallas_call(
        functools.partial(_wkv_body, n_tanh=lw),
        grid=(bsz, nb, nc),
        in_specs=[tile(0), tile(1), tile(2), tail, wa, wa, wg] + [par] * 7,
        out_specs=pl.BlockSpec((1, tb, width), lambda b, g, c: (b, c, g)),
        out_shape=jax.ShapeDtypeStruct((bsz, seq, d), BF16),
        scratch_shapes=[pltpu.VMEM((width // sw, sw, sw), F32)],
        compiler_params=_cparams(("parallel", "parallel", "arbitrary")),
        name="wkv7_chunked",
    )(proj, proj, proj, proj, w2p, a2p, g2.astype(BF16),
      row(w0), row(a0), row(k_k), row(k_a), row(r_k), row(gn_g), row(gn_b))


def _rwkv_mixer(x, w_in, mu, w0, w2, a0, a2, g2, k_k, k_a, r_k, gn_g, gn_b, riders):
    bsz, seq, d = x.shape
    n_proj = w_in.shape[1]
    proj, casted = _proj_shift_lerp(x.reshape(bsz * seq, d), w_in.astype(BF16), mu, riders,
                                    seq=seq, tm=1024, col_chunk=256)
    return _wkv_core(proj.reshape(bsz, seq, n_proj), w2, a2, g2, w0, a0, k_k, k_a, r_k, gn_g, gn_b), casted


def kernel(x, mlstm_w_in, mlstm_b_i, mlstm_b_f, mlstm_conv_w, mlstm_conv_b, mlstm_norm_g, mlstm_w_out, rwkv_w_in, rwkv_mu, rwkv_w0, rwkv_w2, rwkv_a0, rwkv_a2, rwkv_g2, rwkv_k_k, rwkv_k_a, rwkv_r_k, rwkv_gn_g, rwkv_gn_b, rwkv_w_out, ln_mix_g, ln_mix_b, mlp_w1, mlp_w2, ln_ffn_g, ln_ffn_b):
    bsz, seq, d = x.shape
    m = bsz * seq
    xf = x.reshape(m, d)
    for layer in range(DEPTH):
        j = layer // 2
        riders = [(mlp_w1, layer), (mlp_w2, layer)]
        if layer % 2 == 0:
            if layer + 1 < DEPTH:
                riders.append((rwkv_w_in, (layer + 1) // 2))
            mix, casted = _mlstm_mixer(xf.reshape(bsz, seq, d), mlstm_w_in[j], mlstm_b_i[j], mlstm_b_f[j],
                                       mlstm_conv_w[j], mlstm_conv_b[j], mlstm_norm_g[j], riders)
            w1, w2 = casted[:2]
            w_out = mlstm_w_out[j]
        else:
            w_in = casted[2] if len(casted) > 2 else rwkv_w_in[j]
            mix, (w1, w2) = _rwkv_mixer(xf.reshape(bsz, seq, d), w_in, rwkv_mu[j], rwkv_w0[j], rwkv_w2[j],
                                        rwkv_a0[j], rwkv_a2[j], rwkv_g2[j], rwkv_k_k[j], rwkv_k_a[j],
                                        rwkv_r_k[j].reshape(-1), rwkv_gn_g[j], rwkv_gn_b[j], riders)
            w_out = rwkv_w_out[j]
        xf = _mix_ffn(mix.reshape(m, -1), w_out.astype(BF16), xf, ln_mix_g[layer], ln_mix_b[layer],
                      w1, w2, ln_ffn_g[layer], ln_ffn_b[layer], tm=512, ff_chunk=1024, row_split=2)
    return xf.reshape(bsz, seq, d)
```

```python
import functools
import math

import jax
import jax.numpy as jnp
from jax import lax
from jax.experimental import pallas as pl
from jax.experimental.pallas import tpu as pltpu

F32 = jnp.float32
BF16 = jnp.bfloat16

DEPTH = 2
DN_ALPHA = (2.0 * DEPTH) ** 0.25
LN_EPS = 1e-5

M_HEADS = 4
M_DK = 128
M_DV = 256
M_GATE_CAP = 15.0
M_CHUNK = 256
M_CHUNKS_PER_STEP = 4
M_NORM_EPS = 1e-6

R_N = 64
R_GN_EPS = 64e-5
R_DECAY_SCALE = math.exp(-0.5)
R_CHUNK = 64
R_GROUP = 2
R_STATE_HEADS = 2
R_GROUPS_PER_STEP = 8
R_CHUNKS_PER_STEP = 4

LANES = 128
SUBLANES = 8
VMEM_LIMIT = 52 * 1024 * 1024


def _cparams(sem):
    return pltpu.CompilerParams(dimension_semantics=sem, vmem_limit_bytes=VMEM_LIMIT)


def _dot(a, b):
    return jnp.dot(a, b, preferred_element_type=F32)


def _dot_nt(a, b):
    return lax.dot_general(a, b, (((1,), (1,)), ((), ())), preferred_element_type=F32)


def _dot_tn(a, b):
    return lax.dot_general(a, b, (((0,), (0,)), ((), ())), preferred_element_type=F32)


def _softplus(z):
    return jnp.maximum(z, 0.0) + jnp.log1p(jnp.exp(-jnp.abs(z)))


def _sigmoid(z):
    return 0.5 + 0.5 * jnp.tanh(0.5 * z)


def _split_bf16(x):
    hi = x.astype(BF16)
    return hi, (x - hi.astype(F32)).astype(BF16)


def _split3_bf16(x):
    hi, rest = x.astype(BF16), x
    rest = rest - hi.astype(F32)
    mid = rest.astype(BF16)
    return hi, mid, (rest - mid.astype(F32)).astype(BF16)


def _cast_rider_specs(stacked, layer, steps):
    _, rows, cols = stacked.shape
    assert rows % steps == 0 and (rows // steps) % (2 * SUBLANES) == 0
    slab = rows // steps
    return (pl.BlockSpec((None, slab, cols), lambda i: (layer, i, 0)),
            pl.BlockSpec((slab, cols), lambda i: (i, 0)),
            jax.ShapeDtypeStruct((rows, cols), BF16))


def _run_cast_riders(in_refs, out_refs):
    for src_ref, dst_ref in zip(in_refs, out_refs):
        dst_ref[...] = src_ref[...].astype(BF16)


def _proj_lerp_body(x_ref, w_ref, mu_ref, *refs, tiles_per_seq, col_chunk, n_riders):
    o_ref, carry_ref = refs[n_riders], refs[-1]
    _run_cast_riders(refs[:n_riders], refs[n_riders + 1:-1])

    @pl.when(pl.program_id(0) % tiles_per_seq == 0)
    def _():
        carry_ref[...] = jnp.zeros_like(carry_ref)

    xb = x_ref[...].astype(BF16)
    tm = xb.shape[0]
    first = lax.broadcasted_iota(jnp.int32, (SUBLANES, col_chunk), 0) == 0

    def epilogue(acc, cols):
        rolled = pltpu.roll(acc, 1, 0)
        top = jnp.where(first, carry_ref[0:1, cols], rolled[0:SUBLANES])
        shifted = jnp.concatenate([top, rolled[SUBLANES:]], axis=0)
        o_ref[:, cols] = (acc + mu_ref[:, cols] * (shifted - acc)).astype(o_ref.dtype)
        carry_ref[:, cols] = jnp.broadcast_to(acc[tm - 1:tm, :], (SUBLANES, col_chunk))

    chunks = [slice(c0, c0 + col_chunk) for c0 in range(0, w_ref.shape[1], col_chunk)]
    pending = None
    for cols in chunks:
        acc = _dot(xb, w_ref[:, cols])
        if pending is not None:
            epilogue(*pending)
        pending = (acc, cols)
    epilogue(*pending)


def _proj_shift_lerp(x, w, mu, riders, *, seq, tm, col_chunk):
    m, k = x.shape
    n = w.shape[1]
    tm = min(tm, seq)
    assert seq % tm == 0 and n % col_chunk == 0
    rider_specs = [_cast_rider_specs(a, layer, m // tm) for a, layer in riders]
    out = pl.pallas_call(
        functools.partial(_proj_lerp_body, tiles_per_seq=seq // tm, col_chunk=col_chunk, n_riders=len(riders)),
        grid=(m // tm,),
        in_specs=[pl.BlockSpec((tm, k), lambda i: (i, 0)),
                  pl.BlockSpec((k, n), lambda i: (0, 0), pipeline_mode=pl.Buffered(1)),
                  pl.BlockSpec((1, n), lambda i: (0, 0), pipeline_mode=pl.Buffered(1))] + [s[0] for s in rider_specs],
        out_specs=[pl.BlockSpec((tm, n), lambda i: (i, 0))] + [s[1] for s in rider_specs],
        out_shape=[jax.ShapeDtypeStruct((m, n), F32)] + [s[2] for s in rider_specs],
        scratch_shapes=[pltpu.VMEM((SUBLANES, n), F32)],
        compiler_params=_cparams(("arbitrary",)),
        name="proj_token_shift",
    )(x, w, mu.reshape(1, n), *[a for a, _ in riders])
    return out[0], out[1:]


def _proj_conv_body(x_ref, w_ref, wg_ref, gb_ref, cw_ref, cb_ref, *refs, tiles_per_seq, col_chunk, n_riders):
    o_ref, g_ref, carry_ref = refs[n_riders], refs[n_riders + 1], refs[-1]
    _run_cast_riders(refs[:n_riders], refs[n_riders + 2:-1])

    @pl.when(pl.program_id(0) % tiles_per_seq == 0)
    def _():
        carry_ref[...] = jnp.zeros_like(carry_ref)

    xb = x_ref[...].astype(BF16)
    tm = xb.shape[0]
    taps, n_conv = cw_ref.shape

    def epilogue(acc, cols):
        if cols.start < n_conv:
            ext = jnp.concatenate([carry_ref[:, cols], acc], axis=0)
            conv = cw_ref[0:1, cols] * ext
            for tap in range(1, taps):
                conv = pltpu.roll(conv, 1, 0) + cw_ref[tap:tap + 1, cols] * ext
            conv = conv[SUBLANES:] + cb_ref[:, cols]
            o_ref[:, cols] = (conv * _sigmoid(conv)).astype(o_ref.dtype)
            carry_ref[:, cols] = acc[tm - SUBLANES:tm]
        else:
            o_ref[:, cols] = acc.astype(o_ref.dtype)

    chunks = [slice(c0, c0 + col_chunk) for c0 in range(0, o_ref.shape[1], col_chunk)]
    heavy = [c for c in chunks if c.start < n_conv]
    light = [c for c in chunks if c.start >= n_conv]
    per_heavy = len(light) // max(len(heavy), 1)
    order = []
    for i, c in enumerate(heavy):
        order += [c] + light[i * per_heavy:(i + 1) * per_heavy]
    order += light[len(heavy) * per_heavy:]
    pending = None
    for cols in order:
        acc = _dot(xb, w_ref[:, cols])
        if pending is not None:
            epilogue(*pending)
        pending = (acc, cols)
    capped = M_GATE_CAP * jnp.tanh((_dot(xb, wg_ref[...]) + gb_ref[...]) * (1.0 / M_GATE_CAP))
    is_input_gate = lax.broadcasted_iota(jnp.int32, capped.shape, 1) < M_HEADS
    log_gate = jnp.where(is_input_gate, capped, -_softplus(-capped))
    g_ref[0] = log_gate.T[0:2 * M_HEADS, :]
    epilogue(*pending)


def _proj_conv_silu(x, w, n, w_gate, b_gate, conv_w, conv_b, riders, *, seq, tm, col_chunk):
    m, k = x.shape
    n_gate = w_gate.shape[1]
    assert 2 * M_HEADS == SUBLANES and n_gate % LANES == 0
    taps, n_conv = conv_w.shape
    tm = min(tm, seq)
    assert seq % tm == 0 and n % col_chunk == 0 and n_conv % col_chunk == 0 and taps <= SUBLANES
    resident = lambda shape: pl.BlockSpec(shape, lambda i: (0, 0), pipeline_mode=pl.Buffered(1))
    rider_specs = [_cast_rider_specs(a, layer, m // tm) for a, layer in riders]
    out = pl.pallas_call(
        functools.partial(_proj_conv_body, tiles_per_seq=seq // tm, col_chunk=col_chunk, n_riders=len(riders)),
        grid=(m // tm,),
        in_specs=[pl.BlockSpec((tm, k), lambda i: (i, 0)), resident(w.shape), resident((k, n_gate)),
                  resident((1, n_gate)),
                  resident((taps, n_conv)), resident((1, n_conv))] + [s[0] for s in rider_specs],
        out_specs=[pl.BlockSpec((tm, n), lambda i: (i, 0)),
                   pl.BlockSpec((1, 2 * M_HEADS, tm), lambda i: (i, 0, 0))] + [s[1] for s in rider_specs],
        out_shape=[jax.ShapeDtypeStruct((m, n), BF16),
                   jax.ShapeDtypeStruct((m // tm, 2 * M_HEADS, tm), F32)] + [s[2] for s in rider_specs],
        scratch_shapes=[pltpu.VMEM((SUBLANES, n_conv), F32)],
        compiler_params=_cparams(("arbitrary",)),
        name="proj_conv_silu",
    )(x, w, w_gate, b_gate.reshape(1, n_gate), conv_w, conv_b.reshape(1, n_conv), *[a for a, _ in riders])
    return out[0], out[1], out[2:]


def _layer_norm(y, g, b):
    mu = jnp.mean(y, axis=-1, keepdims=True)
    yc = y - mu
    var = jnp.mean(jnp.square(yc), axis=-1, keepdims=True)
    return yc * lax.rsqrt(var + LN_EPS) * g + b


def _mix_ffn_body(mix_ref, wo_ref, res_ref, g1_ref, b1_ref, w1_ref, w2_ref, g2_ref, b2_ref, o_ref, *,
                  ff_chunk, row_split):
    sub = mix_ref.shape[0] // row_split

    def stages(rows):
        st = {}

        def out_proj():
            st["y"] = DN_ALPHA * res_ref[rows, :] + _dot(mix_ref[rows, :], wo_ref[...])

        def norm1():
            st["x1"] = _layer_norm(st.pop("y"), g1_ref[...], b1_ref[...])
            st["x1b"] = st["x1"].astype(BF16)
            st["acc"] = DN_ALPHA * st.pop("x1")

        def up(f):
            st["hid"] = _dot(st["x1b"], w1_ref[:, f:f + ff_chunk])

        def act():
            st["hid"] = jnp.square(jnp.maximum(st["hid"], 0.0)).astype(BF16)

        def down(f):
            st["acc"] = st["acc"] + _dot(st.pop("hid"), w2_ref[f:f + ff_chunk, :])

        def norm2():
            o_ref[rows, :] = _layer_norm(st.pop("acc"), g2_ref[...], b2_ref[...])

        seq = [out_proj, norm1]
        for f in range(0, w1_ref.shape[1], ff_chunk):
            seq += [functools.partial(up, f), act, functools.partial(down, f)]
        return seq + [norm2]

    threads = [stages(slice(h * sub, (h + 1) * sub)) for h in range(row_split)]
    for t in range(len(threads[0]) + row_split - 1):
        for h, seq in enumerate(threads):
            if 0 <= t - h < len(seq):
                seq[t - h]()


def _mix_ffn(mix, w_out, res, g1, b1, w1, w2, g2, b2, *, tm, ff_chunk, row_split):
    m, k = mix.shape
    d = w_out.shape[1]
    dff = w1.shape[1]
    tm = min(tm, m)
    assert m % tm == 0 and dff % ff_chunk == 0 and tm % (SUBLANES * row_split) == 0
    resident = lambda shape: pl.BlockSpec(shape, lambda i: (0, 0), pipeline_mode=pl.Buffered(1))
    rows = lambda width: pl.BlockSpec((tm, width), lambda i: (i, 0))
    vec = lambda p: p.reshape(1, d)
    return pl.pallas_call(
        functools.partial(_mix_ffn_body, ff_chunk=ff_chunk, row_split=row_split),
        grid=(m // tm,),
        in_specs=[rows(k), resident((k, d)), rows(d), resident((1, d)), resident((1, d)),
                  resident((d, dff)), resident((dff, d)), resident((1, d)), resident((1, d))],
        out_specs=rows(d),
        out_shape=jax.ShapeDtypeStruct((m, d), F32),
        compiler_params=_cparams(("parallel",)),
        name="outproj_ln_ffn_ln",
    )(mix, w_out, res, vec(g1), vec(b1), w1, w2, vec(g2), vec(b2))


def _mlstm_body(q_ref, k_ref, v_ref, o_ref, gate_ref, ng_ref, h_ref, c_ref, n_ref, m_ref, *, lc):
    heads = range(M_HEADS)

    @pl.when(pl.program_id(1) == 0)
    def _():
        c_ref[...] = jnp.zeros_like(c_ref)
        n_ref[...] = jnp.zeros_like(n_ref)
        m_ref[...] = jnp.zeros_like(m_ref)

    ri = lax.broadcasted_iota(jnp.int32, (lc, lc), 0)
    ci = lax.broadcasted_iota(jnp.int32, (lc, lc), 1)
    causal = ci <= ri
    diag = ci == ri
    tri_u3 = jnp.concatenate([jnp.where(ri <= ci, 1.0, 0.0).astype(BF16)] * 3, axis=0)
    scale = M_DK ** -0.5
    m_prev = [m_ref[h, 0:1, 0:1] for h in heads]
    c_prev = [c_ref[h] for h in heads]
    n_prev = [n_ref[h, 0:1, :] for h in heads]

    for c0 in range(0, q_ref.shape[1], lc):
        rows = slice(c0, c0 + lc)
        g_rows = gate_ref[0, :, rows]
        cum_rows = _dot(jnp.concatenate(_split3_bf16(g_rows), axis=1), tri_u3)
        q_all = q_ref[0, rows, :]
        k_all = k_ref[0, rows, :]
        v_all = v_ref[0, rows, :]
        qb = [q_all[:, h * M_DK:(h + 1) * M_DK] for h in heads]
        kb = [k_all[:, h * M_DK:(h + 1) * M_DK] for h in heads]
        vb = [v_all[:, h * M_DV:(h + 1) * M_DV] for h in heads]
        q = [qb[h].astype(F32) for h in heads]
        k = [kb[h].astype(F32) for h in heads]
        i_row = [g_rows[h:h + 1, :] for h in heads]
        f_row = [g_rows[M_HEADS + h:M_HEADS + h + 1, :] for h in heads]
        bcum_row = [cum_rows[M_HEADS + h:M_HEADS + h + 1, :] for h in heads]
        bcum_col = [jnp.sum(jnp.where(causal, f_row[h], 0.0), axis=-1, keepdims=True) for h in heads]
        i_col = [jnp.sum(jnp.where(diag, i_row[h], 0.0), axis=-1, keepdims=True) for h in heads]

        log_d = [jnp.where(causal, bcum_col[h] - bcum_row[h] + i_row[h], -jnp.inf) for h in heads]
        log_inter = [bcum_col[h] + m_prev[h] for h in heads]
        m_row = [jnp.maximum(jnp.max(log_d[h], axis=-1, keepdims=True), log_inter[h]) for h in heads]
        inter = [jnp.exp(log_inter[h] - m_row[h]) for h in heads]
        s = [_dot_nt(qb[h], kb[h]) * (scale * jnp.exp(log_d[h] - m_row[h])) for h in heads]
        inter_s = [scale * inter[h] for h in heads]
        qc = [_dot(qb[h], c_prev[h].astype(BF16)) for h in heads]
        num = [_dot(s[h].astype(BF16), vb[h]) + inter_s[h] * qc[h] for h in heads]
        den = [jnp.sum(s[h], axis=-1, keepdims=True)
               + inter_s[h] * jnp.sum(q[h] * n_prev[h], axis=-1, keepdims=True) for h in heads]
        inv_den = [1.0 / jnp.maximum(jnp.abs(den[h]), jnp.exp(-m_row[h])) for h in heads]

        b_last = [bcum_col[h][lc - 1:lc, :] for h in heads]
        log_w = [b_last[h] - bcum_col[h] + i_col[h] for h in heads]
        m_new = [jnp.maximum(b_last[h] + m_prev[h], jnp.max(log_w[h], axis=0, keepdims=True)) for h in heads]
        decay = [jnp.exp(b_last[h] + m_prev[h] - m_new[h]) for h in heads]
        kw = [k[h] * jnp.exp(log_w[h] - m_new[h]) for h in heads]
        c_prev = [decay[h] * c_prev[h] + _dot_tn(kw[h].astype(BF16), vb[h]) for h in heads]
        n_prev = [decay[h] * n_prev[h] + jnp.sum(kw[h], axis=0, keepdims=True) for h in heads]
        m_prev = m_new

        msq = [jnp.mean(jnp.square(num[h]), axis=-1, keepdims=True) for h in heads]
        hn = jnp.concatenate([num[h] * (inv_den[h] * lax.rsqrt(jnp.square(inv_den[h]) * msq[h] + M_NORM_EPS))
                              for h in heads], axis=1)
        h_ref[0, rows, :] = (hn * ng_ref[...] * _sigmoid(o_ref[0, rows, :].astype(F32))).astype(h_ref.dtype)

    for h in heads:
        c_ref[h] = c_prev[h]
        n_ref[h] = jnp.broadcast_to(n_prev[h], n_ref.shape[1:])
        m_ref[h] = jnp.broadcast_to(m_prev[h], m_ref.shape[1:])


def _mlstm_core(proj, gates, norm_g):
    bsz, seq, _ = proj.shape
    tg = gates.shape[-1]
    lc = min(M_CHUNK, seq)
    tb = min(lc * M_CHUNKS_PER_STEP, seq)
    nc = seq // tb
    assert tg % tb == 0 and seq % tg == 0 and tb % lc == 0
    per_tile = tg // tb
    hdk = M_HEADS * M_DK
    hdv = M_HEADS * M_DV
    assert 2 * hdk == hdv
    return pl.pallas_call(
        functools.partial(_mlstm_body, lc=lc),
        grid=(bsz, nc),
        in_specs=[pl.BlockSpec((1, tb, hdk), lambda b, c: (b, c, 0)),
                  pl.BlockSpec((1, tb, hdk), lambda b, c: (b, c, 1)),
                  pl.BlockSpec((1, tb, hdv), lambda b, c: (b, c, 1)),
                  pl.BlockSpec((1, tb, hdv), lambda b, c: (b, c, 2)),
                  pl.BlockSpec((1, 2 * M_HEADS, tb), lambda b, c: (b * (seq // tg) + c // per_tile, 0, c % per_tile)),
                  pl.BlockSpec((1, hdv), lambda b, c: (0, 0))],
        out_specs=pl.BlockSpec((1, tb, hdv), lambda b, c: (b, c, 0)),
        out_shape=jax.ShapeDtypeStruct((bsz, seq, hdv), BF16),
        scratch_shapes=[pltpu.VMEM((M_HEADS, M_DK, M_DV), F32),
                        pltpu.VMEM((M_HEADS, SUBLANES, M_DK), F32),
                        pltpu.VMEM((M_HEADS, SUBLANES, LANES), F32)],
        compiler_params=_cparams(("parallel", "arbitrary")),
        name="mlstm_chunkwise",
    )(proj, proj, proj, proj, gates, norm_g.reshape(1, hdv))


def _mlstm_mixer(x, w_in, b_i, b_f, conv_w, conv_b, norm_g, riders):
    bsz, seq, d = x.shape
    m = bsz * seq
    hdv = M_HEADS * M_DV
    n_main = 2 * M_HEADS * M_DK + 2 * hdv
    xf = x.reshape(m, d)
    pad = LANES - 2 * M_HEADS
    w_gate = jnp.pad(w_in[:, n_main:], ((0, 0), (0, pad))).astype(BF16)
    b_gate = jnp.pad(jnp.concatenate([b_i, b_f]), (0, pad))
    proj, gates, casted = _proj_conv_silu(xf, w_in.astype(BF16), n_main, w_gate, b_gate, conv_w, conv_b, riders,
                                          seq=seq, tm=1024, col_chunk=256)
    proj = proj.reshape(bsz, seq, n_main)
    return _mlstm_core(proj, gates, norm_g), casted


def _wkv_body(r_ref, k_ref, v_ref, tail_ref, w2_ref, a2_ref, g2_ref, w0_ref, a0_ref, kk_ref, ka_ref, rk_ref,
              gng_ref, gnb_ref, o_ref, s_ref, *, n_tanh):
    tb = r_ref.shape[1]
    lc = min(R_CHUNK, tb)
    chunks = range(tb // lc)
    width = r_ref.shape[2]
    gw = R_GROUP * R_N
    sw = R_STATE_HEADS * R_N
    rows = R_GROUP * lc
    assert tb % lc == 0 and gw % sw == 0 and width % gw == 0

    @pl.when(pl.program_id(2) == 0)
    def _():
        s_ref[...] = jnp.zeros_like(s_ref)

    groups = range(width // gw)
    tiles = range(width // sw)
    tls = [slice(j * sw, (j + 1) * sw) for j in tiles]
    brow = lax.broadcasted_iota(jnp.int32, (sw, sw), 0) // R_N
    bcol = lax.broadcasted_iota(jnp.int32, (sw, sw), 1) // R_N
    same_head = brow == bcol
    head_ones = jnp.where(same_head, 1.0, 0.0).astype(BF16)

    def seg_sum(x, parts=1):
        pieces = (x.astype(BF16),) if parts == 1 else _split_bf16(x)
        ones = jnp.concatenate([head_ones] * len(pieces), axis=0)
        return jnp.concatenate([_dot(jnp.concatenate([piece[:, sl] for piece in pieces], axis=1), ones)
                                for sl in tls], axis=1)

    r = r_ref[0].astype(F32)
    k = k_ref[0].astype(F32)
    v = v_ref[0].astype(F32)
    n_wa = w2_ref.shape[0]
    tail = tail_ref[0].astype(F32)
    t_wa = tail[:, :n_wa]
    is_tanh = lax.broadcasted_iota(jnp.int32, (1, n_wa), 1) < n_tanh
    f_wa = jnp.where(is_tanh, jnp.tanh(t_wa), t_wa).astype(BF16)
    gate = _dot(_sigmoid(tail[:, n_wa:]).astype(BF16), g2_ref[...])
    ld = (-R_DECAY_SCALE) * _sigmoid(w0_ref[...] + _dot(f_wa, w2_ref[...]))
    a_lr = _sigmoid(a0_ref[...] + _dot(f_wa, a2_ref[...]))
    kk = k * kk_ref[...]
    kk = kk * lax.rsqrt(jnp.maximum(seg_sum(jnp.square(kk)), 1e-24))
    k = k * (a_lr * ka_ref[...] + (1.0 - ka_ref[...]))
    b_ = kk * a_lr

    ti = lax.broadcasted_iota(jnp.int32, (tb, tb), 0)
    tj = lax.broadcasted_iota(jnp.int32, (tb, tb), 1)
    in_chunk_tri = jnp.logical_and(tj <= ti, ti // lc == tj // lc)
    tri = jnp.where(in_chunk_tri, 1.0, 0.0).astype(BF16)
    cum = _dot(jnp.concatenate([tri, tri], axis=1), jnp.concatenate(_split_bf16(ld), axis=0))
    e_pos = jnp.exp(cum)
    e_neg = jnp.exp(-cum)
    w_end = [jnp.exp(cum[(c + 1) * lc - 1:(c + 1) * lc, :]) for c in chunks]
    a_til = -kk * jnp.exp(cum - ld)
    r_til = r * e_pos
    b_til = b_ * e_neg
    k_til = k * e_neg

    def stacker(n):
        lane_head = lax.broadcasted_iota(jnp.int32, (lc, n), 1) // (n // R_GROUP)
        sel = [lane_head == h for h in range(R_GROUP)]
        return lambda x: jnp.concatenate([jnp.where(s_, x, jnp.zeros_like(x)) for s_ in sel], axis=0)

    stack_c = stacker(gw)
    stack_t = stacker(rows)
    trow = lax.broadcasted_iota(jnp.int32, (lc, rows), 0)
    tcol = lax.broadcasted_iota(jnp.int32, (lc, rows), 1) % lc
    strict = tcol < trow
    causal = tcol <= trow
    eye = jnp.where(tcol == trow, 1.0, 0.0).astype(F32)
    level_masks = []
    s = 1
    while s < lc:
        lo, hi = s.bit_length() - 1, s.bit_length()
        level_masks.append(jnp.logical_and((trow >> hi) == (tcol >> hi), (trow >> lo) != (tcol >> lo)))
        s *= 2

    units = [(c, g) for c in chunks for g in groups]
    tile = lambda x, c, g: x[c * lc:(c + 1) * lc, g * gw:(g + 1) * gw]
    ar_b = {u: jnp.concatenate([tile(a_til, *u), tile(r_til, *u)], axis=0).astype(BF16) for u in units}
    bk_b = {u: (tile(b_til, *u).astype(BF16), tile(k_til, *u).astype(BF16)) for u in units}
    bk_s = {u: jnp.concatenate([stack_c(bk_b[u][0]), stack_c(bk_b[u][1])], axis=0) for u in units}
    v_b = {u: tile(v, *u).astype(BF16) for u in units}
    v_s = {u: stack_c(v_b[u]) for u in units}
    bk_cat = {u: jnp.concatenate(bk_b[u], axis=0) for u in units}

    pm = {u: _dot_nt(ar_b[u], bk_s[u]) for u in units}
    n_ab = {u: jnp.where(strict, pm[u][:lc, :rows], 0.0) for u in units}
    m_xk = {u: jnp.concatenate([jnp.where(strict, pm[u][:lc, rows:], 0.0),
                                jnp.where(causal, pm[u][lc:, rows:], 0.0)], axis=0).astype(BF16) for u in units}
    m_rb = {u: jnp.where(causal, pm[u][lc:, :rows], 0.0).astype(BF16) for u in units}

    n_b = {u: n_ab[u].astype(BF16) for u in units}
    zero_b = jnp.zeros((lc, rows), BF16)
    t_inv = {u: eye + jnp.where(level_masks[0], n_ab[u], 0.0) for u in units}
    for msk in level_masks[1:]:
        t_b = {u: t_inv[u].astype(BF16) for u in units}
        half = {u: _dot(t_b[u], stack_t(jnp.where(msk, n_b[u], zero_b))) for u in units}
        t_inv = {u: t_inv[u] + _dot(half[u].astype(BF16), stack_t(t_b[u])) for u in units}
    t_b = {u: t_inv[u].astype(BF16) for u in units}
    resid = {u: eye - t_b[u].astype(F32) + _dot(n_b[u], stack_t(t_b[u])) for u in units}
    t_fix = {u: _dot(t_b[u], stack_t(resid[u].astype(BF16))).astype(BF16) for u in units}

    state = [s_ref[j] for j in tiles]
    tiles_of = lambda g: range(g * (gw // sw), (g + 1) * (gw // sw))
    sub = lambda x, g, j: x[:, (j - g * (gw // sw)) * sw:(j - g * (gw // sw) + 1) * sw]
    y_rows = []
    for c in chunks:
        us = [(c, g) for g in groups]
        zy = [jnp.concatenate([_dot_nt(sub(ar_b[u], u[1], j), state[j].astype(BF16)) for j in tiles_of(u[1])], axis=1)
              + _dot(m_xk[u], v_s[u]) for u in us]
        ub = []
        for i, u in enumerate(us):
            z_s = stack_c(zy[i][:lc].astype(BF16))
            ub.append(_dot(jnp.concatenate([t_b[u], t_fix[u]], axis=1), jnp.concatenate([z_s, z_s], axis=0)).astype(BF16))
        y_rows.append(jnp.concatenate([zy[i][lc:] + _dot(m_rb[u], stack_c(ub[i])) for i, u in enumerate(us)], axis=1))
        for i, u in enumerate(us):
            for j in tiles_of(u[1]):
                upd = _dot_tn(jnp.concatenate([sub(ub[i], u[1], j), sub(v_b[u], u[1], j)], axis=0),
                              sub(bk_cat[u], u[1], j))
                state[j] = (state[j] + jnp.where(same_head, upd, 0.0)) * w_end[c][:, tls[j]]
    for j in tiles:
        s_ref[j] = state[j]
    y = jnp.concatenate(y_rows, axis=0)

    mu = seg_sum(y, parts=2) * (1.0 / R_N)
    yc = y - mu
    var = seg_sum(jnp.square(yc)) * (1.0 / R_N)
    yn = yc * lax.rsqrt(var + R_GN_EPS) * gng_ref[...] + gnb_ref[...]
    bonus = seg_sum(r * k * rk_ref[...]) * v
    o_ref[0] = ((yn + bonus) * gate).astype(o_ref.dtype)


def _wkv_core(proj, w2, a2, g2, w0, a0, k_k, k_a, r_k, gn_g, gn_b):
    bsz, seq, n_proj = proj.shape
    d = w0.shape[-1]
    lw, la, lg = w2.shape[0], a2.shape[0], g2.shape[0]
    n_tail = lw + la + lg
    assert n_proj == 3 * d + n_tail and (3 * d) % n_tail == 0
    gw = R_GROUP * R_N
    sw = R_STATE_HEADS * R_N
    width = R_GROUPS_PER_STEP * gw
    nb = d // width
    tb = min(R_CHUNK * R_CHUNKS_PER_STEP, seq)
    nc = seq // tb
    w2p = jnp.concatenate([w2, jnp.zeros_like(a2)], axis=0).astype(BF16)
    a2p = jnp.concatenate([jnp.zeros_like(w2), a2], axis=0).astype(BF16)
    tile = lambda off: pl.BlockSpec((1, tb, width), lambda b, g, c: (b, c, off * nb + g))
    tail = pl.BlockSpec((1, tb, n_tail), lambda b, g, c: (b, c, 3 * d // n_tail))
    wa = pl.BlockSpec((lw + la, width), lambda b, g, c: (0, g))
    wg = pl.BlockSpec((lg, width), lambda b, g, c: (0, g))
    par = pl.BlockSpec((1, width), lambda b, g, c: (0, g))
    row = lambda p: p.reshape(1, d)
    return pl.pallas_call(
        functools.partial(_wkv_body, n_tanh=lw),
        grid=(bsz, nb, nc),
        in_specs=[tile(0), tile(1), tile(2), tail, wa, wa, wg] + [par] * 7,
        out_specs=pl.BlockSpec((1, tb, width), lambda b, g, c: (b, c, g)),
        out_shape=jax.ShapeDtypeStruct((bsz, seq, d), BF16),
        scratch_shapes=[pltpu.VMEM((width // sw, sw, sw), F32)],
        compiler_params=_cparams(("parallel", "parallel", "arbitrary")),
        name="wkv7_chunked",
    )(proj, proj, proj, proj, w2p, a2p, g2.astype(BF16),
      row(w0), row(a0), row(k_k), row(k_a), row(r_k), row(gn_g), row(gn_b))


def _rwkv_mixer(x, w_in, mu, w0, w2, a0, a2, g2, k_k, k_a, r_k, gn_g, gn_b, riders):
    bsz, seq, d = x.shape
    n_proj = w_in.shape[1]
    proj, casted = _proj_shift_lerp(x.reshape(bsz * seq, d), w_in.astype(BF16), mu, riders,
                                    seq=seq, tm=1024, col_chunk=256)
    return _wkv_core(proj.reshape(bsz, seq, n_proj), w2, a2, g2, w0, a0, k_k, k_a, r_k, gn_g, gn_b), casted


def kernel(x, mlstm_w_in, mlstm_b_i, mlstm_b_f, mlstm_conv_w, mlstm_conv_b, mlstm_norm_g, mlstm_w_out, rwkv_w_in, rwkv_mu, rwkv_w0, rwkv_w2, rwkv_a0, rwkv_a2, rwkv_g2, rwkv_k_k, rwkv_k_a, rwkv_r_k, rwkv_gn_g, rwkv_gn_b, rwkv_w_out, ln_mix_g, ln_mix_b, mlp_w1, mlp_w2, ln_ffn_g, ln_ffn_b):
    bsz, seq, d = x.shape
    m = bsz * seq
    xf = x.reshape(m, d)
    for layer in range(DEPTH):
        j = layer // 2
        riders = [(mlp_w1, layer), (mlp_w2, layer)]
        if layer % 2 == 0:
            if layer + 1 < DEPTH:
                riders.append((rwkv_w_in, (layer + 1) // 2))
            mix, casted = _mlstm_mixer(xf.reshape(bsz, seq, d), mlstm_w_in[j], mlstm_b_i[j], mlstm_b_f[j],
                                       mlstm_conv_w[j], mlstm_conv_b[j], mlstm_norm_g[j], riders)
            w1, w2 = casted[:2]
            w_out = mlstm_w_out[j]
        else:
            w_in = casted[2] if len(casted) > 2 else rwkv_w_in[j]
            mix, (w1, w2) = _rwkv_mixer(xf.reshape(bsz, seq, d), w_in, rwkv_mu[j], rwkv_w0[j], rwkv_w2[j],
                                        rwkv_a0[j], rwkv_a2[j], rwkv_g2[j], rwkv_k_k[j], rwkv_k_a[j],
                                        rwkv_r_k[j].reshape(-1), rwkv_gn_g[j], rwkv_gn_b[j], riders)
            w_out = rwkv_w_out[j]
        xf = _mix_ffn(mix.reshape(m, -1), w_out.astype(BF16), xf, ln_mix_g[layer], ln_mix_b[layer],
                      w1, w2, ln_ffn_g[layer], ln_ffn_b[layer], tm=512, ff_chunk=1024, row_split=2)
    return xf.reshape(bsz, seq, d)
```
